```python
import jax
import jax.numpy as jnp
from jax import lax
import numpy as np

D_MODEL = 1024
BATCH = 8
SEQ = 2048
DEPTH = 4

GRID_W = 64
CTX_LEN = 256
N_EVEN = (DEPTH + 1) // 2
N_ODD = DEPTH // 2
D_FF = 4 * D_MODEL
N_MOD = 6
EPS = 1e-6
F32 = jnp.float32

MLA_HEADS = 8
MLA_NOPE = 64
MLA_ROPE = 32
MLA_V = 64
MLA_QK = MLA_NOPE + MLA_ROPE
MLA_Q_RANK = 3 * D_MODEL // 8
MLA_KV_RANK = D_MODEL // 4
ROPE_THETA = 10000.0
Q_BLOCK = 128

GLA_HEADS = 4
GLA_DK = 64
GLA_DV = 128
GLA_GATE_RANK = 16
GLA_GATE_NORM = 16.0
GLA_CHUNK = 64

GDN_HEADS = 8
GDN_DK = 64
GDN_DV = 128
GDN_CONV = 3
GDN_CHUNK = 64

EVEN_SPLITS = (MLA_Q_RANK, MLA_KV_RANK, MLA_ROPE,
               GLA_HEADS * GLA_DK, GLA_HEADS * GLA_DK, GLA_HEADS * GLA_DV, GLA_HEADS * GLA_DV,
               2 * GLA_GATE_RANK)
EVEN_IN = 2240
MIX_WIDTH = MLA_HEADS * MLA_V + GLA_HEADS * GLA_DV
GDN_QKV = 2 * GDN_HEADS * GDN_DK + GDN_HEADS * GDN_DV
ODD_SPLITS = (GDN_QKV, GDN_HEADS * GDN_DV, 2 * GDN_HEADS, 2 * GDN_HEADS)
ODD_IN = 3104
GDN_WIDTH = GDN_HEADS * GDN_DV

kernel_name = "hybrid_mla_gla_gdn_dit_prefix"


def rms_norm(x, w):
    xf = x.astype(F32)
    y = xf * lax.rsqrt(jnp.mean(xf * xf, axis=-1, keepdims=True) + EPS)
    return (y * w.astype(F32)).astype(x.dtype)


def l2norm(t):
    tf = t.astype(F32)
    return (tf * lax.rsqrt(jnp.sum(tf * tf, axis=-1, keepdims=True) + EPS)).astype(t.dtype)


def split_cols(y, sizes):
    out, start = [], 0
    for s in sizes:
        out.append(y[..., start:start + s])
        start += s
    return out


def modulate(h, shift, scale):
    return h * (1.0 + scale) + shift


def flip_seq(ts):
    return tuple(jnp.flip(t, axis=2) for t in ts)


def axial_rope_tables(n_tokens):
    rows = n_tokens // GRID_W
    row = jnp.repeat(jnp.arange(rows, dtype=F32), GRID_W)
    col = jnp.tile(jnp.arange(GRID_W, dtype=F32), rows)
    axis_dim = MLA_ROPE // 2
    inv_freq = ROPE_THETA ** (-jnp.arange(0, axis_dim, 2, dtype=F32) / axis_dim)
    ang = jnp.concatenate([row[:, None] * inv_freq, col[:, None] * inv_freq], axis=-1)
    return jnp.cos(ang), jnp.sin(ang)


def apply_axial_rope(x, cos, sin):
    shp = x.shape
    nf = MLA_ROPE // 4
    xf = x.astype(F32).reshape(shp[:-1] + (2, 2, nf))
    x1, x2 = xf[..., 0, :], xf[..., 1, :]
    cos = cos.reshape(-1, 2, nf)
    sin = sin.reshape(-1, 2, nf)
    out = jnp.stack([x1 * cos - x2 * sin, x2 * cos + x1 * sin], axis=-2)
    return out.reshape(shp).astype(x.dtype)


def mla_attend(q, k, v):
    B, H, Lq, D = q.shape
    scale = MLA_QK ** -0.5
    nb = Lq // Q_BLOCK
    qb = jnp.moveaxis(q.reshape(B, H, nb, Q_BLOCK, D), 2, 0)

    def one(qblk):
        s = jnp.einsum('bhqd,bhkd->bhqk', qblk, k).astype(F32) * scale
        p = jax.nn.softmax(s, axis=-1).astype(v.dtype)
        return jnp.einsum('bhqk,bhkv->bhqv', p, v)

    o = lax.map(one, qb)
    return jnp.moveaxis(o, 0, 2).reshape(B, H, Lq, v.shape[-1])


def gla_chunked(q, k, v, logg, s0):
    out_dtype = v.dtype
    q, k, v, logg = (t.astype(F32) for t in (q, k, v, logg))
    B, H, L, DK = q.shape
    DV = v.shape[-1]
    C = GLA_CHUNK
    n = L // C
    q = q.reshape(B, H, n, C, DK) * DK ** -0.5
    k = k.reshape(B, H, n, C, DK)
    v = v.reshape(B, H, n, C, DV)
    b = jnp.cumsum(logg.reshape(B, H, n, C, DK), axis=3)
    q_dec = q * jnp.exp(b)
    k_inv = k * jnp.exp(-b)
    k_end = k * jnp.exp(b[..., -1:, :] - b)
    g_end = jnp.exp(b[..., -1, :])
    incl = jnp.tril(jnp.ones((C, C), bool))
    a = jnp.where(incl, jnp.einsum('bhnid,bhnjd->bhnij', q_dec, k_inv), 0.0)
    o_intra = jnp.einsum('bhnij,bhnjv->bhniv', a, v)

    def step(s, xs):
        qd_c, ke_c, v_c, ge_c = xs
        o = jnp.einsum('bhcd,bhdv->bhcv', qd_c, s)
        s = s * ge_c[..., None] + jnp.einsum('bhcd,bhcv->bhdv', ke_c, v_c)
        return s, o

    xs = tuple(jnp.moveaxis(t, 2, 0) for t in (q_dec, k_end, v, g_end))
    s_fin, o_inter = lax.scan(step, s0, xs)
    o = o_intra + jnp.moveaxis(o_inter, 0, 2)
    return o.reshape(B, H, L, DV).astype(out_dtype), s_fin


def gdn_chunked(q, k, v, log_a, beta, s0):
    out_dtype = v.dtype
    q, k, v, log_a, beta = (t.astype(F32) for t in (q, k, v, log_a, beta))
    B, H, L, DK = q.shape
    DV = v.shape[-1]
    C = GDN_CHUNK
    n = L // C
    q = q.reshape(B, H, n, C, DK) * DK ** -0.5
    k = k.reshape(B, H, n, C, DK)
    v = v.reshape(B, H, n, C, DV)
    beta = beta.reshape(B, H, n, C)[..., None]
    g = jnp.cumsum(log_a.reshape(B, H, n, C), axis=-1)
    incl = jnp.tril(jnp.ones((C, C), bool))
    strict = jnp.tril(jnp.ones((C, C), bool), -1)
    diff = g[..., :, None] - g[..., None, :]
    decay = jnp.where(incl, jnp.exp(jnp.where(incl, diff, 0.0)), 0.0)
    kb = k * beta
    a_mat = jnp.where(strict, jnp.einsum('bhnid,bhnjd->bhnij', kb, k) * decay, 0.0) + jnp.eye(C, dtype=F32)
    rhs = jnp.concatenate([v * beta, kb * jnp.exp(g)[..., None]], axis=-1)
    sol = lax.linalg.triangular_solve(a_mat, rhs, left_side=True, lower=True)
    u, w = sol[..., :DV], sol[..., DV:]
    qk = jnp.einsum('bhnid,bhnjd->bhnij', q, k) * decay
    q_dec = q * jnp.exp(g)[..., None]
    k_end = k * jnp.exp(g[..., -1:] - g)[..., None]
    g_end = jnp.exp(g[..., -1])

    def step(s, xs):
        qk_c, qd_c, w_c, u_c, ke_c, ge_c = xs
        v_new = u_c - jnp.einsum('bhcd,bhdv->bhcv', w_c, s)
        o = jnp.einsum('bhcd,bhdv->bhcv', qd_c, s) + jnp.einsum('bhcs,bhsv->bhcv', qk_c, v_new)
        s = s * ge_c[..., None, None] + jnp.einsum('bhcd,bhcv->bhdv', ke_c, v_new)
        return s, o

    xs = tuple(jnp.moveaxis(t, 2, 0) for t in (qk, q_dec, w, u, k_end, g_end))
    s_fin, o = lax.scan(step, s0, xs)
    o = jnp.moveaxis(o, 0, 2).reshape(B, H, L, DV)
    return o.astype(out_dtype), s_fin


def bidir_scan(scan_fn, ctx_f, ctx_b, lat_f, lat_b, s0):
    o_cf, s_f = scan_fn(*ctx_f, s0)
    o_cb, s_b = scan_fn(*flip_seq(ctx_b), s0)
    o_lf, _ = scan_fn(*lat_f, s_f)
    o_lb, _ = scan_fn(*flip_seq(lat_b), s_b)
    return o_lf + jnp.flip(o_lb, axis=2), (o_cf, o_cb)


def even_project(y, q_norm, w_uq, kv_norm, w_ukv, gate_up, gate_bias, rope):
    cq, ckv, kpe, gq, gk, gv, gg, glow = split_cols(y, EVEN_SPLITS)
    B, L, _ = y.shape
    q = (rms_norm(cq, q_norm) @ w_uq).reshape(B, L, MLA_HEADS, MLA_QK).transpose(0, 2, 1, 3)
    q_nope, q_pe = q[..., :MLA_NOPE], q[..., MLA_NOPE:]
    kv = (rms_norm(ckv, kv_norm) @ w_ukv).reshape(B, L, MLA_HEADS, MLA_NOPE + MLA_V).transpose(0, 2, 1, 3)
    k_nope, mv = kv[..., :MLA_NOPE], kv[..., MLA_NOPE:]
    if rope is not None:
        q_pe = apply_axial_rope(q_pe, *rope)
        kpe = apply_axial_rope(kpe, *rope)
    mq = jnp.concatenate([q_nope, q_pe], axis=-1)
    mk = jnp.concatenate([k_nope, jnp.broadcast_to(kpe[:, None], (B, MLA_HEADS, L, MLA_ROPE))], axis=-1)
    glow = glow.reshape(B, L, 2, GLA_GATE_RANK)
    logit = jnp.einsum('blzr,zrk->zblk', glow, gate_up) + gate_bias[:, None, None, :]
    logg = jax.nn.log_sigmoid(logit.astype(F32)) / GLA_GATE_NORM
    logg = logg.reshape(2, B, L, GLA_HEADS, GLA_DK).transpose(0, 1, 3, 2, 4)
    heads = lambda t, d: t.reshape(B, L, -1, d).transpose(0, 2, 1, 3)
    return (mq, mk, mv, heads(gq, GLA_DK), heads(gk, GLA_DK), heads(gv, GLA_DV), gg, logg[0], logg[1])


def even_merge(att, gla, gg, o_norm, w_out):
    B, _, L, _ = att.shape
    att = att.transpose(0, 2, 1, 3).reshape(B, L, MLA_HEADS * MLA_V)
    gla = rms_norm(gla.transpose(0, 2, 1, 3), o_norm) * jax.nn.silu(gg.reshape(B, L, GLA_HEADS, GLA_DV))
    return jnp.concatenate([att, gla.reshape(B, L, GLA_HEADS * GLA_DV)], axis=-1) @ w_out


def even_mixer(a_lat, a_ctx, w_in, q_norm, w_uq, kv_norm, w_ukv, gate_up, gate_bias, o_norm, w_out, rope, need_ctx):
    mq_l, mk_l, mv_l, gq_l, gk_l, gv_l, gg_l, lf_l, lb_l = even_project(
        a_lat @ w_in, q_norm, w_uq, kv_norm, w_ukv, gate_up, gate_bias, rope)
    mq_c, mk_c, mv_c, gq_c, gk_c, gv_c, gg_c, lf_c, lb_c = even_project(
        a_ctx @ w_in, q_norm, w_uq, kv_norm, w_ukv, gate_up, gate_bias, None)
    att_l = mla_attend(mq_l, jnp.concatenate([mk_l, mk_c], axis=2), jnp.concatenate([mv_l, mv_c], axis=2))
    B = a_lat.shape[0]
    s0 = jnp.zeros((B, GLA_HEADS, GLA_DK, GLA_DV), F32)
    gla_l, (o_cf, o_cb) = bidir_scan(gla_chunked, (gq_c, gk_c, gv_c, lf_c), (gq_c, gk_c, gv_c, lb_c),
                                     (gq_l, gk_l, gv_l, lf_l), (gq_l, gk_l, gv_l, lb_l), s0)
    o_lat = even_merge(att_l, gla_l, gg_l, o_norm, w_out)
    o_ctx = None
    if need_ctx:
        att_c = mla_attend(mq_c, mk_c, mv_c)
        o_ctx = even_merge(att_c, o_cf + jnp.flip(o_cb, axis=2), gg_c, o_norm, w_out)
    return o_lat, o_ctx


def short_conv(x, w):
    ch = x.shape[-1]
    y = lax.conv_general_dilated(x, w[:, None, :], window_strides=(1,),
                                 padding=[((GDN_CONV - 1) // 2, GDN_CONV // 2)],
                                 dimension_numbers=('NWC', 'WIO', 'NWC'), feature_group_count=ch)
    return jax.nn.silu(y)


def odd_project(a, w_in, conv_w, a_log, dt_bias):
    B, L, _ = a.shape
    qkv, g, a_in, b_in = split_cols(a @ w_in, ODD_SPLITS)
    q, k, v = split_cols(short_conv(qkv, conv_w), (GDN_HEADS * GDN_DK, GDN_HEADS * GDN_DK, GDN_HEADS * GDN_DV))
    heads = lambda t, d: t.reshape(B, L, GDN_HEADS, d).transpose(0, 2, 1, 3)
    q = l2norm(heads(q, GDN_DK))
    k = l2norm(heads(k, GDN_DK))
    v = heads(v, GDN_DV)
    a_in = a_in.reshape(B, L, 2, GDN_HEADS).astype(F32)
    log_a = -jnp.exp(a_log.astype(F32)) * jax.nn.softplus(a_in + dt_bias.astype(F32))
    beta = jax.nn.sigmoid(b_in.reshape(B, L, 2, GDN_HEADS).astype(F32))
    return q, k, v, g, log_a.transpose(2, 0, 3, 1), beta.transpose(2, 0, 3, 1)


def odd_merge(o, g, o_norm, w_out):
    B, _, L, _ = o.shape
    o = rms_norm(o.transpose(0, 2, 1, 3), o_norm) * jax.nn.silu(g.reshape(B, L, GDN_HEADS, GDN_DV))
    return o.reshape(B, L, GDN_WIDTH) @ w_out


def odd_mixer(a_lat, a_ctx, w_in, conv_w, a_log, dt_bias, o_norm, w_out, need_ctx):
    lq, lk, lv, lg, lla, lbe = odd_project(a_lat, w_in, conv_w, a_log, dt_bias)
    cq, ck, cv, cg, cla, cbe = odd_project(a_ctx, w_in, conv_w, a_log, dt_bias)
    B = a_lat.shape[0]
    s0 = jnp.zeros((B, GDN_HEADS, GDN_DK, GDN_DV), F32)
    o_l, (o_cf, o_cb) = bidir_scan(gdn_chunked, (cq, ck, cv, cla[0], cbe[0]), (cq, ck, cv, cla[1], cbe[1]),
                                   (lq, lk, lv, lla[0], lbe[0]), (lq, lk, lv, lla[1], lbe[1]), s0)
    o_lat = odd_merge(o_l, lg, o_norm, w_out)
    o_ctx = None
    if need_ctx:
        o_ctx = odd_merge(o_cf + jnp.flip(o_cb, axis=2), cg, o_norm, w_out)
    return o_lat, o_ctx


def sq_relu_mlp(h, w1, w2):
    return jnp.square(jax.nn.relu(h @ w1)) @ w2


def setup_inputs(seed: int = 0) -> dict:
    key = jax.random.key(seed)
    ks = iter(jax.random.split(key, 40))
    nrm = lambda shape, scale: jax.random.normal(next(ks), shape, F32) * scale
    gain = lambda shape: 1.0 + 0.05 * jax.random.normal(next(ks), shape, F32)
    d = D_MODEL
    dt = jnp.exp(jax.random.uniform(next(ks), (N_ODD, 2, GDN_HEADS), F32, np.log(1e-3), np.log(1e-1)))
    return {
        "x": nrm((BATCH, SEQ, d), 1.0),
        "c": nrm((BATCH, d), 1.0),
        "ctx": nrm((BATCH, CTX_LEN, d), 1.0),
        "c_ctx": nrm((d,), 1.0),
        "ada_w": nrm((DEPTH, d, N_MOD * d), 0.5 * d ** -0.5),
        "ada_b": nrm((DEPTH, N_MOD * d), 0.02),
        "norm1_w": gain((DEPTH, d)),
        "norm2_w": gain((DEPTH, d)),
        "mlp_w1": nrm((DEPTH, d, D_FF), d ** -0.5),
        "mlp_w2": nrm((DEPTH, D_FF, d), D_FF ** -0.5),
        "even_w_in": nrm((N_EVEN, d, EVEN_IN), d ** -0.5),
        "mla_q_norm": gain((N_EVEN, MLA_Q_RANK)),
        "mla_w_uq": nrm((N_EVEN, MLA_Q_RANK, MLA_HEADS * MLA_QK), MLA_Q_RANK ** -0.5),
        "mla_kv_norm": gain((N_EVEN, MLA_KV_RANK)),
        "mla_w_ukv": nrm((N_EVEN, MLA_KV_RANK, MLA_HEADS * (MLA_NOPE + MLA_V)), MLA_KV_RANK ** -0.5),
        "gla_gate_up": nrm((N_EVEN, 2, GLA_GATE_RANK, GLA_HEADS * GLA_DK), GLA_GATE_RANK ** -0.5),
        "gla_gate_bias": nrm((N_EVEN, 2, GLA_HEADS * GLA_DK), 0.1),
        "gla_o_norm": gain((N_EVEN, GLA_DV)),
        "even_w_out": nrm((N_EVEN, MIX_WIDTH, d), MIX_WIDTH ** -0.5),
        "gdn_w_in": nrm((N_ODD, d, ODD_IN), d ** -0.5),
        "gdn_conv_w": nrm((N_ODD, GDN_CONV, GDN_QKV), GDN_CONV ** -0.5),
        "gdn_a_log": jnp.log(jax.random.uniform(next(ks), (N_ODD, 2, GDN_HEADS), F32, 1.0, 16.0)),
        "gdn_dt_bias": dt + jnp.log(-jnp.expm1(-dt)),
        "gdn_o_norm": gain((N_ODD, GDN_DV)),
        "gdn_w_out": nrm((N_ODD, GDN_WIDTH, d), GDN_WIDTH ** -0.5),
        "final_norm": gain((d,)),
    }


def reference(x, c, ctx, c_ctx, ada_w, ada_b, norm1_w, norm2_w, mlp_w1, mlp_w2,
              even_w_in, mla_q_norm, mla_w_uq, mla_kv_norm, mla_w_ukv, gla_gate_up, gla_gate_bias,
              gla_o_norm, even_w_out, gdn_w_in, gdn_conv_w, gdn_a_log, gdn_dt_bias, gdn_o_norm,
              gdn_w_out, final_norm):
    rope = axial_rope_tables(x.shape[1])
    h_lat, h_ctx = x, ctx
    for layer in range(DEPTH):
        need_ctx = layer < DEPTH - 1
        ml = [m[:, None, :] for m in jnp.split(jax.nn.silu(c) @ ada_w[layer] + ada_b[layer], N_MOD, axis=-1)]
        mc = jnp.split(jax.nn.silu(c_ctx) @ ada_w[layer] + ada_b[layer], N_MOD, axis=-1)
        a_lat = modulate(rms_norm(h_lat, norm1_w[layer]), ml[0], ml[1])
        a_ctx = modulate(rms_norm(h_ctx, norm1_w[layer]), mc[0], mc[1])
        i = layer // 2
        if layer % 2 == 0:
            o_lat, o_ctx = even_mixer(a_lat, a_ctx, even_w_in[i], mla_q_norm[i], mla_w_uq[i], mla_kv_norm[i],
                                      mla_w_ukv[i], gla_gate_up[i], gla_gate_bias[i], gla_o_norm[i],
                                      even_w_out[i], rope, need_ctx)
        else:
            o_lat, o_ctx = odd_mixer(a_lat, a_ctx, gdn_w_in[i], gdn_conv_w[i], gdn_a_log[i], gdn_dt_bias[i],
                                     gdn_o_norm[i], gdn_w_out[i], need_ctx)
        h_lat = h_lat + ml[2] * o_lat
        h_lat = h_lat + ml[5] * sq_relu_mlp(modulate(rms_norm(h_lat, norm2_w[layer]), ml[3], ml[4]),
                                            mlp_w1[layer], mlp_w2[layer])
        if need_ctx:
            h_ctx = h_ctx + mc[2] * o_ctx
            h_ctx = h_ctx + mc[5] * sq_relu_mlp(modulate(rms_norm(h_ctx, norm2_w[layer]), mc[3], mc[4]),
                                                mlp_w1[layer], mlp_w2[layer])
    return rms_norm(h_lat, final_norm)
```

```python
import functools

import jax
import jax.numpy as jnp
from jax import lax
from jax.experimental import pallas as pl
from jax.experimental.pallas import tpu as pltpu

F32 = jnp.float32
BF16 = jnp.bfloat16
EPS = 1e-6

GRID_W = 64
N_MOD = 6
MLA_HEADS = 8
MLA_NOPE = 64
MLA_ROPE = 32
MLA_V = 64
MLA_QK = MLA_NOPE + MLA_ROPE
MLA_Q_RANK = 384
MLA_KV_RANK = 256
ROPE_THETA = 10000.0
GLA_HEADS = 4
GLA_DK = 64
GLA_DV = 128
GLA_GATE_RANK = 16
GLA_GATE_NORM = 16.0
GDN_HEADS = 8
GDN_DK = 64
GDN_DV = 128
CHUNK = 64
LANES = 128
HEAD_SLOT = 128
VMEM_LIMIT = 56 * 1024 * 1024


def _cparams(sem):
    return pltpu.CompilerParams(dimension_semantics=sem, vmem_limit_bytes=VMEM_LIMIT)


def _dot(a, b):
    return jnp.dot(a.astype(BF16), b.astype(BF16), preferred_element_type=F32)


def _dot_nt(a, b):
    return lax.dot_general(a.astype(BF16), b.astype(BF16), (((1,), (1,)), ((), ())),
                           preferred_element_type=F32)


def _dot_tn(a, b):
    return lax.dot_general(a.astype(BF16), b.astype(BF16), (((0,), (0,)), ((), ())),
                           preferred_element_type=F32)


def _split(x, n):
    out = []
    for _ in range(n - 1):
        p = x.astype(BF16)
        out.append(p)
        x = x - p.astype(F32)
    out.append(x.astype(BF16))
    return out


def _dot3(a, b):
    a1, a2 = _split(a, 2)
    b1, b2 = _split(b, 2)
    d = functools.partial(jnp.dot, preferred_element_type=F32)
    return d(a1, b1) + (d(a1, b2) + d(a2, b1))


def _dot_exact_lhs(m_bf, x, nt=False):
    acc = None
    for p in _split(x, 3):
        if nt:
            t = lax.dot_general(p, m_bf, (((1,), (1,)), ((), ())), preferred_element_type=F32)
        else:
            t = jnp.dot(m_bf, p, preferred_element_type=F32)
        acc = t if acc is None else acc + t
    return acc


def _rms(x, w):
    return x * lax.rsqrt(jnp.mean(x * x, axis=-1, keepdims=True) + EPS) * w


def _sigmoid(x):
    return 1.0 / (1.0 + jnp.exp(-x))


def _silu(x):
    return x * _sigmoid(x)


def _softplus(x):
    return jnp.maximum(x, 0.0) + jnp.log1p(jnp.exp(-jnp.abs(x)))


def _tile_lanes(x, n):
    return jnp.concatenate([x] * n, axis=-1)


def _ada_kernel(cc_ref, w_ref, b_ref, o_ref):
    o_ref[0] = _dot(_silu(cc_ref[...]), w_ref[0]) + b_ref[0]


def _ada_all(cc, ada_w, ada_b):
    depth, d, n = ada_w.shape
    rows = cc.shape[0]
    tn = 1024
    return pl.pallas_call(
        _ada_kernel,
        grid=(depth, n // tn),
        in_specs=[pl.BlockSpec((rows, d), lambda l, j: (0, 0)),
                  pl.BlockSpec((1, d, tn), lambda l, j: (l, 0, j)),
                  pl.BlockSpec((1, 1, tn), lambda l, j: (l, 0, j))],
        out_specs=pl.BlockSpec((1, rows, tn), lambda l, j: (l, 0, j)),
        out_shape=jax.ShapeDtypeStruct((depth, rows, n), F32),
        compiler_params=_cparams(("parallel", "parallel")),
        name="ada_mod",
    )(cc, ada_w, ada_b.reshape(depth, 1, n))


_E_CQ = (0, 384)
_E_CKV = (384, 640)
_E_KP = (640, 768)
_E_KPR = (768, 896)
_E_GQ = (896, 1152)
_E_GK = (1152, 1408)
_E_GV = (1408, 1920)
_E_GG = (1920, 2432)
_E_GLOW = (2432, 2560)
_E_N = 2560


def _even_proj_kernel(h_ref, mod_ref, n1_ref, win_ref, qn_ref, wq_ref, kvn_ref, wk_ref, wv_ref,
                      wg_ref, gb_ref, c_ref, s_ref,
                      q_out, k_out, v_out, gq_out, gk_out, gv_out, gg_out, lg_out):
    mod = mod_ref[...]
    a = _rms(h_ref[...], n1_ref[...]) * (1.0 + mod[1:2]) + mod[0:1]
    y = _dot(a, win_ref[...])
    sl = lambda r: y[:, r[0]:r[1]]
    ct, st = c_ref[...], s_ref[...]
    nh = MLA_HEADS
    qq = _dot(_rms(sl(_E_CQ), qn_ref[...]), wq_ref[...])
    wq = nh * HEAD_SLOT
    q_out[...] = (qq[:, :wq] * _tile_lanes(ct, nh) + qq[:, wq:] * _tile_lanes(st, nh)).astype(q_out.dtype)
    ckvn = _rms(sl(_E_CKV), kvn_ref[...])
    kpe = sl(_E_KP) * ct + sl(_E_KPR) * st
    k_out[...] = (_dot(ckvn, wk_ref[...]) + _tile_lanes(kpe, nh)).astype(k_out.dtype)
    v_out[...] = _dot(ckvn, wv_ref[...]).astype(v_out.dtype)
    gq_out[...] = sl(_E_GQ)
    gk_out[...] = sl(_E_GK)
    gv_out[...] = sl(_E_GV)
    gg_out[...] = sl(_E_GG)
    logit = _dot(sl(_E_GLOW), wg_ref[...]) + gb_ref[...]
    lg_out[...] = (jnp.minimum(logit, 0.0) - jnp.log1p(jnp.exp(-jnp.abs(logit)))) * (1.0 / GLA_GATE_NORM)


def _rot_half_cols(w):
    f = MLA_ROPE // 4
    return jnp.concatenate([-w[..., f:2 * f], w[..., 0:f], -w[..., 3 * f:4 * f], w[..., 2 * f:3 * f]], axis=-1)


def _even_weights(w_in, w_uq, w_ukv, gate_up, gate_bias):
    d = w_in.shape[0]
    cq, ckv, kpe, gq, gk, gv, gg, glow = jnp.split(
        w_in, [384, 640, 672, 928, 1184, 1696, 2208], axis=1)
    z = lambda n: jnp.zeros((d, n), F32)
    slot = lambda w: jnp.concatenate([z(MLA_NOPE), w, z(HEAD_SLOT - MLA_QK)], axis=1)
    win = jnp.concatenate([cq, ckv, slot(kpe), slot(_rot_half_cols(kpe)), gq, gk, gv, gg, glow,
                           z(LANES - 2 * GLA_GATE_RANK)], axis=1).astype(BF16)
    r = w_uq.shape[0]
    wq = (w_uq * (MLA_QK ** -0.5)).reshape(r, MLA_HEADS, MLA_QK)
    nope, pe = wq[..., :MLA_NOPE], wq[..., MLA_NOPE:]
    zq = lambda n: jnp.zeros((r, MLA_HEADS, n), F32)
    pad = HEAD_SLOT - MLA_QK
    wq_a = jnp.concatenate([nope, pe, zq(pad)], axis=-1).reshape(r, -1)
    wq_b = jnp.concatenate([zq(MLA_NOPE), _rot_half_cols(pe), zq(pad)], axis=-1).reshape(r, -1)
    wq_all = jnp.concatenate([wq_a, wq_b], axis=1).astype(BF16)
    rk = w_ukv.shape[0]
    wkv = w_ukv.reshape(rk, MLA_HEADS, MLA_NOPE + MLA_V)
    wk = jnp.concatenate([wkv[..., :MLA_NOPE], jnp.zeros((rk, MLA_HEADS, HEAD_SLOT - MLA_NOPE), F32)],
                         axis=-1).reshape(rk, -1).astype(BF16)
    wv = wkv[..., MLA_NOPE:].reshape(rk, -1).astype(BF16)
    hk = GLA_HEADS * GLA_DK
    wg = jnp.zeros((LANES, 2 * hk), F32)
    wg = wg.at[0:GLA_GATE_RANK, 0:hk].set(gate_up[0])
    wg = wg.at[GLA_GATE_RANK:2 * GLA_GATE_RANK, hk:].set(gate_up[1])
    gb = jnp.concatenate([gate_bias[0], gate_bias[1]])[None, :]
    return win, wq_all, wk, wv, wg.astype(BF16), gb


def _rope_tables(n_lat, n_ctx):
    rows = n_lat // GRID_W
    row = jnp.repeat(jnp.arange(rows, dtype=F32), GRID_W)
    col = jnp.tile(jnp.arange(GRID_W, dtype=F32), rows)
    axis_dim = MLA_ROPE // 2
    inv_freq = ROPE_THETA ** (-jnp.arange(0, axis_dim, 2, dtype=F32) / axis_dim)
    ar, ac = row[:, None] * inv_freq, col[:, None] * inv_freq
    c32 = jnp.concatenate([jnp.cos(ar), jnp.cos(ar), jnp.cos(ac), jnp.cos(ac)], axis=1)
    s32 = jnp.concatenate([jnp.sin(ar), jnp.sin(ar), jnp.sin(ac), jnp.sin(ac)], axis=1)
    pad = HEAD_SLOT - MLA_QK
    ct = jnp.concatenate([jnp.ones((n_lat, MLA_NOPE), F32), c32, jnp.zeros((n_lat, pad), F32)], axis=1)
    st = jnp.concatenate([jnp.zeros((n_lat, MLA_NOPE), F32), s32, jnp.zeros((n_lat, pad), F32)], axis=1)
    cc = jnp.concatenate([jnp.ones((n_ctx, MLA_QK), F32), jnp.zeros((n_ctx, pad), F32)], axis=1)
    return jnp.concatenate([ct, cc], axis=0), jnp.concatenate([st, jnp.zeros((n_ctx, HEAD_SLOT), F32)], axis=0)


def _full(shape):
    nd = len(shape)
    return pl.BlockSpec(shape, lambda *_: (0,) * nd)


def _even_project(h, mod, n1, ew, qn, kvn, tabs, tm, n_lat):
    b, t, d = h.shape
    win, wq, wk, wv, wg, gb = ew
    ct, st = tabs
    nlb = n_lat // tm
    tok = lambda n: pl.BlockSpec((None, tm, n), lambda i, j: (i, j, 0))
    tab = pl.BlockSpec((tm, HEAD_SLOT), lambda i, j: (j, 0))
    hk, hv = GLA_HEADS * GLA_DK, GLA_HEADS * GLA_DV
    outs = [(MLA_HEADS * HEAD_SLOT, BF16), (MLA_HEADS * HEAD_SLOT, BF16), (MLA_HEADS * MLA_V, BF16),
            (hk, F32), (hk, F32), (hv, F32), (hv, F32), (2 * hk, F32)]
    return pl.pallas_call(
        _even_proj_kernel,
        grid=(b, t // tm),
        in_specs=[tok(d), pl.BlockSpec((None, None, N_MOD, d), lambda i, j: (i, j // nlb, 0, 0)),
                  _full(n1.shape), _full(win.shape), _full(qn.shape), _full(wq.shape), _full(kvn.shape),
                  _full(wk.shape), _full(wv.shape), _full(wg.shape), _full(gb.shape), tab, tab],
        out_specs=[tok(n) for n, _ in outs],
        out_shape=[jax.ShapeDtypeStruct((b, t, n), dt) for n, dt in outs],
        compiler_params=_cparams(("parallel", "parallel")),
        name="even_proj",
    )(h, mod, n1, win, qn, wq, kvn, wk, wv, wg, gb, ct, st)


def _attn_kernel(q_ref, k_ref, v_ref, *rest):
    o_ref = rest[-1]
    q, k, v = q_ref[...], k_ref[...], v_ref[...]
    outs = []
    for hh in range(2):
        qh = q[:, hh * HEAD_SLOT:(hh + 1) * HEAD_SLOT]
        kh = k[:, hh * HEAD_SLOT:(hh + 1) * HEAD_SLOT]
        vh = v[:, hh * MLA_V:(hh + 1) * MLA_V]
        s = lax.dot_general(qh, kh, (((1,), (1,)), ((), ())), preferred_element_type=F32)
        p = jnp.exp(s - jnp.max(s, axis=-1, keepdims=True))
        l = jnp.sum(p, axis=-1, keepdims=True)
        outs.append(jnp.dot(p.astype(BF16), vh, preferred_element_type=F32) / l)
    o_ref[...] = jnp.concatenate(outs, axis=-1).astype(o_ref.dtype)


def _attention(q, k, v, n_lat, n_ctx, need_ctx):
    b, t, _ = q.shape
    hp = MLA_HEADS // 2
    tq = 512 if n_lat % 512 == 0 else 128
    out_shape = jax.ShapeDtypeStruct((b, t, MLA_HEADS * MLA_V), BF16)
    att = pl.pallas_call(
        _attn_kernel,
        grid=(b, hp, n_lat // tq),
        in_specs=[pl.BlockSpec((None, tq, 2 * HEAD_SLOT), lambda i, h, j: (i, j, h)),
                  pl.BlockSpec((None, t, 2 * HEAD_SLOT), lambda i, h, j: (i, 0, h)),
                  pl.BlockSpec((None, t, 2 * MLA_V), lambda i, h, j: (i, 0, h))],
        out_specs=pl.BlockSpec((None, tq, 2 * MLA_V), lambda i, h, j: (i, j, h)),
        out_shape=out_shape,
        compiler_params=_cparams(("parallel", "parallel", "arbitrary")),
        name="mla_attn_lat",
    )(q, k, v)
    if not need_ctx:
        return att
    cb = t // n_ctx - 1
    return pl.pallas_call(
        _attn_kernel,
        grid=(b, hp),
        in_specs=[pl.BlockSpec((None, n_ctx, 2 * HEAD_SLOT), lambda i, h: (i, cb, h)),
                  pl.BlockSpec((None, n_ctx, 2 * HEAD_SLOT), lambda i, h: (i, cb, h)),
                  pl.BlockSpec((None, n_ctx, 2 * MLA_V), lambda i, h: (i, cb, h)),
                  pl.BlockSpec(memory_space=pl.ANY)],
        out_specs=pl.BlockSpec((None, n_ctx, 2 * MLA_V), lambda i, h: (i, cb, h)),
        out_shape=out_shape,
        input_output_aliases={3: 0},
        compiler_params=_cparams(("parallel", "parallel")),
        name="mla_attn_ctx",
    )(q, k, v, att)


def _scan_maps(nlb, ncb):
    fwd = lambda j: jnp.where(j < ncb, nlb + j, j - ncb)
    bwd = lambda j: nlb + ncb - 1 - j
    return fwd, bwd


def _scan_blk(n_lat, n_ctx):
    for blk in (256, 128, 64):
        if n_lat % blk == 0 and n_ctx % blk == 0:
            return blk
    raise ValueError("sequence lengths must be multiples of the 64-token chunk")


def _gla_chunk(q, k, v, lg, s_ref, upper, masks):
    ri, ci = masks["ri"], masks["ci"]
    incl = (ri <= ci) if upper else (ri >= ci)
    tri = jnp.where(incl, 1.0, 0.0).astype(BF16)
    bcum = _dot_exact_lhs(tri, lg)
    btot = bcum[0:1] if upper else bcum[CHUNK - 1:CHUNK]
    q_dec = q * (GLA_DK ** -0.5) * jnp.exp(bcum)
    k_inv = k * jnp.exp(-bcum)
    k_end = k * jnp.exp(btot - bcum)
    g_end = jnp.exp(btot)
    hk = GLA_HEADS * GLA_DK
    k_blk = jnp.concatenate([jnp.where(masks["klane"] == h, k_inv, 0.0) for h in range(GLA_HEADS)], axis=0)
    a = _dot_nt(q_dec, k_blk)
    a = jnp.where(masks["a_incl_u"] if upper else masks["a_incl_l"], a, 0.0)
    v_blk = jnp.concatenate([jnp.where(masks["vlane"] == h, v, 0.0) for h in range(GLA_HEADS)], axis=0)
    s_t = s_ref[...]
    o = _dot(a, v_blk) + _dot_nt(q_dec, s_t)
    upd = _dot_tn(v, k_end)
    s_ref[...] = s_t * g_end + jnp.where(masks["sdiag"], upd, 0.0)
    return o


def _gla_kernel(qf, kf, vf, lf, qb, kb, vb, lb, of, ob, sf, sb):
    @pl.when(pl.program_id(1) == 0)
    def _():
        sf[...] = jnp.zeros_like(sf)
        sb[...] = jnp.zeros_like(sb)

    hk, hv = GLA_HEADS * GLA_DK, GLA_HEADS * GLA_DV
    i32 = jnp.int32
    io = lambda shape, d: lax.broadcasted_iota(i32, shape, d)
    aj = io((CHUNK, GLA_HEADS * CHUNK), 1) % CHUNK
    ai = io((CHUNK, GLA_HEADS * CHUNK), 0)
    masks = dict(
        ri=io((CHUNK, CHUNK), 0), ci=io((CHUNK, CHUNK), 1),
        klane=io((CHUNK, hk), 1) // GLA_DK, vlane=io((CHUNK, hv), 1) // GLA_DV,
        a_incl_l=ai >= aj, a_incl_u=ai <= aj,
        sdiag=(io((hv, hk), 0) // GLA_DV) == (io((hv, hk), 1) // GLA_DK))
    nc = qf.shape[0] // CHUNK
    for c in range(nc):
        r = slice(c * CHUNK, (c + 1) * CHUNK)
        of[r, :] = _gla_chunk(qf[r, :], kf[r, :], vf[r, :], lf[r, 0:hk], sf, False, masks)
        cb = nc - 1 - c
        r = slice(cb * CHUNK, (cb + 1) * CHUNK)
        ob[r, :] = _gla_chunk(qb[r, :], kb[r, :], vb[r, :], lb[r, hk:2 * hk], sb, True, masks)


def _gla_scan(gq, gk, gv, lg, n_lat, n_ctx):
    b, t, hk = gq.shape
    hv = gv.shape[-1]
    blk = _scan_blk(n_lat, n_ctx)
    nlb, ncb = n_lat // blk, n_ctx // blk
    fwd, bwd = _scan_maps(nlb, ncb)
    spec = lambda n, m: pl.BlockSpec((None, blk, n), lambda i, j: (i, m(j), 0))
    out = jax.ShapeDtypeStruct((b, t, hv), F32)
    return pl.pallas_call(
        _gla_kernel,
        grid=(b, nlb + ncb),
        in_specs=[spec(hk, fwd), spec(hk, fwd), spec(hv, fwd), spec(2 * hk, fwd),
                  spec(hk, bwd), spec(hk, bwd), spec(hv, bwd), spec(2 * hk, bwd)],
        out_specs=[spec(hv, fwd), spec(hv, bwd)],
        out_shape=[out, out],
        scratch_shapes=[pltpu.VMEM((hv, hk), F32), pltpu.VMEM((hv, hk), F32)],
        compiler_params=_cparams(("parallel", "arbitrary")),
        name="gla_scan",
    )(gq, gk, gv, lg, gq, gk, gv, lg)


def _head_norm_gate(of_ref, ob_ref, g_ref, on_ref, n_heads):
    x = of_ref[...] + ob_ref[...]
    g = g_ref[...]
    outs = []
    for h in range(n_heads):
        s = slice(h * LANES, (h + 1) * LANES)
        outs.append(_rms(x[:, s], on_ref[...]) * _silu(g[:, s]))
    return jnp.concatenate(outs, axis=-1)


def _even_merge_kernel(att_ref, of_ref, ob_ref, gg_ref, on_ref, wo_ref, h_ref, mod_ref, o_ref):
    na = att_ref.shape[-1]
    y = _head_norm_gate(of_ref, ob_ref, gg_ref, on_ref, GLA_HEADS)
    o = jnp.dot(att_ref[...], wo_ref[0:na, :], preferred_element_type=F32) + _dot(y, wo_ref[na:, :])
    o_ref[...] = h_ref[...] + mod_ref[...][2:3] * o


def _odd_merge_kernel(of_ref, ob_ref, g_ref, on_ref, wo_ref, h_ref, mod_ref, o_ref):
    y = _head_norm_gate(of_ref, ob_ref, g_ref, on_ref, GDN_HEADS)
    o_ref[...] = h_ref[...] + mod_ref[...][2:3] * _dot(y, wo_ref[...])


def _merge(kern, parts, on, wo, h, mod, tm, n_lat, n_tok, name):
    b, _, d = h.shape
    nlb = n_lat // tm
    tok = lambda n: pl.BlockSpec((None, tm, n), lambda i, j: (i, j, 0))
    return pl.pallas_call(
        kern,
        grid=(b, n_tok // tm),
        in_specs=[tok(p.shape[-1]) for p in parts] + [
            _full(on.shape), _full(wo.shape), tok(d),
            pl.BlockSpec((None, None, N_MOD, d), lambda i, j: (i, j // nlb, 0, 0))],
        out_specs=tok(d),
        out_shape=jax.ShapeDtypeStruct((b, n_tok, d), F32),
        compiler_params=_cparams(("parallel", "parallel")),
        name=name,
    )(*parts, on, wo, h, mod)


def _mlp_kernel(h_ref, mod_ref, n2_ref, w1_ref, w2_ref, *rest):
    o_ref = rest[-1]
    mod = mod_ref[...]
    h = h_ref[...]
    a = _rms(h, n2_ref[...]) * (1.0 + mod[4:5]) + mod[3:4]
    u = jnp.maximum(_dot(a, w1_ref[...]), 0.0)
    out = h + mod[5:6] * _dot(u * u, w2_ref[...])
    if len(rest) == 2:
        out = _rms(out, rest[0][...])
    o_ref[...] = out


def _mlp(h, mod, n2, w1, w2, tm, n_lat, final_norm=None):
    b, t, d = h.shape
    nlb = n_lat // tm
    tok = pl.BlockSpec((None, tm, d), lambda i, j: (i, j, 0))
    once = lambda a: pl.BlockSpec(a.shape, lambda *_: (0,) * a.ndim, pipeline_mode=pl.Buffered(1))
    args = [h, mod, n2, w1, w2]
    specs = [tok, pl.BlockSpec((None, None, N_MOD, d), lambda i, j: (i, j // nlb, 0, 0)),
             _full(n2.shape), once(w1), once(w2)]
    if final_norm is not None:
        args.append(final_norm)
        specs.append(_full(final_norm.shape))
    return pl.pallas_call(
        _mlp_kernel,
        grid=(b, t // tm),
        in_specs=specs,
        out_specs=tok,
        out_shape=jax.ShapeDtypeStruct((b, t, d), F32),
        compiler_params=_cparams(("parallel", "parallel")),
        name="mlp",
    )(*args)


_O_QKV = 2 * GDN_HEADS * GDN_DK + GDN_HEADS * GDN_DV
_O_QK = 2 * GDN_HEADS * GDN_DK
_O_G = GDN_HEADS * GDN_DV
_O_N = _O_QKV + _O_G + LANES
HALO = 8


def _odd_proj_kernel(h_ref, hp_ref, hn_ref, mod_ref, n1_ref, win_ref, cw_ref, e_ref, et_ref, rs_ref,
                     al_ref, dt_ref, q_out, k_out, v_out, g_out, gt_out, *, nlb, nb):
    j = pl.program_id(1)
    mod = mod_ref[...]
    n1 = n1_ref[...]
    pre = lambda x: _rms(x, n1) * (1.0 + mod[1:2]) + mod[0:1]
    y = _dot(pre(h_ref[...]), win_ref[...])
    wqkv = win_ref[:, 0:_O_QKV]
    first = jnp.logical_or(j == 0, j == nlb)
    last = jnp.logical_or(j == nlb - 1, j == nb - 1)
    prev = jnp.where(first, 0.0, _dot(pre(hp_ref[...]), wqkv)[HALO - 1:HALO])
    nxt = jnp.where(last, 0.0, _dot(pre(hn_ref[...]), wqkv)[0:1])
    x = y[:, 0:_O_QKV]
    tm = x.shape[0]
    rows = lax.broadcasted_iota(jnp.int32, x.shape, 0)
    x_prev = jnp.where(rows == 0, prev, pltpu.roll(x, 1, axis=0))
    x_next = jnp.where(rows == tm - 1, nxt, pltpu.roll(x, tm - 1, axis=0))
    cw = cw_ref[...]
    s = _silu(cw[0:1] * x_prev + cw[1:2] * x + cw[2:3] * x_next)
    sqk = s[:, 0:_O_QK]
    sq1, sq2 = _split(sqk * sqk, 2)
    e = e_ref[...]
    ss = jnp.dot(sq1, e, preferred_element_type=F32) + jnp.dot(sq2, e, preferred_element_type=F32)
    r = lax.rsqrt(ss + EPS) * rs_ref[...]
    et = et_ref[...]
    rf = sum(jnp.dot(p, et, preferred_element_type=F32) for p in _split(r, 3))
    qkn = sqk * rf
    hq = GDN_HEADS * GDN_DK
    q_out[...] = qkn[:, 0:hq]
    k_out[...] = qkn[:, hq:2 * hq]
    v_out[...] = s[:, _O_QK:_O_QKV]
    g_out[...] = y[:, _O_QKV:_O_QKV + _O_G]
    t = y[:, _O_QKV + _O_G:_O_N]
    log_a = -jnp.exp(al_ref[...]) * _softplus(t + dt_ref[...])
    lane = lax.broadcasted_iota(jnp.int32, t.shape, 1)
    gt_out[...] = jnp.where(lane < 2 * GDN_HEADS, log_a, _sigmoid(t))


def _odd_project(h, mod, n1, w_in, conv_w, a_log, dt_bias, tm, n_lat):
    b, t, d = h.shape
    nlb, nb = n_lat // tm, t // tm
    win = jnp.concatenate([w_in, jnp.zeros((d, _O_N - w_in.shape[1]), F32)], axis=1).astype(BF16)
    nqk = 2 * GDN_HEADS
    e = (jnp.arange(_O_QK)[:, None] // GDN_DK == jnp.arange(LANES)[None, :]).astype(BF16)
    et = e.T
    rs = jnp.concatenate([jnp.full((GDN_HEADS,), GDN_DK ** -0.5, F32), jnp.ones((LANES - GDN_HEADS,), F32)])[None, :]
    padl = lambda v: jnp.concatenate([v.reshape(-1), jnp.zeros((LANES - v.size,), F32)])[None, :]
    al, dt = padl(a_log), padl(dt_bias)
    tok = lambda n: pl.BlockSpec((None, tm, n), lambda i, j: (i, j, 0))
    hb = tm // HALO
    outs = [GDN_HEADS * GDN_DK, GDN_HEADS * GDN_DK, GDN_HEADS * GDN_DV, _O_G, LANES]
    return pl.pallas_call(
        functools.partial(_odd_proj_kernel, nlb=nlb, nb=nb),
        grid=(b, nb),
        in_specs=[tok(d),
                  pl.BlockSpec((None, HALO, d), lambda i, j: (i, jnp.maximum(j * hb - 1, 0), 0)),
                  pl.BlockSpec((None, HALO, d), lambda i, j: (i, jnp.minimum((j + 1) * hb, t // HALO - 1), 0)),
                  pl.BlockSpec((None, None, N_MOD, d), lambda i, j: (i, j // nlb, 0, 0)),
                  _full(n1.shape), _full(win.shape), _full(conv_w.shape), _full(e.shape), _full(et.shape),
                  _full(rs.shape), _full(al.shape), _full(dt.shape)],
        out_specs=[tok(n) for n in outs],
        out_shape=[jax.ShapeDtypeStruct((b, t, n), F32) for n in outs],
        compiler_params=_cparams(("parallel", "parallel")),
        name="odd_proj",
    )(h, h, h, mod, n1, win, conv_w, e, et, rs, al, dt)


def _tri_inverse(n, masks):
    eye = masks["eye"]
    n1 = jnp.where(masks["same16"], n, 0.0)
    n2 = jnp.where(masks["lvl32"], n, 0.0)
    n3 = jnp.where(masks["same32"], 0.0, n)
    a2 = _dot3(n1, n1)
    a4 = _dot3(a2, a2)
    a8 = _dot3(a4, a4)
    t = _dot3(eye - n1, eye + a2)
    t = _dot3(t, eye + a4)
    t = _dot3(t, eye + a8)
    t = t - _dot3(_dot3(t, n2), t)
    t = t - _dot3(_dot3(t, n3), t)
    return t


def _gdn_chunk(q, k, v, gc, gr, s_ref, z, upper, masks):
    ri, ci = masks["ri"], masks["ci"]
    incl = (ri <= ci) if upper else (ri >= ci)
    strict = (ri < ci) if upper else (ri > ci)
    tri = jnp.where(incl, 1.0, 0.0).astype(BF16)
    cum_col = _dot_exact_lhs(tri, gc)
    cum_row = _dot_exact_lhs(tri, gr, nt=True)
    outs = []
    for hh in range(2):
        idx = 2 * z + hh
        gcol = cum_col[:, idx:idx + 1]
        grow = cum_row[idx:idx + 1, :]
        beta = gc[:, 4 + idx:5 + idx]
        decay = jnp.where(incl, jnp.exp(jnp.where(incl, gcol - grow, 0.0)), 0.0)
        kh = k[:, hh * GDN_DK:(hh + 1) * GDN_DK]
        qh = q[:, hh * GDN_DK:(hh + 1) * GDN_DK]
        vh = v[:, hh * GDN_DV:(hh + 1) * GDN_DV]
        kk = _dot_nt(kh, kh)
        n = jnp.where(strict, kk * beta * decay, 0.0)
        t = _tri_inverse(n, masks)
        eg = jnp.exp(gcol)
        u = _dot3(t, vh * beta)
        w = _dot3(t, kh * (beta * eg))
        qk = _dot_nt(qh, kh) * decay
        gl = gcol[0:1] if upper else gcol[CHUNK - 1:CHUNK]
        k_end = kh * jnp.exp(gl - gcol)
        s = s_ref[z, hh]
        v_new = u - _dot(w, s)
        outs.append(_dot(qh * eg, s) + _dot(qk, v_new))
        s_ref[z, hh] = s * jnp.exp(gl) + _dot_tn(k_end, v_new)
    return jnp.concatenate(outs, axis=-1)


def _gdn_kernel(qf, kf, vf, gcf, grf, qb, kb, vb, gcb, grb, of, ob, s_ref):
    @pl.when(pl.program_id(2) == 0)
    def _():
        s_ref[...] = jnp.zeros_like(s_ref)

    ri = lax.broadcasted_iota(jnp.int32, (CHUNK, CHUNK), 0)
    ci = lax.broadcasted_iota(jnp.int32, (CHUNK, CHUNK), 1)
    same16 = (ri // 16) == (ci // 16)
    same32 = (ri // 32) == (ci // 32)
    masks = dict(ri=ri, ci=ci, same16=same16, same32=same32,
                 lvl32=jnp.logical_and(same32, jnp.logical_not(same16)),
                 eye=jnp.where(ri == ci, 1.0, 0.0).astype(F32))
    nc = qf.shape[0] // CHUNK
    for c in range(nc):
        r = slice(c * CHUNK, (c + 1) * CHUNK)
        of[r, :] = _gdn_chunk(qf[r, :], kf[r, :], vf[r, :], gcf[r, :], grf[c], s_ref, 0, False, masks)
        cb = nc - 1 - c
        r = slice(cb * CHUNK, (cb + 1) * CHUNK)
        ob[r, :] = _gdn_chunk(qb[r, :], kb[r, :], vb[r, :], gcb[r, :], grb[cb], s_ref, 1, True, masks)


def _gdn_scan(q, k, v, gates, n_lat, n_ctx):
    b, t, _ = q.shape
    hp = GDN_HEADS // 2
    g = gates[..., :4 * GDN_HEADS].reshape(b, t, 2, 2, hp, 2)
    gcol = g.transpose(0, 4, 1, 2, 3, 5).reshape(b, hp, t, 8)
    grow = gcol.reshape(b, hp, t // CHUNK, CHUNK, 8).transpose(0, 1, 2, 4, 3)
    blk = _scan_blk(n_lat, n_ctx)
    nc = blk // CHUNK
    nlb, ncb = n_lat // blk, n_ctx // blk
    fwd, bwd = _scan_maps(nlb, ncb)
    d = None
    tok = lambda n, m: pl.BlockSpec((d, blk, n), lambda i, h, j: (i, m(j), h))
    gcs = lambda m: pl.BlockSpec((d, d, blk, 8), lambda i, h, j: (i, h, m(j), 0))
    grs = lambda m: pl.BlockSpec((d, d, nc, 8, CHUNK), lambda i, h, j: (i, h, m(j), 0, 0))
    out = jax.ShapeDtypeStruct((b, t, GDN_HEADS * GDN_DV), F32)
    dk2, dv2 = 2 * GDN_DK, 2 * GDN_DV
    return pl.pallas_call(
        _gdn_kernel,
        grid=(b, hp, nlb + ncb),
        in_specs=[tok(dk2, fwd), tok(dk2, fwd), tok(dv2, fwd), gcs(fwd), grs(fwd),
                  tok(dk2, bwd), tok(dk2, bwd), tok(dv2, bwd), gcs(bwd), grs(bwd)],
        out_specs=[tok(dv2, fwd), tok(dv2, bwd)],
        out_shape=[out, out],
        scratch_shapes=[pltpu.VMEM((2, 2, GDN_DK, GDN_DV), F32)],
        compiler_params=_cparams(("parallel", "parallel", "arbitrary")),
        name="gdn_scan",
    )(q, k, v, gcol, grow, q, k, v, gcol, grow)


def kernel(x, c, ctx, c_ctx, ada_w, ada_b, norm1_w, norm2_w, mlp_w1, mlp_w2, even_w_in, mla_q_norm,
           mla_w_uq, mla_kv_norm, mla_w_ukv, gla_gate_up, gla_gate_bias, gla_o_norm, even_w_out,
           gdn_w_in, gdn_conv_w, gdn_a_log, gdn_dt_bias, gdn_o_norm, gdn_w_out, final_norm):
    b, n_lat, d = x.shape
    n_ctx = ctx.shape[1]
    depth = ada_w.shape[0]
    tm = 256 if (n_lat % 256 == 0 and n_ctx % 256 == 0) else 128
    assert n_lat % tm == 0 and n_ctx % tm == 0 and n_lat % n_ctx == 0 and n_lat % GRID_W == 0

    rows = -(-(b + 1) // 8) * 8
    cc = jnp.concatenate([c, c_ctx[None, :], jnp.zeros((rows - b - 1, d), F32)], axis=0)
    mods = _ada_all(cc, ada_w, ada_b).reshape(depth, rows, N_MOD, d)
    mod_tok = jnp.stack([mods[:, :b], jnp.broadcast_to(mods[:, b:b + 1], (depth, b, N_MOD, d))], axis=2)

    h = jnp.concatenate([x, ctx], axis=1)
    tabs = _rope_tables(n_lat, n_ctx)
    row = lambda v: v[None, :]
    for layer in range(depth):
        need_ctx = layer < depth - 1
        n_tok = n_lat + n_ctx if need_ctx else n_lat
        mod = mod_tok[layer]
        n1 = row(norm1_w[layer])
        i = layer // 2
        if layer % 2 == 0:
            ew = _even_weights(even_w_in[i], mla_w_uq[i], mla_w_ukv[i], gla_gate_up[i], gla_gate_bias[i])
            q, k, v, gq, gk, gv, gg, lg = _even_project(
                h, mod, n1, ew, row(mla_q_norm[i]), row(mla_kv_norm[i]), tabs, tm, n_lat)
            att = _attention(q, k, v, n_lat, n_ctx, need_ctx)
            o_f, o_b = _gla_scan(gq, gk, gv, lg, n_lat, n_ctx)
            h = _merge(_even_merge_kernel, [att, o_f, o_b, gg], row(gla_o_norm[i]),
                       even_w_out[i].astype(BF16), h, mod, tm, n_lat, n_tok, "even_merge")
        else:
            q, k, v, g, gates = _odd_project(h, mod, n1, gdn_w_in[i], gdn_conv_w[i], gdn_a_log[i],
                                             gdn_dt_bias[i], tm, n_lat)
            o_f, o_b = _gdn_scan(q, k, v, gates, n_lat, n_ctx)
            h = _merge(_odd_merge_kernel, [o_f, o_b, g], row(gdn_o_norm[i]),
                       gdn_w_out[i].astype(BF16), h, mod, tm, n_lat, n_tok, "odd_merge")
        h = _mlp(h, mod, row(norm2_w[layer]), mlp_w1[layer].astype(BF16), mlp_w2[layer].astype(BF16),
                 tm, n_lat, final_norm=None if need_ctx else row(final_norm))
    return h
```

```python
import functools

import jax
import jax.numpy as jnp
from jax import lax
from jax.experimental import pallas as pl
from jax.experimental.pallas import tpu as pltpu

F32 = jnp.float32
BF16 = jnp.bfloat16
EPS = 1e-6
LOG2E = 1.4426950408889634

GRID_W = 64
N_MOD = 6
MLA_HEADS = 8
MLA_NOPE = 64
MLA_ROPE = 32
MLA_V = 64
MLA_QK = MLA_NOPE + MLA_ROPE
MLA_Q_RANK = 384
MLA_KV_RANK = 256
ROPE_THETA = 10000.0
GLA_HEADS = 4
GLA_DK = 64
GLA_DV = 128
GLA_GATE_RANK = 16
GLA_GATE_NORM = 16.0
GDN_HEADS = 8
GDN_DK = 64
GDN_DV = 128
CHUNK = 64
GDN_SCAN_HEADS = 4
LANES = 128
HEAD_SLOT = 128
VMEM_LIMIT = 56 * 1024 * 1024


def _cparams(sem):
    return pltpu.CompilerParams(dimension_semantics=sem, vmem_limit_bytes=VMEM_LIMIT)


def _dot(a, b):
    return jnp.dot(a.astype(BF16), b.astype(BF16), preferred_element_type=F32)


def _dot_nt(a, b):
    return lax.dot_general(a.astype(BF16), b.astype(BF16), (((1,), (1,)), ((), ())),
                           preferred_element_type=F32)


def _dot_tn(a, b):
    return lax.dot_general(a.astype(BF16), b.astype(BF16), (((0,), (0,)), ((), ())),
                           preferred_element_type=F32)


def _split(x, n):
    out = []
    for _ in range(n - 1):
        p = x.astype(BF16)
        out.append(p)
        x = x - p.astype(F32)
    out.append(x.astype(BF16))
    return out


def _dot3(a, b):
    a1, a2 = _split(a, 2)
    b1, b2 = _split(b, 2)
    d = functools.partial(jnp.dot, preferred_element_type=F32)
    return d(a1, b1) + (d(a1, b2) + d(a2, b1))


def _dot_exact_lhs(m_bf, x, nt=False):
    acc = None
    for p in _split(x, 3):
        if nt:
            t = lax.dot_general(p, m_bf, (((1,), (1,)), ((), ())), preferred_element_type=F32)
        else:
            t = jnp.dot(m_bf, p, preferred_element_type=F32)
        acc = t if acc is None else acc + t
    return acc


def _rms(x, w):
    return x * lax.rsqrt(jnp.mean(x * x, axis=-1, keepdims=True) + EPS) * w


def _sigmoid(x):
    return 1.0 / (1.0 + jnp.exp(-x))


def _silu(x):
    return x * _sigmoid(x)


def _softplus(x):
    return jnp.maximum(x, 0.0) + jnp.log1p(jnp.exp(-jnp.abs(x)))


def _tile_lanes(x, n):
    return jnp.concatenate([x] * n, axis=-1)


def _ada_kernel(cc_ref, w_ref, b_ref, o_ref):
    o_ref[0] = _dot(_silu(cc_ref[...]), w_ref[0]) + b_ref[0]


def _ada_all(cc, ada_w, ada_b):
    depth, d, n = ada_w.shape
    rows = cc.shape[0]
    tn = 1024
    return pl.pallas_call(
        _ada_kernel,
        grid=(depth, n // tn),
        in_specs=[pl.BlockSpec((rows, d), lambda l, j: (0, 0)),
                  pl.BlockSpec((1, d, tn), lambda l, j: (l, 0, j)),
                  pl.BlockSpec((1, 1, tn), lambda l, j: (l, 0, j))],
        out_specs=pl.BlockSpec((1, rows, tn), lambda l, j: (l, 0, j)),
        out_shape=jax.ShapeDtypeStruct((depth, rows, n), F32),
        compiler_params=_cparams(("parallel", "parallel")),
        name="ada_mod",
    )(cc, ada_w, ada_b.reshape(depth, 1, n))


_E_CQ = (0, 384)
_E_CKV = (384, 640)
_E_KP = (640, 768)
_E_KPR = (768, 896)
_E_GQ = (896, 1152)
_E_GK = (1152, 1408)
_E_GV = (1408, 1920)
_E_GG = (1920, 2432)
_E_GLOW = (2432, 2560)
_E_N = 2560


def _even_proj_kernel(h_ref, mod_ref, n1_ref, win_ref, qn_ref, wq_ref, kvn_ref, wk_ref, wv_ref,
                      wg_ref, gb_ref, c_ref, s_ref, ctt_ref, stt_ref,
                      q_out, k_out, v_out, gq_out, gk_out, gv_out, gg_out, lg_out):
    mod = mod_ref[...]
    a = _rms(h_ref[...], n1_ref[...]) * (1.0 + mod[1:2]) + mod[0:1]
    y = _dot(a, win_ref[...])
    sl = lambda r: y[:, r[0]:r[1]]
    ct, st = c_ref[...], s_ref[...]
    nh = MLA_HEADS
    qq = _dot_nt(wq_ref[...], _rms(sl(_E_CQ), qn_ref[...]))
    wq = nh * HEAD_SLOT
    tile_rows = lambda x: jnp.concatenate([x] * nh, axis=0)
    q_out[...] = (qq[:wq] * tile_rows(ctt_ref[...]) + qq[wq:] * tile_rows(stt_ref[...])).astype(q_out.dtype)
    ckvn = _rms(sl(_E_CKV), kvn_ref[...])
    kpe = sl(_E_KP) * ct + sl(_E_KPR) * st
    k_out[...] = (_dot(ckvn, wk_ref[...]) + _tile_lanes(kpe, nh)).astype(k_out.dtype)
    v_out[...] = _dot_nt(wv_ref[...], ckvn).astype(v_out.dtype)
    gq_out[...] = sl(_E_GQ)
    gk_out[...] = sl(_E_GK)
    gv_out[...] = sl(_E_GV)
    gg_out[...] = sl(_E_GG)
    logit = _dot(sl(_E_GLOW), wg_ref[...]) + gb_ref[...]
    lg_out[...] = (jnp.minimum(logit, 0.0) - jnp.log1p(jnp.exp(-jnp.abs(logit)))) * (1.0 / GLA_GATE_NORM)


def _rot_half_cols(w):
    f = MLA_ROPE // 4
    return jnp.concatenate([-w[..., f:2 * f], w[..., 0:f], -w[..., 3 * f:4 * f], w[..., 2 * f:3 * f]], axis=-1)


def _even_weights(w_in, w_uq, w_ukv, gate_up, gate_bias):
    d = w_in.shape[0]
    cq, ckv, kpe, gq, gk, gv, gg, glow = jnp.split(
        w_in, [384, 640, 672, 928, 1184, 1696, 2208], axis=1)
    z = lambda n: jnp.zeros((d, n), F32)
    slot = lambda w: jnp.concatenate([z(MLA_NOPE), w, z(HEAD_SLOT - MLA_QK)], axis=1)
    win = jnp.concatenate([cq, ckv, slot(kpe), slot(_rot_half_cols(kpe)), gq, gk, gv, gg, glow,
                           z(LANES - 2 * GLA_GATE_RANK)], axis=1).astype(BF16)
    r = w_uq.shape[0]
    wq = (w_uq * (MLA_QK ** -0.5 * LOG2E)).reshape(r, MLA_HEADS, MLA_QK)
    nope, pe = wq[..., :MLA_NOPE], wq[..., MLA_NOPE:]
    zq = lambda n: jnp.zeros((r, MLA_HEADS, n), F32)
    pad = HEAD_SLOT - MLA_QK
    wq_a = jnp.concatenate([nope, pe, zq(pad)], axis=-1).reshape(r, -1)
    wq_b = jnp.concatenate([zq(MLA_NOPE), _rot_half_cols(pe), zq(pad)], axis=-1).reshape(r, -1)
    wq_all = jnp.concatenate([wq_a, wq_b], axis=1).T.astype(BF16)
    rk = w_ukv.shape[0]
    wkv = w_ukv.reshape(rk, MLA_HEADS, MLA_NOPE + MLA_V)
    wk = jnp.concatenate([wkv[..., :MLA_NOPE], jnp.zeros((rk, MLA_HEADS, HEAD_SLOT - MLA_NOPE), F32)],
                         axis=-1).reshape(rk, -1).astype(BF16)
    wv = wkv[..., MLA_NOPE:].reshape(rk, -1).T.astype(BF16)
    hk = GLA_HEADS * GLA_DK
    wg = jnp.zeros((LANES, 2 * hk), F32)
    wg = wg.at[0:GLA_GATE_RANK, 0:hk].set(gate_up[0])
    wg = wg.at[GLA_GATE_RANK:2 * GLA_GATE_RANK, hk:].set(gate_up[1])
    gb = jnp.concatenate([gate_bias[0], gate_bias[1]])[None, :]
    return win, wq_all, wk, wv, wg.astype(BF16), gb


def _rope_tables(n_lat, n_ctx):
    rows = n_lat // GRID_W
    row = jnp.repeat(jnp.arange(rows, dtype=F32), GRID_W)
    col = jnp.tile(jnp.arange(GRID_W, dtype=F32), rows)
    axis_dim = MLA_ROPE // 2
    inv_freq = ROPE_THETA ** (-jnp.arange(0, axis_dim, 2, dtype=F32) / axis_dim)
    ar, ac = row[:, None] * inv_freq, col[:, None] * inv_freq
    c32 = jnp.concatenate([jnp.cos(ar), jnp.cos(ar), jnp.cos(ac), jnp.cos(ac)], axis=1)
    s32 = jnp.concatenate([jnp.sin(ar), jnp.sin(ar), jnp.sin(ac), jnp.sin(ac)], axis=1)
    pad = HEAD_SLOT - MLA_QK
    ct = jnp.concatenate([jnp.ones((n_lat, MLA_NOPE), F32), c32, jnp.zeros((n_lat, pad), F32)], axis=1)
    st = jnp.concatenate([jnp.zeros((n_lat, MLA_NOPE), F32), s32, jnp.zeros((n_lat, pad), F32)], axis=1)
    cc = jnp.concatenate([jnp.ones((n_ctx, MLA_QK), F32), jnp.zeros((n_ctx, pad), F32)], axis=1)
    ct = jnp.concatenate([ct, cc], axis=0)
    st = jnp.concatenate([st, jnp.zeros((n_ctx, HEAD_SLOT), F32)], axis=0)
    return ct, st, ct.T, st.T


def _full(shape):
    nd = len(shape)
    return pl.BlockSpec(shape, lambda *_: (0,) * nd)


def _even_project(h, mod, n1, ew, qn, kvn, tabs, tm, n_lat):
    b, t, d = h.shape
    win, wq, wk, wv, wg, gb = ew
    ct, st, ctt, stt = tabs
    nlb = n_lat // tm
    tok = lambda n: pl.BlockSpec((None, tm, n), lambda i, j: (i, j, 0))
    tab = pl.BlockSpec((tm, HEAD_SLOT), lambda i, j: (j, 0))
    tabt = pl.BlockSpec((HEAD_SLOT, tm), lambda i, j: (0, j))
    hk, hv = GLA_HEADS * GLA_DK, GLA_HEADS * GLA_DV
    feat = lambda n: pl.BlockSpec((None, n, tm), lambda i, j: (i, 0, j))
    tshape = lambda n, dt: jax.ShapeDtypeStruct((b, t, n), dt)
    fshape = lambda n, dt: jax.ShapeDtypeStruct((b, n, t), dt)
    nq, nv = MLA_HEADS * HEAD_SLOT, MLA_HEADS * MLA_V
    return pl.pallas_call(
        _even_proj_kernel,
        grid=(b, t // tm),
        in_specs=[tok(d), pl.BlockSpec((None, None, N_MOD, d), lambda i, j: (i, j // nlb, 0, 0)),
                  _full(n1.shape), _full(win.shape), _full(qn.shape), _full(wq.shape), _full(kvn.shape),
                  _full(wk.shape), _full(wv.shape), _full(wg.shape), _full(gb.shape), tab, tab, tabt, tabt],
        out_specs=[feat(nq), tok(nq), feat(nv), tok(hk), tok(hk), tok(hv), tok(hv), tok(2 * hk)],
        out_shape=[fshape(nq, BF16), tshape(nq, BF16), fshape(nv, BF16), tshape(hk, F32), tshape(hk, F32),
                   tshape(hv, F32), tshape(hv, F32), tshape(2 * hk, F32)],
        compiler_params=_cparams(("parallel", "parallel")),
        name="even_proj",
    )(h, mod, n1, win, qn, wq, kvn, wk, wv, wg, gb, ct, st, ctt, stt)


def _attn_kernel(qt_ref, k_ref, vt_ref, *rest):
    o_ref = rest[-1]
    tk_all = k_ref.shape[0]
    tk = 256 if tk_all % 256 == 0 else 128
    heads = range(2)
    nkb = tk_all // tk
    qts = [qt_ref[hh * HEAD_SLOT:(hh + 1) * HEAD_SLOT, :] for hh in heads]

    def scores(kb):
        return [jnp.dot(k_ref[kb * tk:(kb + 1) * tk, hh * HEAD_SLOT:(hh + 1) * HEAD_SLOT], qts[hh],
                        preferred_element_type=F32) for hh in heads]

    s_next = scores(0)
    m = l = acc = None
    for kb in range(nkb):
        s = s_next
        if kb + 1 < nkb:
            s_next = scores(kb + 1)
        bm = [jnp.max(x, axis=0, keepdims=True) for x in s]
        if m is None:
            m_new = bm
        else:
            m_new = [jnp.maximum(a, b_) for a, b_ in zip(m, bm)]
        p = [jnp.exp2(x - mn) for x, mn in zip(s, m_new)]
        ps = [jnp.sum(x, axis=0, keepdims=True) for x in p]
        pv = [jnp.dot(vt_ref[hh * MLA_V:(hh + 1) * MLA_V, kb * tk:(kb + 1) * tk], p[hh].astype(BF16),
                      preferred_element_type=F32) for hh in heads]
        if m is None:
            l, acc = ps, pv
        else:
            alpha = [jnp.exp2(a - b_) for a, b_ in zip(m, m_new)]
            l = [a * x + y for a, x, y in zip(alpha, l, ps)]
            acc = [a * x + y for a, x, y in zip(alpha, acc, pv)]
        m = m_new
    o_ref[...] = jnp.concatenate([(a / x).T for a, x in zip(acc, l)], axis=-1).astype(o_ref.dtype)


def _attention(qt, k, vt, n_lat, n_ctx, need_ctx):
    b, t, _ = k.shape
    hp = MLA_HEADS // 2
    tq = 512 if n_lat % 512 == 0 else 128
    out_shape = jax.ShapeDtypeStruct((b, t, MLA_HEADS * MLA_V), BF16)
    att = pl.pallas_call(
        _attn_kernel,
        grid=(b, hp, n_lat // tq),
        in_specs=[pl.BlockSpec((None, 2 * HEAD_SLOT, tq), lambda i, h, j: (i, h, j)),
                  pl.BlockSpec((None, t, 2 * HEAD_SLOT), lambda i, h, j: (i, 0, h)),
                  pl.BlockSpec((None, 2 * MLA_V, t), lambda i, h, j: (i, h, 0))],
        out_specs=pl.BlockSpec((None, tq, 2 * MLA_V), lambda i, h, j: (i, j, h)),
        out_shape=out_shape,
        compiler_params=_cparams(("parallel", "parallel", "arbitrary")),
        name="mla_attn_lat",
    )(qt, k, vt)
    if not need_ctx:
        return att
    cb = t // n_ctx - 1
    return pl.pallas_call(
        _attn_kernel,
        grid=(b, hp),
        in_specs=[pl.BlockSpec((None, 2 * HEAD_SLOT, n_ctx), lambda i, h: (i, h, cb)),
                  pl.BlockSpec((None, n_ctx, 2 * HEAD_SLOT), lambda i, h: (i, cb, h)),
                  pl.BlockSpec((None, 2 * MLA_V, n_ctx), lambda i, h: (i, h, cb)),
                  pl.BlockSpec(memory_space=pl.ANY)],
        out_specs=pl.BlockSpec((None, n_ctx, 2 * MLA_V), lambda i, h: (i, cb, h)),
        out_shape=out_shape,
        input_output_aliases={3: 0},
        compiler_params=_cparams(("parallel", "parallel")),
        name="mla_attn_ctx",
    )(qt, k, vt, att)


def _scan_maps(nlb, ncb):
    fwd = lambda j: jnp.where(j < ncb, nlb + j, j - ncb)
    bwd = lambda j: nlb + ncb - 1 - j
    return fwd, bwd


def _scan_blk(n_lat, n_ctx):
    for blk in (256, 128, 64):
        if n_lat % blk == 0 and n_ctx % blk == 0:
            return blk
    raise ValueError("sequence lengths must be multiples of the 64-token chunk")


def _gla_chunk(q, k, v, lg, s_ref, upper, masks):
    ri, ci = masks["ri"], masks["ci"]
    incl = (ri <= ci) if upper else (ri >= ci)
    tri = jnp.where(incl, 1.0, 0.0).astype(BF16)
    bcum = _dot_exact_lhs(tri, lg)
    btot = bcum[0:1] if upper else bcum[CHUNK - 1:CHUNK]
    q_dec = q * (GLA_DK ** -0.5) * jnp.exp(bcum)
    k_inv = k * jnp.exp(-bcum)
    k_end = k * jnp.exp(btot - bcum)
    g_end = jnp.exp(btot)
    hk = GLA_HEADS * GLA_DK
    k_blk = jnp.concatenate([jnp.where(masks["klane"] == h, k_inv, 0.0) for h in range(GLA_HEADS)], axis=0)
    a = _dot_nt(q_dec, k_blk)
    a = jnp.where(masks["a_incl_u"] if upper else masks["a_incl_l"], a, 0.0)
    v_blk = jnp.concatenate([jnp.where(masks["vlane"] == h, v, 0.0) for h in range(GLA_HEADS)], axis=0)
    s_t = s_ref[...]
    o = _dot(a, v_blk) + _dot_nt(q_dec, s_t)
    upd = _dot_tn(v, k_end)
    s_ref[...] = s_t * g_end + jnp.where(masks["sdiag"], upd, 0.0)
    return o


def _gla_kernel(qf, kf, vf, lf, qb, kb, vb, lb, of, ob, sf, sb):
    @pl.when(pl.program_id(1) == 0)
    def _():
        sf[...] = jnp.zeros_like(sf)
        sb[...] = jnp.zeros_like(sb)

    hk, hv = GLA_HEADS * GLA_DK, GLA_HEADS * GLA_DV
    i32 = jnp.int32
    io = lambda shape, d: lax.broadcasted_iota(i32, shape, d)
    aj = io((CHUNK, GLA_HEADS * CHUNK), 1) % CHUNK
    ai = io((CHUNK, GLA_HEADS * CHUNK), 0)
    masks = dict(
        ri=io((CHUNK, CHUNK), 0), ci=io((CHUNK, CHUNK), 1),
        klane=io((CHUNK, hk), 1) // GLA_DK, vlane=io((CHUNK, hv), 1) // GLA_DV,
        a_incl_l=ai >= aj, a_incl_u=ai <= aj,
        sdiag=(io((hv, hk), 0) // GLA_DV) == (io((hv, hk), 1) // GLA_DK))
    nc = qf.shape[0] // CHUNK
    for c in range(nc):
        r = slice(c * CHUNK, (c + 1) * CHUNK)
        of[r, :] = _gla_chunk(qf[r, :], kf[r, :], vf[r, :], lf[r, 0:hk], sf, False, masks)
        cb = nc - 1 - c
        r = slice(cb * CHUNK, (cb + 1) * CHUNK)
        ob[r, :] = _gla_chunk(qb[r, :], kb[r, :], vb[r, :], lb[r, hk:2 * hk], sb, True, masks)


def _gla_scan(gq, gk, gv, lg, n_lat, n_ctx):
    b, t, hk = gq.shape
    hv = gv.shape[-1]
    blk = _scan_blk(n_lat, n_ctx)
    nlb, ncb = n_lat // blk, n_ctx // blk
    fwd, bwd = _scan_maps(nlb, ncb)
    spec = lambda n, m: pl.BlockSpec((None, blk, n), lambda i, j: (i, m(j), 0))
    out = jax.ShapeDtypeStruct((b, t, hv), F32)
    return pl.pallas_call(
        _gla_kernel,
        grid=(b, nlb + ncb),
        in_specs=[spec(hk, fwd), spec(hk, fwd), spec(hv, fwd), spec(2 * hk, fwd),
                  spec(hk, bwd), spec(hk, bwd), spec(hv, bwd), spec(2 * hk, bwd)],
        out_specs=[spec(hv, fwd), spec(hv, bwd)],
        out_shape=[out, out],
        scratch_shapes=[pltpu.VMEM((hv, hk), F32), pltpu.VMEM((hv, hk), F32)],
        compiler_params=_cparams(("parallel", "arbitrary")),
        name="gla_scan",
    )(gq, gk, gv, lg, gq, gk, gv, lg)


def _cast_kernel(x_ref, o_ref):
    o_ref[...] = x_ref[...].astype(o_ref.dtype)


def _to_bf16(w):
    depth, r, c = w.shape
    tr = 512 if r % 512 == 0 else r
    spec = pl.BlockSpec((None, tr, c), lambda l, i: (l, i, 0))
    return pl.pallas_call(
        _cast_kernel,
        grid=(depth, r // tr),
        in_specs=[spec],
        out_specs=spec,
        out_shape=jax.ShapeDtypeStruct(w.shape, BF16),
        compiler_params=_cparams(("parallel", "parallel")),
        name="cast_bf16",
    )(w)


def _head_norm_gate(of_ref, ob_ref, g_ref, on_ref, n_heads):
    x = of_ref[...] + ob_ref[...]
    g = g_ref[...]
    outs = []
    for h in range(n_heads):
        s = slice(h * LANES, (h + 1) * LANES)
        outs.append(_rms(x[:, s], on_ref[...]) * _silu(g[:, s]))
    return jnp.concatenate(outs, axis=-1)


def _merge_mlp_kernel(*refs, n_parts, final):
    parts = refs[:n_parts]
    on_ref, wo_ref, h_ref, mod_ref, n2_ref, w1_ref, w2_ref = refs[n_parts:n_parts + 7]
    o_ref = refs[-1]
    mod = mod_ref[...]
    if n_parts == 4:
        att_ref = parts[0]
        na = att_ref.shape[-1]
        y = _head_norm_gate(*parts[1:], on_ref, GLA_HEADS)
        o = jnp.dot(att_ref[...], wo_ref[0:na, :], preferred_element_type=F32) + _dot(y, wo_ref[na:, :])
    else:
        o = _dot(_head_norm_gate(*parts, on_ref, GDN_HEADS), wo_ref[...])
    h = h_ref[...] + mod[2:3] * o
    a = _rms(h, n2_ref[...]) * (1.0 + mod[4:5]) + mod[3:4]
    u = jnp.maximum(_dot(a, w1_ref[...]), 0.0)
    out = h + mod[5:6] * _dot(u * u, w2_ref[...])
    if final:
        out = _rms(out, refs[-2][...])
    o_ref[...] = out


def _merge_mlp(parts, on, wo, h, mod, n2, w1, w2, tm, n_lat, n_tok, final_norm=None):
    b, _, d = h.shape
    nlb = n_lat // tm
    tok = lambda n: pl.BlockSpec((None, tm, n), lambda i, j: (i, j, 0))
    once = lambda a: pl.BlockSpec(a.shape, lambda *_: (0,) * a.ndim, pipeline_mode=pl.Buffered(1))
    args = [*parts, on, wo, h, mod, n2, w1, w2]
    specs = [tok(p.shape[-1]) for p in parts] + [
        _full(on.shape), once(wo), tok(d),
        pl.BlockSpec((None, None, N_MOD, d), lambda i, j: (i, j // nlb, 0, 0)),
        _full(n2.shape), once(w1), once(w2)]
    if final_norm is not None:
        args.append(final_norm)
        specs.append(_full(final_norm.shape))
    return pl.pallas_call(
        functools.partial(_merge_mlp_kernel, n_parts=len(parts), final=final_norm is not None),
        grid=(b, n_tok // tm),
        in_specs=specs,
        out_specs=tok(d),
        out_shape=jax.ShapeDtypeStruct((b, n_tok, d), F32),
        compiler_params=_cparams(("parallel", "parallel")),
        name="merge_mlp",
    )(*args)


_O_QKV = 2 * GDN_HEADS * GDN_DK + GDN_HEADS * GDN_DV
_O_QK = 2 * GDN_HEADS * GDN_DK
_O_G = GDN_HEADS * GDN_DV
_O_N = _O_QKV + _O_G + LANES
HALO = 8


def _odd_proj_kernel(h_ref, hp_ref, hn_ref, mod_ref, n1_ref, win_ref, cw_ref, e_ref, et_ref, rs_ref,
                     al_ref, dt_ref, q_out, k_out, v_out, g_out, gt_out, *, nlb, nb):
    j = pl.program_id(1)
    mod = mod_ref[...]
    n1 = n1_ref[...]
    pre = lambda x: _rms(x, n1) * (1.0 + mod[1:2]) + mod[0:1]
    y = _dot(pre(h_ref[...]), win_ref[...])
    wqkv = win_ref[:, 0:_O_QKV]
    first = jnp.logical_or(j == 0, j == nlb)
    last = jnp.logical_or(j == nlb - 1, j == nb - 1)
    prev = jnp.where(first, 0.0, _dot(pre(hp_ref[...]), wqkv)[HALO - 1:HALO])
    nxt = jnp.where(last, 0.0, _dot(pre(hn_ref[...]), wqkv)[0:1])
    x = y[:, 0:_O_QKV]
    tm = x.shape[0]
    rows = lax.broadcasted_iota(jnp.int32, x.shape, 0)
    x_prev = jnp.where(rows == 0, prev, pltpu.roll(x, 1, axis=0))
    x_next = jnp.where(rows == tm - 1, nxt, pltpu.roll(x, tm - 1, axis=0))
    cw = cw_ref[...]
    s = _silu(cw[0:1] * x_prev + cw[1:2] * x + cw[2:3] * x_next)
    sqk = s[:, 0:_O_QK]
    sq1, sq2 = _split(sqk * sqk, 2)
    e = e_ref[...]
    ss = jnp.dot(sq1, e, preferred_element_type=F32) + jnp.dot(sq2, e, preferred_element_type=F32)
    r = lax.rsqrt(ss + EPS) * rs_ref[...]
    et = et_ref[...]
    rf = sum(jnp.dot(p, et, preferred_element_type=F32) for p in _split(r, 3))
    qkn = sqk * rf
    hq = GDN_HEADS * GDN_DK
    q_out[...] = qkn[:, 0:hq]
    k_out[...] = qkn[:, hq:2 * hq]
    v_out[...] = s[:, _O_QK:_O_QKV]
    g_out[...] = y[:, _O_QKV:_O_QKV + _O_G]
    t = y[:, _O_QKV + _O_G:_O_N]
    log_a = -jnp.exp(al_ref[...]) * _softplus(t + dt_ref[...])
    lane = lax.broadcasted_iota(jnp.int32, t.shape, 1)
    gt_out[...] = jnp.where(lane < 2 * GDN_HEADS, log_a, _sigmoid(t))


def _odd_project(h, mod, n1, w_in, conv_w, a_log, dt_bias, tm, n_lat):
    b, t, d = h.shape
    nlb, nb = n_lat // tm, t // tm
    win = jnp.concatenate([w_in, jnp.zeros((d, _O_N - w_in.shape[1]), F32)], axis=1).astype(BF16)
    nqk = 2 * GDN_HEADS
    e = (jnp.arange(_O_QK)[:, None] // GDN_DK == jnp.arange(LANES)[None, :]).astype(BF16)
    et = e.T
    rs = jnp.concatenate([jnp.full((GDN_HEADS,), GDN_DK ** -0.5, F32), jnp.ones((LANES - GDN_HEADS,), F32)])[None, :]
    padl = lambda v: jnp.concatenate([v.reshape(-1), jnp.zeros((LANES - v.size,), F32)])[None, :]
    al, dt = padl(a_log), padl(dt_bias)
    tok = lambda n: pl.BlockSpec((None, tm, n), lambda i, j: (i, j, 0))
    hb = tm // HALO
    outs = [GDN_HEADS * GDN_DK, GDN_HEADS * GDN_DK, GDN_HEADS * GDN_DV, _O_G, LANES]
    return pl.pallas_call(
        functools.partial(_odd_proj_kernel, nlb=nlb, nb=nb),
        grid=(b, nb),
        in_specs=[tok(d),
                  pl.BlockSpec((None, HALO, d), lambda i, j: (i, jnp.maximum(j * hb - 1, 0), 0)),
                  pl.BlockSpec((None, HALO, d), lambda i, j: (i, jnp.minimum((j + 1) * hb, t // HALO - 1), 0)),
                  pl.BlockSpec((None, None, N_MOD, d), lambda i, j: (i, j // nlb, 0, 0)),
                  _full(n1.shape), _full(win.shape), _full(conv_w.shape), _full(e.shape), _full(et.shape),
                  _full(rs.shape), _full(al.shape), _full(dt.shape)],
        out_specs=[tok(n) for n in outs],
        out_shape=[jax.ShapeDtypeStruct((b, t, n), F32) for n in outs],
        compiler_params=_cparams(("parallel", "parallel")),
        name="odd_proj",
    )(h, h, h, mod, n1, win, conv_w, e, et, rs, al, dt)


def _stage(fn, *cols):
    return [fn(*a) for a in zip(*cols)]


def _tri_inverse_all(ns, masks):
    lvl = masks["lvl"]
    ts = _stage(lambda n: masks["eye"] - jnp.where(lvl[0], n, 0.0), ns)
    for off in lvl[1:]:
        xs = _stage(lambda t, n: _dot(t, jnp.where(off, n, 0.0)), ts, ns)
        ts = _stage(lambda t, x: t - _dot(x, t), ts, xs)
    return ts


def _gdn_kernel(qf, kf, vf, gcf, grf, qb, kb, vb, gcb, grb, of, ob, s_ref):
    @pl.when(pl.program_id(2) == 0)
    def _():
        s_ref[...] = jnp.zeros_like(s_ref)

    blk = qf.shape[0]
    nc = blk // CHUNK
    nh = qf.shape[1] // GDN_DK
    i32 = jnp.int32
    ri = lax.broadcasted_iota(i32, (CHUNK, CHUNK), 0)
    ci = lax.broadcasted_iota(i32, (CHUNK, CHUNK), 1)
    same = [(ri // w) == (ci // w) for w in (2, 4, 8, 16, 32)]
    lvl = [same[0]] + [jnp.logical_and(same[i], jnp.logical_not(same[i - 1])) for i in range(1, 5)]
    lvl.append(jnp.logical_not(same[4]))
    masks = dict(lvl=lvl, eye=jnp.where(ri == ci, 1.0, 0.0).astype(F32))
    incl = (ri >= ci, ri <= ci)
    strict = (ri > ci, ri < ci)
    bi = lax.broadcasted_iota(i32, (blk, blk), 0)
    bj = lax.broadcasted_iota(i32, (blk, blk), 1)
    same_chunk = (bi // CHUNK) == (bj // CHUNK)

    zs, cs, hs, qs, ks, vs, gbs, bbs, grows, brows, gls = ([] for _ in range(11))
    sq = (CHUNK, CHUNK)
    for z, (q, k, v, gc, gr) in enumerate(((qf, kf, vf, gcf, grf), (qb, kb, vb, gcb, grb))):
        tri = jnp.where(jnp.logical_and(same_chunk, (bi <= bj) if z else (bi >= bj)), 1.0, 0.0).astype(BF16)
        gcv, grv = gc[...], gr[...]
        cum_col = _dot_exact_lhs(tri, gcv)
        cum_row = _dot_exact_lhs(tri, grv, nt=True)
        for c in range(nc):
            r = slice(c * CHUNK, (c + 1) * CHUNK)
            last = c * CHUNK if z else (c + 1) * CHUNK - 1
            for hh in range(nh):
                idx = nh * z + hh
                zs.append(z), cs.append(c), hs.append(hh)
                qs.append(q[r, hh * GDN_DK:(hh + 1) * GDN_DK])
                ks.append(k[r, hh * GDN_DK:(hh + 1) * GDN_DK])
                vs.append(v[r, hh * GDN_DV:(hh + 1) * GDN_DV])
                gbs.append(jnp.broadcast_to(cum_col[r, idx:idx + 1], sq))
                bbs.append(jnp.broadcast_to(gcv[r, 2 * nh + idx:2 * nh + idx + 1], sq))
                grows.append(cum_row[idx:idx + 1, r])
                brows.append(grv[2 * nh + idx:2 * nh + idx + 1, r])
                gls.append(cum_col[last:last + 1, idx:idx + 1])

    decays = _stage(lambda z, gb, gr_: jnp.where(incl[z], jnp.exp(jnp.where(incl[z], gb - gr_, 0.0)), 0.0),
                    zs, gbs, grows)
    qkk = _stage(lambda q, k: _dot_nt(jnp.concatenate([q, k], axis=0), k), qs, ks)
    ns = _stage(lambda z, x, bb, d_: jnp.where(strict[z], x[CHUNK:] * bb * d_, 0.0), zs, qkk, bbs, decays)
    ts = _tri_inverse_all(ns, masks)
    egs = _stage(jnp.exp, gbs)
    uws = _stage(lambda t, br, v, k, e_: _dot(t * br, jnp.concatenate([v, k * e_], axis=-1)),
                 ts, brows, vs, ks, egs)
    qks = _stage(lambda x, d_: x[:CHUNK] * d_, qkk, decays)
    kes = _stage(lambda k, gl, gb: k * jnp.exp(gl - gb), ks, gls, gbs)
    kuws = _stage(_dot_tn, kes, uws)
    quws = _stage(_dot, qks, uws)
    lhs = _stage(lambda q, e_, qu, ku: jnp.concatenate([q * e_ - qu[:, GDN_DV:], -ku[:, GDN_DV:]], axis=0),
                 qs, egs, quws, kuws)
    ges = _stage(jnp.exp, gls)

    chain = {(z, c, hh): i for i, (z, c, hh) in enumerate(zip(zs, cs, hs))}
    state = {(z, hh): s_ref[z, hh] for z in range(2) for hh in range(nh)}
    outs = (of, ob)
    for step in range(nc):
        for z in range(2):
            c = nc - 1 - step if z else step
            for hh in range(nh):
                i = chain[(z, c, hh)]
                s = state[(z, hh)]
                r = _dot(lhs[i], s)
                outs[z][c * CHUNK:(c + 1) * CHUNK, hh * GDN_DV:(hh + 1) * GDN_DV] = (
                    r[:CHUNK] + quws[i][:, :GDN_DV])
                state[(z, hh)] = s * ges[i] + (kuws[i][:, :GDN_DV] + r[CHUNK:])
    for (z, hh), s in state.items():
        s_ref[z, hh] = s


def _gdn_scan(q, k, v, gates, n_lat, n_ctx):
    b, t, _ = q.shape
    nh = GDN_SCAN_HEADS
    ng = GDN_HEADS // nh
    g = gates[..., :4 * GDN_HEADS].reshape(b, t, 2, 2, ng, nh)
    gcol = g.transpose(0, 4, 1, 2, 3, 5).reshape(b, ng, t, 4 * nh)
    grow = gcol.transpose(0, 1, 3, 2)
    blk = _scan_blk(n_lat, n_ctx)
    nlb, ncb = n_lat // blk, n_ctx // blk
    fwd, bwd = _scan_maps(nlb, ncb)
    d = None
    tok = lambda n, m: pl.BlockSpec((d, blk, n), lambda i, h, j: (i, m(j), h))
    gcs = lambda m: pl.BlockSpec((d, d, blk, 4 * nh), lambda i, h, j: (i, h, m(j), 0))
    grs = lambda m: pl.BlockSpec((d, d, 4 * nh, blk), lambda i, h, j: (i, h, 0, m(j)))
    out = jax.ShapeDtypeStruct((b, t, GDN_HEADS * GDN_DV), F32)
    dk2, dv2 = nh * GDN_DK, nh * GDN_DV
    return pl.pallas_call(
        _gdn_kernel,
        grid=(b, ng, nlb + ncb),
        in_specs=[tok(dk2, fwd), tok(dk2, fwd), tok(dv2, fwd), gcs(fwd), grs(fwd),
                  tok(dk2, bwd), tok(dk2, bwd), tok(dv2, bwd), gcs(bwd), grs(bwd)],
        out_specs=[tok(dv2, fwd), tok(dv2, bwd)],
        out_shape=[out, out],
        scratch_shapes=[pltpu.VMEM((2, nh, GDN_DK, GDN_DV), F32)],
        compiler_params=_cparams(("parallel", "parallel", "arbitrary")),
        name="gdn_scan",
    )(q, k, v, gcol, grow, q, k, v, gcol, grow)


def kernel(x, c, ctx, c_ctx, ada_w, ada_b, norm1_w, norm2_w, mlp_w1, mlp_w2, even_w_in, mla_q_norm,
           mla_w_uq, mla_kv_norm, mla_w_ukv, gla_gate_up, gla_gate_bias, gla_o_norm, even_w_out,
           gdn_w_in, gdn_conv_w, gdn_a_log, gdn_dt_bias, gdn_o_norm, gdn_w_out, final_norm):
    b, n_lat, d = x.shape
    n_ctx = ctx.shape[1]
    depth = ada_w.shape[0]
    tm = 256 if (n_lat % 256 == 0 and n_ctx % 256 == 0) else 128
    assert n_lat % tm == 0 and n_ctx % tm == 0 and n_lat % n_ctx == 0 and n_lat % GRID_W == 0

    rows = -(-(b + 1) // 8) * 8
    cc = jnp.concatenate([c, c_ctx[None, :], jnp.zeros((rows - b - 1, d), F32)], axis=0)
    mods = _ada_all(cc, ada_w, ada_b).reshape(depth, rows, N_MOD, d)
    mod_tok = jnp.stack([mods[:, :b], jnp.broadcast_to(mods[:, b:b + 1], (depth, b, N_MOD, d))], axis=2)

    h = jnp.concatenate([x, ctx], axis=1)
    tabs = _rope_tables(n_lat, n_ctx)
    w1_bf, w2_bf = _to_bf16(mlp_w1), _to_bf16(mlp_w2)
    row = lambda v: v[None, :]
    for layer in range(depth):
        need_ctx = layer < depth - 1
        n_tok = n_lat + n_ctx if need_ctx else n_lat
        mod = mod_tok[layer]
        n1 = row(norm1_w[layer])
        i = layer // 2
        if layer % 2 == 0:
            ew = _even_weights(even_w_in[i], mla_w_uq[i], mla_w_ukv[i], gla_gate_up[i], gla_gate_bias[i])
            q, k, v, gq, gk, gv, gg, lg = _even_project(
                h, mod, n1, ew, row(mla_q_norm[i]), row(mla_kv_norm[i]), tabs, tm, n_lat)
            att = _attention(q, k, v, n_lat, n_ctx, need_ctx)
            o_f, o_b = _gla_scan(gq, gk, gv, lg, n_lat, n_ctx)
            parts, on, wo = [att, o_f, o_b, gg], gla_o_norm[i], even_w_out[i]
        else:
            q, k, v, g, gates = _odd_project(h, mod, n1, gdn_w_in[i], gdn_conv_w[i], gdn_a_log[i],
                                             gdn_dt_bias[i], tm, n_lat)
            o_f, o_b = _gdn_scan(q, k, v, gates, n_lat, n_ctx)
            parts, on, wo = [o_f, o_b, g], gdn_o_norm[i], gdn_w_out[i]
        h = _merge_mlp(parts, row(on), wo.astype(BF16), h, mod, row(norm2_w[layer]), w1_bf[layer], w2_bf[layer],
                       tm, n_lat, n_tok, final_norm=None if need_ctx else row(final_norm))
    return h
```

```python
import functools

import jax
import jax.numpy as jnp
from jax import lax
from jax.experimental import pallas as pl
from jax.experimental.pallas import tpu as pltpu

F32 = jnp.float32
BF16 = jnp.bfloat16
EPS = 1e-6
LOG2E = 1.4426950408889634

GRID_W = 64
N_MOD = 6
MLA_HEADS = 8
MLA_NOPE = 64
MLA_ROPE = 32
MLA_V = 64
MLA_QK = MLA_NOPE + MLA_ROPE
MLA_Q_RANK = 384
MLA_KV_RANK = 256
ROPE_THETA = 10000.0
GLA_HEADS = 4
GLA_DK = 64
GLA_DV = 128
GLA_GATE_RANK = 16
GLA_GATE_NORM = 16.0
GDN_HEADS = 8
GDN_DK = 64
GDN_DV = 128
CHUNK = 64
MLP_HIDDEN_CHUNK = 1024
GDN_SCAN_HEADS = 4
LANES = 128
HEAD_SLOT = 128
V_SLOT = 80
VMEM_LIMIT = 56 * 1024 * 1024


def _cparams(sem):
    return pltpu.CompilerParams(dimension_semantics=sem, vmem_limit_bytes=VMEM_LIMIT)


def _dot(a, b):
    return jnp.dot(a.astype(BF16), b.astype(BF16), preferred_element_type=F32)


def _dot_nt(a, b):
    return lax.dot_general(a.astype(BF16), b.astype(BF16), (((1,), (1,)), ((), ())),
                           preferred_element_type=F32)


def _dot_tn(a, b):
    return lax.dot_general(a.astype(BF16), b.astype(BF16), (((0,), (0,)), ((), ())),
                           preferred_element_type=F32)


def _split(x, n):
    out = []
    for _ in range(n - 1):
        p = x.astype(BF16)
        out.append(p)
        x = x - p.astype(F32)
    out.append(x.astype(BF16))
    return out


def _dot3(a, b):
    a1, a2 = _split(a, 2)
    b1, b2 = _split(b, 2)
    d = functools.partial(jnp.dot, preferred_element_type=F32)
    return d(a1, b1) + (d(a1, b2) + d(a2, b1))


def _dot_exact_lhs(m_bf, x, nt=False):
    acc = None
    for p in _split(x, 3):
        if nt:
            t = lax.dot_general(p, m_bf, (((1,), (1,)), ((), ())), preferred_element_type=F32)
        else:
            t = jnp.dot(m_bf, p, preferred_element_type=F32)
        acc = t if acc is None else acc + t
    return acc


def _rms(x, w):
    return x * lax.rsqrt(jnp.mean(x * x, axis=-1, keepdims=True) + EPS) * w


def _sigmoid(x):
    return 1.0 / (1.0 + jnp.exp(-x))


def _silu(x):
    return x * _sigmoid(x)


def _softplus(x):
    return jnp.maximum(x, 0.0) + jnp.log1p(jnp.exp(-jnp.abs(x)))


def _tile_lanes(x, n):
    return jnp.concatenate([x] * n, axis=-1)


def _stage(fn, *cols):
    return [fn(*a) for a in zip(*cols)]


def _ada_kernel(cc_ref, w_ref, b_ref, o_ref):
    o_ref[0] = _dot(_silu(cc_ref[...]), w_ref[0]) + b_ref[0]


def _ada_all(cc, ada_w, ada_b):
    depth, d, n = ada_w.shape
    rows = cc.shape[0]
    tn = 1024
    return pl.pallas_call(
        _ada_kernel,
        grid=(depth, n // tn),
        in_specs=[pl.BlockSpec((rows, d), lambda l, j: (0, 0)),
                  pl.BlockSpec((1, d, tn), lambda l, j: (l, 0, j)),
                  pl.BlockSpec((1, 1, tn), lambda l, j: (l, 0, j))],
        out_specs=pl.BlockSpec((1, rows, tn), lambda l, j: (l, 0, j)),
        out_shape=jax.ShapeDtypeStruct((depth, rows, n), F32),
        compiler_params=_cparams(("parallel", "parallel")),
        name="ada_mod",
    )(cc, ada_w, ada_b.reshape(depth, 1, n))


_E_CQ = (0, 384)
_E_CKV = (384, 640)
_E_KP = (640, 768)
_E_KPR = (768, 896)
_E_GQ = (896, 1152)
_E_GK = (1152, 1408)
_E_GV = (1408, 1920)
_E_GG = (1920, 2432)
_E_GLOW = (2432, 2560)
_E_N = 2560


def _even_proj_kernel(h_ref, mod_ref, n1_ref, win_ref, qn_ref, wq_ref, kvn_ref, wk_ref, wv_ref,
                      wg_ref, gb_ref, c_ref, s_ref, ctt_ref, stt_ref,
                      q_out, k_out, v_out, gq_out, gk_out, gv_out, gg_out, lg_out):
    mod = mod_ref[...]
    a = _rms(h_ref[...], n1_ref[...]) * (1.0 + mod[1:2]) + mod[0:1]
    y = _dot(a, win_ref[...])
    sl = lambda r: y[:, r[0]:r[1]]
    ct, st = c_ref[...], s_ref[...]
    nh = MLA_HEADS
    qq = _dot_nt(wq_ref[...], _rms(sl(_E_CQ), qn_ref[...]))
    wq = nh * HEAD_SLOT
    tile_rows = lambda x: jnp.concatenate([x] * nh, axis=0)
    q_out[...] = (qq[:wq] * tile_rows(ctt_ref[...]) + qq[wq:] * tile_rows(stt_ref[...])).astype(q_out.dtype)
    ckvn = _rms(sl(_E_CKV), kvn_ref[...])
    kpe = sl(_E_KP) * ct + sl(_E_KPR) * st
    k_out[...] = (_dot(ckvn, wk_ref[...]) + _tile_lanes(kpe, nh)).astype(k_out.dtype)
    vt = _dot_nt(wv_ref[...], ckvn)
    pad_rows = lax.broadcasted_iota(jnp.int32, (V_SLOT - MLA_V, vt.shape[1]), 0)
    ones_row = jnp.where(pad_rows == 0, 1.0, 0.0)
    v_out[...] = jnp.concatenate(
        [x for h in range(nh) for x in (vt[h * MLA_V:(h + 1) * MLA_V], ones_row)], axis=0).astype(v_out.dtype)
    gq_out[...] = sl(_E_GQ)
    gk_out[...] = sl(_E_GK)
    gv_out[...] = sl(_E_GV)
    gg_out[...] = sl(_E_GG)
    logit = _dot(sl(_E_GLOW), wg_ref[...]) + gb_ref[...]
    lg_out[...] = (jnp.minimum(logit, 0.0) - jnp.log1p(jnp.exp(-jnp.abs(logit)))) * (1.0 / GLA_GATE_NORM)


def _rot_half_cols(w):
    f = MLA_ROPE // 4
    return jnp.concatenate([-w[..., f:2 * f], w[..., 0:f], -w[..., 3 * f:4 * f], w[..., 2 * f:3 * f]], axis=-1)


def _even_weights(w_in, w_uq, w_ukv, gate_up, gate_bias):
    d = w_in.shape[0]
    cq, ckv, kpe, gq, gk, gv, gg, glow = jnp.split(
        w_in, [384, 640, 672, 928, 1184, 1696, 2208], axis=1)
    z = lambda n: jnp.zeros((d, n), F32)
    slot = lambda w: jnp.concatenate([z(MLA_NOPE), w, z(HEAD_SLOT - MLA_QK)], axis=1)
    win = jnp.concatenate([cq, ckv, slot(kpe), slot(_rot_half_cols(kpe)), gq, gk, gv, gg, glow,
                           z(LANES - 2 * GLA_GATE_RANK)], axis=1).astype(BF16)
    r = w_uq.shape[0]
    wq = (w_uq * (MLA_QK ** -0.5 * LOG2E)).reshape(r, MLA_HEADS, MLA_QK)
    nope, pe = wq[..., :MLA_NOPE], wq[..., MLA_NOPE:]
    zq = lambda n: jnp.zeros((r, MLA_HEADS, n), F32)
    pad = HEAD_SLOT - MLA_QK
    wq_a = jnp.concatenate([nope, pe, zq(pad)], axis=-1).reshape(r, -1)
    wq_b = jnp.concatenate([zq(MLA_NOPE), _rot_half_cols(pe), zq(pad)], axis=-1).reshape(r, -1)
    wq_all = jnp.concatenate([wq_a, wq_b], axis=1).T.astype(BF16)
    rk = w_ukv.shape[0]
    wkv = w_ukv.reshape(rk, MLA_HEADS, MLA_NOPE + MLA_V)
    wk = jnp.concatenate([wkv[..., :MLA_NOPE], jnp.zeros((rk, MLA_HEADS, HEAD_SLOT - MLA_NOPE), F32)],
                         axis=-1).reshape(rk, -1).astype(BF16)
    wv = wkv[..., MLA_NOPE:].reshape(rk, -1).T.astype(BF16)
    hk = GLA_HEADS * GLA_DK
    wg = jnp.zeros((LANES, 2 * hk), F32)
    wg = wg.at[0:GLA_GATE_RANK, 0:hk].set(gate_up[0])
    wg = wg.at[GLA_GATE_RANK:2 * GLA_GATE_RANK, hk:].set(gate_up[1])
    gb = jnp.concatenate([gate_bias[0], gate_bias[1]])[None, :]
    return win, wq_all, wk, wv, wg.astype(BF16), gb


def _rope_tables(n_lat, n_ctx):
    rows = n_lat // GRID_W
    row = jnp.repeat(jnp.arange(rows, dtype=F32), GRID_W)
    col = jnp.tile(jnp.arange(GRID_W, dtype=F32), rows)
    axis_dim = MLA_ROPE // 2
    inv_freq = ROPE_THETA ** (-jnp.arange(0, axis_dim, 2, dtype=F32) / axis_dim)
    ar, ac = row[:, None] * inv_freq, col[:, None] * inv_freq
    c32 = jnp.concatenate([jnp.cos(ar), jnp.cos(ar), jnp.cos(ac), jnp.cos(ac)], axis=1)
    s32 = jnp.concatenate([jnp.sin(ar), jnp.sin(ar), jnp.sin(ac), jnp.sin(ac)], axis=1)
    pad = HEAD_SLOT - MLA_QK
    ct = jnp.concatenate([jnp.ones((n_lat, MLA_NOPE), F32), c32, jnp.zeros((n_lat, pad), F32)], axis=1)
    st = jnp.concatenate([jnp.zeros((n_lat, MLA_NOPE), F32), s32, jnp.zeros((n_lat, pad), F32)], axis=1)
    cc = jnp.concatenate([jnp.ones((n_ctx, MLA_QK), F32), jnp.zeros((n_ctx, pad), F32)], axis=1)
    ct = jnp.concatenate([ct, cc], axis=0)
    st = jnp.concatenate([st, jnp.zeros((n_ctx, HEAD_SLOT), F32)], axis=0)
    return ct, st, ct.T, st.T


def _full(shape):
    nd = len(shape)
    return pl.BlockSpec(shape, lambda *_: (0,) * nd)


def _even_project(h, mod, n1, ew, qn, kvn, tabs, tm, n_lat):
    b, t, d = h.shape
    win, wq, wk, wv, wg, gb = ew
    ct, st, ctt, stt = tabs
    nlb = n_lat // tm
    tok = lambda n: pl.BlockSpec((None, tm, n), lambda i, j: (i, j, 0))
    tab = pl.BlockSpec((tm, HEAD_SLOT), lambda i, j: (j, 0))
    tabt = pl.BlockSpec((HEAD_SLOT, tm), lambda i, j: (0, j))
    hk, hv = GLA_HEADS * GLA_DK, GLA_HEADS * GLA_DV
    feat = lambda n: pl.BlockSpec((None, n, tm), lambda i, j: (i, 0, j))
    tshape = lambda n, dt: jax.ShapeDtypeStruct((b, t, n), dt)
    fshape = lambda n, dt: jax.ShapeDtypeStruct((b, n, t), dt)
    nq, nv = MLA_HEADS * HEAD_SLOT, MLA_HEADS * V_SLOT
    return pl.pallas_call(
        _even_proj_kernel,
        grid=(b, t // tm),
        in_specs=[tok(d), pl.BlockSpec((None, None, N_MOD, d), lambda i, j: (i, j // nlb, 0, 0)),
                  _full(n1.shape), _full(win.shape), _full(qn.shape), _full(wq.shape), _full(kvn.shape),
                  _full(wk.shape), _full(wv.shape), _full(wg.shape), _full(gb.shape), tab, tab, tabt, tabt],
        out_specs=[feat(nq), tok(nq), feat(nv), tok(hk), tok(hk), tok(hv), tok(hv), tok(2 * hk)],
        out_shape=[fshape(nq, BF16), tshape(nq, BF16), fshape(nv, BF16), tshape(hk, F32), tshape(hk, F32),
                   tshape(hv, F32), tshape(hv, F32), tshape(2 * hk, F32)],
        compiler_params=_cparams(("parallel", "parallel")),
        name="even_proj",
    )(h, mod, n1, win, qn, wq, kvn, wk, wv, wg, gb, ct, st, ctt, stt)


def _attn_kernel(qt_ref, k_ref, vt_ref, *rest):
    o_ref = rest[-1]
    tk_all = k_ref.shape[0]
    tk = 256 if tk_all % 256 == 0 else 128
    heads = range(2)
    nkb = tk_all // tk
    qts = [qt_ref[hh * HEAD_SLOT:(hh + 1) * HEAD_SLOT, :] for hh in heads]

    def scores(kb):
        return [jnp.dot(k_ref[kb * tk:(kb + 1) * tk, hh * HEAD_SLOT:(hh + 1) * HEAD_SLOT], qts[hh],
                        preferred_element_type=F32) for hh in heads]

    s_next = scores(0)
    m = acc = None
    for kb in range(nkb):
        s = s_next
        if kb + 1 < nkb:
            s_next = scores(kb + 1)
        bm = [jnp.max(x, axis=0, keepdims=True) for x in s]
        if m is None:
            m_new = bm
        else:
            m_new = [jnp.maximum(a, b_) for a, b_ in zip(m, bm)]
        p = [jnp.exp2(x - mn) for x, mn in zip(s, m_new)]
        pv = [jnp.dot(vt_ref[hh * V_SLOT:(hh + 1) * V_SLOT, kb * tk:(kb + 1) * tk], p[hh].astype(BF16),
                      preferred_element_type=F32) for hh in heads]
        if m is None:
            acc = pv
        else:
            acc = [jnp.exp2(a - b_) * x + y for a, b_, x, y in zip(m, m_new, acc, pv)]
        m = m_new
    o_ref[...] = jnp.concatenate([(a[:MLA_V] / a[MLA_V:MLA_V + 1]).T for a in acc], axis=-1).astype(o_ref.dtype)


def _attention(qt, k, vt, n_lat, n_ctx, need_ctx):
    b, t, _ = k.shape
    hp = MLA_HEADS // 2
    tq = 512 if n_lat % 512 == 0 else 128
    out_shape = jax.ShapeDtypeStruct((b, t, MLA_HEADS * MLA_V), BF16)
    att = pl.pallas_call(
        _attn_kernel,
        grid=(b, hp, n_lat // tq),
        in_specs=[pl.BlockSpec((None, 2 * HEAD_SLOT, tq), lambda i, h, j: (i, h, j)),
                  pl.BlockSpec((None, t, 2 * HEAD_SLOT), lambda i, h, j: (i, 0, h)),
                  pl.BlockSpec((None, 2 * V_SLOT, t), lambda i, h, j: (i, h, 0))],
        out_specs=pl.BlockSpec((None, tq, 2 * MLA_V), lambda i, h, j: (i, j, h)),
        out_shape=out_shape,
        compiler_params=_cparams(("parallel", "parallel", "arbitrary")),
        name="mla_attn_lat",
    )(qt, k, vt)
    if not need_ctx:
        return att
    cb = t // n_ctx - 1
    return pl.pallas_call(
        _attn_kernel,
        grid=(b, hp),
        in_specs=[pl.BlockSpec((None, 2 * HEAD_SLOT, n_ctx), lambda i, h: (i, h, cb)),
                  pl.BlockSpec((None, n_ctx, 2 * HEAD_SLOT), lambda i, h: (i, cb, h)),
                  pl.BlockSpec((None, 2 * V_SLOT, n_ctx), lambda i, h: (i, h, cb)),
                  pl.BlockSpec(memory_space=pl.ANY)],
        out_specs=pl.BlockSpec((None, n_ctx, 2 * MLA_V), lambda i, h: (i, cb, h)),
        out_shape=out_shape,
        input_output_aliases={3: 0},
        compiler_params=_cparams(("parallel", "parallel")),
        name="mla_attn_ctx",
    )(qt, k, vt, att)


def _scan_maps(nlb, ncb):
    fwd = lambda j: jnp.where(j < ncb, nlb + j, j - ncb)
    bwd = lambda j: nlb + ncb - 1 - j
    return fwd, bwd


def _scan_blk(n_lat, n_ctx):
    for blk in (256, 128, 64):
        if n_lat % blk == 0 and n_ctx % blk == 0:
            return blk
    raise ValueError("sequence lengths must be multiples of the 64-token chunk")


def _gla_kernel(qf, kf, vf, lf, qb, kb, vb, lb, of, ob, sf, sb):
    @pl.when(pl.program_id(1) == 0)
    def _():
        sf[...] = jnp.zeros_like(sf)
        sb[...] = jnp.zeros_like(sb)

    hk, hv = GLA_HEADS * GLA_DK, GLA_HEADS * GLA_DV
    i32 = jnp.int32
    io = lambda shape, d: lax.broadcasted_iota(i32, shape, d)
    blk = qf.shape[0]
    nc = blk // CHUNK
    aj = io((CHUNK, GLA_HEADS * CHUNK), 1) % CHUNK
    ai = io((CHUNK, GLA_HEADS * CHUNK), 0)
    a_incl = (ai >= aj, ai <= aj)
    klane = io((CHUNK, hk), 1) // GLA_DK
    vlane = io((CHUNK, hv), 1) // GLA_DV
    sdiag = (io((hv, hk), 0) // GLA_DV) == (io((hv, hk), 1) // GLA_DK)
    bi, bj = io((blk, blk), 0), io((blk, blk), 1)
    same_chunk = (bi // CHUNK) == (bj // CHUNK)

    zs, cs, qs, ks, vs, bcs, bts = ([] for _ in range(7))
    for z, (q, k, v, lg) in enumerate(((qf, kf, vf, lf), (qb, kb, vb, lb))):
        tri = jnp.where(jnp.logical_and(same_chunk, (bi <= bj) if z else (bi >= bj)), 1.0, 0.0).astype(BF16)
        bcum = _dot_exact_lhs(tri, lg[:, z * hk:(z + 1) * hk])
        for c in range(nc):
            r = slice(c * CHUNK, (c + 1) * CHUNK)
            last = c * CHUNK if z else (c + 1) * CHUNK - 1
            zs.append(z), cs.append(c)
            qs.append(q[r, :]), ks.append(k[r, :]), vs.append(v[r, :])
            bcs.append(bcum[r, :]), bts.append(bcum[last:last + 1, :])

    qds = _stage(lambda q, b_: q * (GLA_DK ** -0.5) * jnp.exp(b_), qs, bcs)
    kbs = _stage(lambda k, b_: jnp.concatenate(
        [jnp.where(klane == h, k * jnp.exp(-b_), 0.0) for h in range(GLA_HEADS)], axis=0), ks, bcs)
    a_s = _stage(lambda z, qd, kb_: jnp.where(a_incl[z], _dot_nt(qd, kb_), 0.0), zs, qds, kbs)
    vbs = _stage(lambda v: jnp.concatenate([jnp.where(vlane == h, v, 0.0) for h in range(GLA_HEADS)], axis=0), vs)
    ois = _stage(_dot, a_s, vbs)
    upds = _stage(lambda v, k, b_, bt: jnp.where(sdiag, _dot_tn(v, k * jnp.exp(bt - b_)), 0.0),
                  vs, ks, bcs, bts)
    ges = _stage(jnp.exp, bts)

    chain = {(z, c): i for i, (z, c) in enumerate(zip(zs, cs))}
    state = [sf[...], sb[...]]
    outs = (of, ob)
    for step in range(nc):
        for z in range(2):
            c = nc - 1 - step if z else step
            i = chain[(z, c)]
            outs[z][c * CHUNK:(c + 1) * CHUNK, :] = ois[i] + _dot_nt(qds[i], state[z])
            state[z] = state[z] * ges[i] + upds[i]
    sf[...] = state[0]
    sb[...] = state[1]


def _gla_scan(gq, gk, gv, lg, n_lat, n_ctx):
    b, t, hk = gq.shape
    hv = gv.shape[-1]
    blk = _scan_blk(n_lat, n_ctx)
    nlb, ncb = n_lat // blk, n_ctx // blk
    fwd, bwd = _scan_maps(nlb, ncb)
    spec = lambda n, m: pl.BlockSpec((None, blk, n), lambda i, j: (i, m(j), 0))
    out = jax.ShapeDtypeStruct((b, t, hv), F32)
    return pl.pallas_call(
        _gla_kernel,
        grid=(b, nlb + ncb),
        in_specs=[spec(hk, fwd), spec(hk, fwd), spec(hv, fwd), spec(2 * hk, fwd),
                  spec(hk, bwd), spec(hk, bwd), spec(hv, bwd), spec(2 * hk, bwd)],
        out_specs=[spec(hv, fwd), spec(hv, bwd)],
        out_shape=[out, out],
        scratch_shapes=[pltpu.VMEM((hv, hk), F32), pltpu.VMEM((hv, hk), F32)],
        compiler_params=_cparams(("parallel", "arbitrary")),
        name="gla_scan",
    )(gq, gk, gv, lg, gq, gk, gv, lg)


def _cast_kernel(x_ref, o_ref):
    o_ref[...] = x_ref[...].astype(o_ref.dtype)


def _to_bf16(w):
    depth, r, c = w.shape
    tr = 512 if r % 512 == 0 else r
    spec = pl.BlockSpec((None, tr, c), lambda l, i: (l, i, 0))
    return pl.pallas_call(
        _cast_kernel,
        grid=(depth, r // tr),
        in_specs=[spec],
        out_specs=spec,
        out_shape=jax.ShapeDtypeStruct(w.shape, BF16),
        compiler_params=_cparams(("parallel", "parallel")),
        name="cast_bf16",
    )(w)


def _head_norm_gate(of_ref, ob_ref, g_ref, on_ref, n_heads):
    x = of_ref[...] + ob_ref[...]
    g = g_ref[...]
    outs = []
    for h in range(n_heads):
        s = slice(h * LANES, (h + 1) * LANES)
        outs.append(_rms(x[:, s], on_ref[...]) * _silu(g[:, s]))
    return jnp.concatenate(outs, axis=-1)


def _merge_mlp_kernel(*refs, n_parts, final):
    parts = refs[:n_parts]
    on_ref, wo_ref, h_ref, mod_ref, n2_ref, w1_ref, w2_ref = refs[n_parts:n_parts + 7]
    o_ref = refs[-1]
    mod = mod_ref[...]
    if n_parts == 4:
        att_ref = parts[0]
        na = att_ref.shape[-1]
        y = _head_norm_gate(*parts[1:], on_ref, GLA_HEADS)
        o = jnp.dot(att_ref[...], wo_ref[0:na, :], preferred_element_type=F32) + _dot(y, wo_ref[na:, :])
    else:
        o = _dot(_head_norm_gate(*parts, on_ref, GDN_HEADS), wo_ref[...])
    h = h_ref[...] + mod[2:3] * o
    a = (_rms(h, n2_ref[...]) * (1.0 + mod[4:5]) + mod[3:4]).astype(BF16)
    ff = w1_ref.shape[1]
    hc = min(ff, MLP_HIDDEN_CHUNK)
    acc = None
    for c in range(ff // hc):
        u = jnp.maximum(jnp.dot(a, w1_ref[:, c * hc:(c + 1) * hc], preferred_element_type=F32), 0.0)
        t = _dot(u * u, w2_ref[c * hc:(c + 1) * hc, :])
        acc = t if acc is None else acc + t
    out = h + mod[5:6] * acc
    if final:
        out = _rms(out, refs[n_parts + 7][...])
    o_ref[...] = out


def _merge_mlp(parts, on, wo, h, mod, n2, w1, w2, layer, tm, n_lat, n_tok, final_norm=None):
    b, _, d = h.shape
    nlb = n_lat // tm
    tok = lambda n: pl.BlockSpec((None, tm, n), lambda i, j: (i, j, 0))
    once = lambda a: pl.BlockSpec(a.shape, lambda *_: (0,) * a.ndim, pipeline_mode=pl.Buffered(1))
    slab = lambda a: pl.BlockSpec((None,) + a.shape[1:], lambda *_: (layer, 0, 0), pipeline_mode=pl.Buffered(1))
    args = [*parts, on, wo, h, mod, n2, w1, w2]
    specs = [tok(p.shape[-1]) for p in parts] + [
        _full(on.shape), once(wo), tok(d),
        pl.BlockSpec((None, None, N_MOD, d), lambda i, j: (i, j // nlb, 0, 0)),
        _full(n2.shape), slab(w1), slab(w2)]
    if final_norm is not None:
        args.append(final_norm)
        specs.append(_full(final_norm.shape))
    return pl.pallas_call(
        functools.partial(_merge_mlp_kernel, n_parts=len(parts), final=final_norm is not None),
        grid=(b, n_tok // tm),
        in_specs=specs,
        out_specs=tok(d),
        out_shape=jax.ShapeDtypeStruct((b, n_tok, d), F32),
        compiler_params=_cparams(("parallel", "parallel")),
        name="merge_mlp",
    )(*args)


_O_QKV = 2 * GDN_HEADS * GDN_DK + GDN_HEADS * GDN_DV
_O_QK = 2 * GDN_HEADS * GDN_DK
_O_G = GDN_HEADS * GDN_DV
_O_N = _O_QKV + _O_G + LANES
HALO = 8


def _odd_proj_kernel(h_ref, hp_ref, hn_ref, mod_ref, n1_ref, win_ref, cw_ref, e_ref, et_ref, rs_ref,
                     al_ref, dt_ref, q_out, k_out, v_out, g_out, gt_out, *, nlb, nb):
    j = pl.program_id(1)
    mod = mod_ref[...]
    n1 = n1_ref[...]
    pre = lambda x: (_rms(x, n1) * (1.0 + mod[1:2]) + mod[0:1]).astype(BF16)
    a, a_prev, a_next = pre(h_ref[...]), pre(hp_ref[...]), pre(hn_ref[...])
    first = jnp.logical_or(j == 0, j == nlb)
    last = jnp.logical_or(j == nlb - 1, j == nb - 1)
    tm = a.shape[0]
    hq = GDN_HEADS * GDN_DK
    gw = hq
    rows = lax.broadcasted_iota(jnp.int32, (tm, gw), 0)
    mm = lambda lhs, c0, c1: jnp.dot(lhs, win_ref[:, c0:c1], preferred_element_type=F32)

    def project(g):
        c0, c1 = g * gw, (g + 1) * gw
        return (mm(a, c0, c1), jnp.where(first, 0.0, mm(a_prev, c0, c1)[HALO - 1:HALO]),
                jnp.where(last, 0.0, mm(a_next, c0, c1)[0:1]))

    pending = project(0)
    for g in range(_O_QKV // gw):
        x, prev, nxt = pending
        if (g + 1) * gw < _O_QKV:
            pending = project(g + 1)
        else:
            y_tail = mm(a, _O_QKV, _O_N)
        x_prev = jnp.where(rows == 0, prev, pltpu.roll(x, 1, axis=0))
        x_next = jnp.where(rows == tm - 1, nxt, pltpu.roll(x, tm - 1, axis=0))
        cw = cw_ref[:, g * gw:(g + 1) * gw]
        s = _silu(cw[0:1] * x_prev + cw[1:2] * x + cw[2:3] * x_next)
        if g < 2:
            ss = _dot(s * s, e_ref[g * gw:(g + 1) * gw, :])
            r = lax.rsqrt(ss + EPS) * rs_ref[...]
            et = et_ref[:, g * gw:(g + 1) * gw]
            rf = sum(jnp.dot(p, et, preferred_element_type=F32) for p in _split(r, 2))
            (q_out, k_out)[g][...] = s * rf
        else:
            v_out[:, (g - 2) * gw:(g - 1) * gw] = s
    g_out[...] = y_tail[:, 0:_O_G]
    t = y_tail[:, _O_G:]
    log_a = -jnp.exp(al_ref[...]) * _softplus(t + dt_ref[...])
    lane = lax.broadcasted_iota(jnp.int32, t.shape, 1)
    gt_out[...] = jnp.where(lane < 2 * GDN_HEADS, log_a, _sigmoid(t))


def _odd_project(h, mod, n1, w_in, conv_w, a_log, dt_bias, tm, n_lat):
    b, t, d = h.shape
    nlb, nb = n_lat // tm, t // tm
    win = jnp.concatenate([w_in, jnp.zeros((d, _O_N - w_in.shape[1]), F32)], axis=1).astype(BF16)
    nqk = 2 * GDN_HEADS
    e = (jnp.arange(_O_QK)[:, None] // GDN_DK == jnp.arange(LANES)[None, :]).astype(BF16)
    et = e.T
    rs = jnp.concatenate([jnp.full((GDN_HEADS,), GDN_DK ** -0.5, F32), jnp.ones((LANES - GDN_HEADS,), F32)])[None, :]
    padl = lambda v: jnp.concatenate([v.reshape(-1), jnp.zeros((LANES - v.size,), F32)])[None, :]
    al, dt = padl(a_log), padl(dt_bias)
    tok = lambda n: pl.BlockSpec((None, tm, n), lambda i, j: (i, j, 0))
    hb = tm // HALO
    outs = [GDN_HEADS * GDN_DK, GDN_HEADS * GDN_DK, GDN_HEADS * GDN_DV, _O_G, LANES]
    return pl.pallas_call(
        functools.partial(_odd_proj_kernel, nlb=nlb, nb=nb),
        grid=(b, nb),
        in_specs=[tok(d),
                  pl.BlockSpec((None, HALO, d), lambda i, j: (i, jnp.maximum(j * hb - 1, 0), 0)),
                  pl.BlockSpec((None, HALO, d), lambda i, j: (i, jnp.minimum((j + 1) * hb, t // HALO - 1), 0)),
                  pl.BlockSpec((None, None, N_MOD, d), lambda i, j: (i, j // nlb, 0, 0)),
                  _full(n1.shape), _full(win.shape), _full(conv_w.shape), _full(e.shape), _full(et.shape),
                  _full(rs.shape), _full(al.shape), _full(dt.shape)],
        out_specs=[tok(n) for n in outs],
        out_shape=[jax.ShapeDtypeStruct((b, t, n), F32) for n in outs],
        compiler_params=_cparams(("parallel", "parallel")),
        name="odd_proj",
    )(h, h, h, mod, n1, win, conv_w, e, et, rs, al, dt)


def _tri_inverse_all(ns, masks):
    lvl = masks["lvl"]
    ts = _stage(lambda n: masks["eye"] - jnp.where(lvl[0], n, 0.0), ns)
    for off in lvl[1:]:
        xs = _stage(lambda t, n: _dot(t, jnp.where(off, n, 0.0)), ts, ns)
        ts = _stage(lambda t, x: t - _dot(x, t), ts, xs)
    return ts


def _gdn_kernel(qf, kf, vf, gcf, grf, qb, kb, vb, gcb, grb, of, ob, s_ref):
    @pl.when(pl.program_id(2) == 0)
    def _():
        s_ref[...] = jnp.zeros_like(s_ref)

    blk = qf.shape[0]
    nc = blk // CHUNK
    nh = qf.shape[1] // GDN_DK
    i32 = jnp.int32
    ri = lax.broadcasted_iota(i32, (CHUNK, CHUNK), 0)
    ci = lax.broadcasted_iota(i32, (CHUNK, CHUNK), 1)
    same = [(ri // w) == (ci // w) for w in (2, 4, 8, 16, 32)]
    lvl = [same[0]] + [jnp.logical_and(same[i], jnp.logical_not(same[i - 1])) for i in range(1, 5)]
    lvl.append(jnp.logical_not(same[4]))
    masks = dict(lvl=lvl, eye=jnp.where(ri == ci, 1.0, 0.0).astype(F32))
    incl = (ri >= ci, ri <= ci)
    strict = (ri > ci, ri < ci)
    bi = lax.broadcasted_iota(i32, (blk, blk), 0)
    bj = lax.broadcasted_iota(i32, (blk, blk), 1)
    same_chunk = (bi // CHUNK) == (bj // CHUNK)

    zs, cs, hs, qs, ks, vs, gbs, bbs, grows, brows, gls = ([] for _ in range(11))
    sq = (CHUNK, CHUNK)
    for z, (q, k, v, gc, gr) in enumerate(((qf, kf, vf, gcf, grf), (qb, kb, vb, gcb, grb))):
        tri = jnp.where(jnp.logical_and(same_chunk, (bi <= bj) if z else (bi >= bj)), 1.0, 0.0).astype(BF16)
        gcv, grv = gc[...], gr[...]
        cum_col = _dot_exact_lhs(tri, gcv)
        cum_row = _dot_exact_lhs(tri, grv, nt=True)
        for c in range(nc):
            r = slice(c * CHUNK, (c + 1) * CHUNK)
            last = c * CHUNK if z else (c + 1) * CHUNK - 1
            for hh in range(nh):
                idx = nh * z + hh
                zs.append(z), cs.append(c), hs.append(hh)
                qs.append(q[r, hh * GDN_DK:(hh + 1) * GDN_DK])
                ks.append(k[r, hh * GDN_DK:(hh + 1) * GDN_DK])
                vs.append(v[r, hh * GDN_DV:(hh + 1) * GDN_DV])
                gbs.append(jnp.broadcast_to(cum_col[r, idx:idx + 1], sq))
                bbs.append(jnp.broadcast_to(gcv[r, 2 * nh + idx:2 * nh + idx + 1], sq))
                grows.append(cum_row[idx:idx + 1, r])
                brows.append(grv[2 * nh + idx:2 * nh + idx + 1, r])
                gls.append(cum_col[last:last + 1, idx:idx + 1])

    decays = _stage(lambda z, gb, gr_: jnp.where(incl[z], jnp.exp(jnp.where(incl[z], gb - gr_, 0.0)), 0.0),
                    zs, gbs, grows)
    qkk = _stage(lambda q, k: _dot_nt(jnp.concatenate([q, k], axis=0), k), qs, ks)
    ns = _stage(lambda z, x, bb, d_: jnp.where(strict[z], x[CHUNK:] * bb * d_, 0.0), zs, qkk, bbs, decays)
    ts = _tri_inverse_all(ns, masks)
    egs = _stage(jnp.exp, gbs)
    uws = _stage(lambda t, br, v, k, e_: _dot(t * br, jnp.concatenate([v, k * e_], axis=-1)),
                 ts, brows, vs, ks, egs)
    qks = _stage(lambda x, d_: x[:CHUNK] * d_, qkk, decays)
    kes = _stage(lambda k, gl, gb: k * jnp.exp(gl - gb), ks, gls, gbs)
    kuws = _stage(_dot_tn, kes, uws)
    quws = _stage(_dot, qks, uws)
    lhs = _stage(lambda q, e_, qu, ku: jnp.concatenate([q * e_ - qu[:, GDN_DV:], -ku[:, GDN_DV:]], axis=0),
                 qs, egs, quws, kuws)
    ges = _stage(jnp.exp, gls)

    chain = {(z, c, hh): i for i, (z, c, hh) in enumerate(zip(zs, cs, hs))}
    state = {(z, hh): s_ref[z, hh] for z in range(2) for hh in range(nh)}
    outs = (of, ob)
    for step in range(nc):
        for z in range(2):
            c = nc - 1 - step if z else step
            for hh in range(nh):
                i = chain[(z, c, hh)]
                s = state[(z, hh)]
                r = _dot(lhs[i], s)
                outs[z][c * CHUNK:(c + 1) * CHUNK, hh * GDN_DV:(hh + 1) * GDN_DV] = (
                    r[:CHUNK] + quws[i][:, :GDN_DV])
                state[(z, hh)] = s * ges[i] + (kuws[i][:, :GDN_DV] + r[CHUNK:])
    for (z, hh), s in state.items():
        s_ref[z, hh] = s


def _gdn_scan(q, k, v, gates, n_lat, n_ctx):
    b, t, _ = q.shape
    nh = GDN_SCAN_HEADS
    ng = GDN_HEADS // nh
    g = gates[..., :4 * GDN_HEADS].reshape(b, t, 2, 2, ng, nh)
    gcol = g.transpose(0, 4, 1, 2, 3, 5).reshape(b, ng, t, 4 * nh)
    grow = gcol.transpose(0, 1, 3, 2)
    blk = _scan_blk(n_lat, n_ctx)
    nlb, ncb = n_lat // blk, n_ctx // blk
    fwd, bwd = _scan_maps(nlb, ncb)
    d = None
    tok = lambda n, m: pl.BlockSpec((d, blk, n), lambda i, h, j: (i, m(j), h))
    gcs = lambda m: pl.BlockSpec((d, d, blk, 4 * nh), lambda i, h, j: (i, h, m(j), 0))
    grs = lambda m: pl.BlockSpec((d, d, 4 * nh, blk), lambda i, h, j: (i, h, 0, m(j)))
    out = jax.ShapeDtypeStruct((b, t, GDN_HEADS * GDN_DV), F32)
    dk2, dv2 = nh * GDN_DK, nh * GDN_DV
    return pl.pallas_call(
        _gdn_kernel,
        grid=(b, ng, nlb + ncb),
        in_specs=[tok(dk2, fwd), tok(dk2, fwd), tok(dv2, fwd), gcs(fwd), grs(fwd),
                  tok(dk2, bwd), tok(dk2, bwd), tok(dv2, bwd), gcs(bwd), grs(bwd)],
        out_specs=[tok(dv2, fwd), tok(dv2, bwd)],
        out_shape=[out, out],
        scratch_shapes=[pltpu.VMEM((2, nh, GDN_DK, GDN_DV), F32)],
        compiler_params=_cparams(("parallel", "parallel", "arbitrary")),
        name="gdn_scan",
    )(q, k, v, gcol, grow, q, k, v, gcol, grow)


def kernel(x, c, ctx, c_ctx, ada_w, ada_b, norm1_w, norm2_w, mlp_w1, mlp_w2, even_w_in, mla_q_norm,
           mla_w_uq, mla_kv_norm, mla_w_ukv, gla_gate_up, gla_gate_bias, gla_o_norm, even_w_out,
           gdn_w_in, gdn_conv_w, gdn_a_log, gdn_dt_bias, gdn_o_norm, gdn_w_out, final_norm):
    b, n_lat, d = x.shape
    n_ctx = ctx.shape[1]
    depth = ada_w.shape[0]
    tm = 256 if (n_lat % 256 == 0 and n_ctx % 256 == 0) else 128
    assert n_lat % tm == 0 and n_ctx % tm == 0 and n_lat % n_ctx == 0 and n_lat % GRID_W == 0

    rows = -(-(b + 1) // 8) * 8
    cc = jnp.concatenate([c, c_ctx[None, :], jnp.zeros((rows - b - 1, d), F32)], axis=0)
    mods = _ada_all(cc, ada_w, ada_b).reshape(depth, rows, N_MOD, d)
    mod_tok = jnp.stack([mods[:, :b], jnp.broadcast_to(mods[:, b:b + 1], (depth, b, N_MOD, d))], axis=2)

    h = jnp.concatenate([x, ctx], axis=1)
    tabs = _rope_tables(n_lat, n_ctx)
    w1_bf, w2_bf = _to_bf16(mlp_w1), _to_bf16(mlp_w2)
    row = lambda v: v[None, :]
    for layer in range(depth):
        need_ctx = layer < depth - 1
        n_tok = n_lat + n_ctx if need_ctx else n_lat
        mod = mod_tok[layer]
        n1 = row(norm1_w[layer])
        i = layer // 2
        if layer % 2 == 0:
            ew = _even_weights(even_w_in[i], mla_w_uq[i], mla_w_ukv[i], gla_gate_up[i], gla_gate_bias[i])
            q, k, v, gq, gk, gv, gg, lg = _even_project(
                h, mod, n1, ew, row(mla_q_norm[i]), row(mla_kv_norm[i]), tabs, tm, n_lat)
            att = _attention(q, k, v, n_lat, n_ctx, need_ctx)
            o_f, o_b = _gla_scan(gq, gk, gv, lg, n_lat, n_ctx)
            parts, on, wo = [att, o_f, o_b, gg], gla_o_norm[i], even_w_out[i]
        else:
            q, k, v, g, gates = _odd_project(h, mod, n1, gdn_w_in[i], gdn_conv_w[i], gdn_a_log[i],
                                             gdn_dt_bias[i], tm, n_lat)
            o_f, o_b = _gdn_scan(q, k, v, gates, n_lat, n_ctx)
            parts, on, wo = [o_f, o_b, g], gdn_o_norm[i], gdn_w_out[i]
        h = _merge_mlp(parts, row(on), wo.astype(BF16), h, mod, row(norm2_w[layer]), w1_bf, w2_bf, layer,
                       tm, n_lat, n_tok, final_norm=None if need_ctx else row(final_norm))
    return h
```

```python
import functools

import jax
import jax.numpy as jnp
from jax import lax
from jax.experimental import pallas as pl
from jax.experimental.pallas import tpu as pltpu

F32 = jnp.float32
BF16 = jnp.bfloat16
EPS = 1e-6
LOG2E = 1.4426950408889634

GRID_W = 64
N_MOD = 6
MLA_HEADS = 8
MLA_NOPE = 64
MLA_ROPE = 32
MLA_V = 64
MLA_QK = MLA_NOPE + MLA_ROPE
MLA_Q_RANK = 384
MLA_KV_RANK = 256
ROPE_THETA = 10000.0
GLA_HEADS = 4
GLA_DK = 64
GLA_DV = 128
GLA_GATE_RANK = 16
GLA_GATE_NORM = 16.0
GDN_HEADS = 8
GDN_DK = 64
GDN_DV = 128
CHUNK = 64
MLP_HIDDEN_CHUNK = 1024
GDN_SCAN_HEADS = 4
LANES = 128
HEAD_SLOT = 128
V_SLOT = 80
VMEM_LIMIT = 56 * 1024 * 1024


def _cparams(sem):
    return pltpu.CompilerParams(dimension_semantics=sem, vmem_limit_bytes=VMEM_LIMIT)


def _dot(a, b):
    return jnp.dot(a.astype(BF16), b.astype(BF16), preferred_element_type=F32)


def _dot_nt(a, b):
    return lax.dot_general(a.astype(BF16), b.astype(BF16), (((1,), (1,)), ((), ())),
                           preferred_element_type=F32)


def _dot_tn(a, b):
    return lax.dot_general(a.astype(BF16), b.astype(BF16), (((0,), (0,)), ((), ())),
                           preferred_element_type=F32)


def _split(x, n):
    out = []
    for _ in range(n - 1):
        p = x.astype(BF16)
        out.append(p)
        x = x - p.astype(F32)
    out.append(x.astype(BF16))
    return out


def _dot3(a, b):
    a1, a2 = _split(a, 2)
    b1, b2 = _split(b, 2)
    d = functools.partial(jnp.dot, preferred_element_type=F32)
    return d(a1, b1) + (d(a1, b2) + d(a2, b1))


def _dot_exact_lhs(m_bf, x, nt=False):
    acc = None
    for p in _split(x, 3):
        if nt:
            t = lax.dot_general(p, m_bf, (((1,), (1,)), ((), ())), preferred_element_type=F32)
        else:
            t = jnp.dot(m_bf, p, preferred_element_type=F32)
        acc = t if acc is None else acc + t
    return acc


def _rms(x, w):
    return x * lax.rsqrt(jnp.mean(x * x, axis=-1, keepdims=True) + EPS) * w


def _sigmoid(x):
    return 1.0 / (1.0 + jnp.exp(-x))


def _silu(x):
    return x * _sigmoid(x)


def _softplus(x):
    return jnp.maximum(x, 0.0) + jnp.log1p(jnp.exp(-jnp.abs(x)))


def _tile_lanes(x, n):
    return jnp.concatenate([x] * n, axis=-1)


def _stage(fn, *cols):
    return [fn(*a) for a in zip(*cols)]


def _ada_kernel(cc_ref, w_ref, b_ref, o_ref):
    o_ref[0] = _dot(_silu(cc_ref[...]), w_ref[0]) + b_ref[0]


def _ada_all(cc, ada_w, ada_b):
    depth, d, n = ada_w.shape
    rows = cc.shape[0]
    tn = 1024
    return pl.pallas_call(
        _ada_kernel,
        grid=(depth, n // tn),
        in_specs=[pl.BlockSpec((rows, d), lambda l, j: (0, 0)),
                  pl.BlockSpec((1, d, tn), lambda l, j: (l, 0, j)),
                  pl.BlockSpec((1, 1, tn), lambda l, j: (l, 0, j))],
        out_specs=pl.BlockSpec((1, rows, tn), lambda l, j: (l, 0, j)),
        out_shape=jax.ShapeDtypeStruct((depth, rows, n), F32),
        compiler_params=_cparams(("parallel", "parallel")),
        name="ada_mod",
    )(cc, ada_w, ada_b.reshape(depth, 1, n))


_E_CQ = (0, 384)
_E_CKV = (384, 640)
_E_ROT = (640, 768)
_E_GQ = (768, 1024)
_E_GK = (1024, 1280)
_E_GV = (1280, 1792)
_E_GG = (1792, 2304)
_E_N = 2304


def _even_proj_kernel(h_ref, mod_ref, n1_ref, win_ref, qn_ref, wq_ref, kvn_ref, wk_ref, wv_ref,
                      wg_ref, gb_ref, c_ref, s_ref, ctt_ref, stt_ref,
                      q_out, k_out, v_out, gq_out, gk_out, gv_out, gg_out, lg_out):
    mod = mod_ref[...]
    a = _rms(h_ref[...], n1_ref[...]) * (1.0 + mod[1:2]) + mod[0:1]
    y = _dot(a, win_ref[...])
    sl = lambda r: y[:, r[0]:r[1]]
    ct, st = c_ref[...], s_ref[...]
    nh = MLA_HEADS
    qq = _dot_nt(wq_ref[...], _rms(sl(_E_CQ), qn_ref[...]))
    c32, s32 = ctt_ref[...], stt_ref[...]
    rot0 = nh * MLA_QK
    zero_pad = jnp.zeros((HEAD_SLOT - MLA_QK, qq.shape[1]), F32)
    pieces = []
    for h in range(nh):
        nope = qq[h * MLA_QK:h * MLA_QK + MLA_NOPE]
        pe = qq[h * MLA_QK + MLA_NOPE:(h + 1) * MLA_QK]
        pe_rot = qq[rot0 + h * MLA_ROPE:rot0 + (h + 1) * MLA_ROPE]
        pieces += [nope, pe * c32 + pe_rot * s32, zero_pad]
    q_out[...] = jnp.concatenate(pieces, axis=0).astype(q_out.dtype)
    ckvn = _rms(sl(_E_CKV), kvn_ref[...])
    yr = sl(_E_ROT)
    kpe = pltpu.roll(yr, MLA_NOPE, axis=1) * ct + pltpu.roll(yr, MLA_NOPE - MLA_ROPE, axis=1) * st
    k_out[...] = (_dot(ckvn, wk_ref[...]) + _tile_lanes(kpe, nh)).astype(k_out.dtype)
    vt = _dot_nt(wv_ref[...], ckvn)
    pad_rows = lax.broadcasted_iota(jnp.int32, (V_SLOT - MLA_V, vt.shape[1]), 0)
    ones_row = jnp.where(pad_rows == 0, 1.0, 0.0)
    v_out[...] = jnp.concatenate(
        [x for h in range(nh) for x in (vt[h * MLA_V:(h + 1) * MLA_V], ones_row)], axis=0).astype(v_out.dtype)
    gq_out[...] = sl(_E_GQ)
    gk_out[...] = sl(_E_GK)
    gv_out[...] = sl(_E_GV)
    gg_out[...] = sl(_E_GG)
    logit = _dot(yr, wg_ref[...]) + gb_ref[...]
    lg_out[...] = (jnp.minimum(logit, 0.0) - jnp.log1p(jnp.exp(-jnp.abs(logit)))) * (1.0 / GLA_GATE_NORM)


def _rot_half_cols(w):
    f = MLA_ROPE // 4
    return jnp.concatenate([-w[..., f:2 * f], w[..., 0:f], -w[..., 3 * f:4 * f], w[..., 2 * f:3 * f]], axis=-1)


def _even_weights(w_in, w_uq, w_ukv, gate_up, gate_bias):
    d = w_in.shape[0]
    cq, ckv, kpe, gq, gk, gv, gg, glow = jnp.split(
        w_in, [384, 640, 672, 928, 1184, 1696, 2208], axis=1)
    lowr = 2 * GLA_GATE_RANK
    win = jnp.concatenate([cq, ckv, kpe, _rot_half_cols(kpe), glow, jnp.zeros((d, LANES - 2 * MLA_ROPE - lowr), F32),
                           gq, gk, gv, gg], axis=1).astype(BF16)
    r = w_uq.shape[0]
    wq = (w_uq * (MLA_QK ** -0.5 * LOG2E)).reshape(r, MLA_HEADS, MLA_QK)
    wq_rot = _rot_half_cols(wq[..., MLA_NOPE:])
    wq_all = jnp.concatenate([wq.reshape(r, -1), wq_rot.reshape(r, -1)], axis=1).T.astype(BF16)
    rk = w_ukv.shape[0]
    wkv = w_ukv.reshape(rk, MLA_HEADS, MLA_NOPE + MLA_V)
    wk = jnp.concatenate([wkv[..., :MLA_NOPE], jnp.zeros((rk, MLA_HEADS, HEAD_SLOT - MLA_NOPE), F32)],
                         axis=-1).reshape(rk, -1).astype(BF16)
    wv = wkv[..., MLA_NOPE:].reshape(rk, -1).T.astype(BF16)
    hk = GLA_HEADS * GLA_DK
    g0 = 2 * MLA_ROPE
    wg = jnp.zeros((LANES, 2 * hk), F32)
    wg = wg.at[g0:g0 + GLA_GATE_RANK, 0:hk].set(gate_up[0])
    wg = wg.at[g0 + GLA_GATE_RANK:g0 + 2 * GLA_GATE_RANK, hk:].set(gate_up[1])
    gb = jnp.concatenate([gate_bias[0], gate_bias[1]])[None, :]
    return win, wq_all, wk, wv, wg.astype(BF16), gb


def _rope_tables(n_lat, n_ctx):
    rows = n_lat // GRID_W
    row = jnp.repeat(jnp.arange(rows, dtype=F32), GRID_W)
    col = jnp.tile(jnp.arange(GRID_W, dtype=F32), rows)
    axis_dim = MLA_ROPE // 2
    inv_freq = ROPE_THETA ** (-jnp.arange(0, axis_dim, 2, dtype=F32) / axis_dim)
    ar, ac = row[:, None] * inv_freq, col[:, None] * inv_freq
    c32 = jnp.concatenate([jnp.cos(ar), jnp.cos(ar), jnp.cos(ac), jnp.cos(ac)], axis=1)
    s32 = jnp.concatenate([jnp.sin(ar), jnp.sin(ar), jnp.sin(ac), jnp.sin(ac)], axis=1)
    c32 = jnp.concatenate([c32, jnp.ones((n_ctx, MLA_ROPE), F32)], axis=0)
    s32 = jnp.concatenate([s32, jnp.zeros((n_ctx, MLA_ROPE), F32)], axis=0)
    slot = lambda x: jnp.pad(x, ((0, 0), (MLA_NOPE, HEAD_SLOT - MLA_QK)))
    return slot(c32), slot(s32), c32.T, s32.T


def _full(shape):
    nd = len(shape)
    return pl.BlockSpec(shape, lambda *_: (0,) * nd)


def _even_project(h, mod, n1, ew, qn, kvn, tabs, tm, n_lat):
    b, t, d = h.shape
    win, wq, wk, wv, wg, gb = ew
    ct, st, ctt, stt = tabs
    nlb = n_lat // tm
    tok = lambda n: pl.BlockSpec((None, tm, n), lambda i, j: (i, j, 0))
    tab = pl.BlockSpec((tm, HEAD_SLOT), lambda i, j: (j, 0))
    tabt = pl.BlockSpec((MLA_ROPE, tm), lambda i, j: (0, j))
    hk, hv = GLA_HEADS * GLA_DK, GLA_HEADS * GLA_DV
    feat = lambda n: pl.BlockSpec((None, n, tm), lambda i, j: (i, 0, j))
    tshape = lambda n, dt: jax.ShapeDtypeStruct((b, t, n), dt)
    fshape = lambda n, dt: jax.ShapeDtypeStruct((b, n, t), dt)
    nq, nv = MLA_HEADS * HEAD_SLOT, MLA_HEADS * V_SLOT
    return pl.pallas_call(
        _even_proj_kernel,
        grid=(b, t // tm),
        in_specs=[tok(d), pl.BlockSpec((None, None, N_MOD, d), lambda i, j: (i, j // nlb, 0, 0)),
                  _full(n1.shape), _full(win.shape), _full(qn.shape), _full(wq.shape), _full(kvn.shape),
                  _full(wk.shape), _full(wv.shape), _full(wg.shape), _full(gb.shape), tab, tab, tabt, tabt],
        out_specs=[feat(nq), tok(nq), feat(nv), tok(hk), tok(hk), tok(hv), tok(hv), tok(2 * hk)],
        out_shape=[fshape(nq, BF16), tshape(nq, BF16), fshape(nv, BF16), tshape(hk, F32), tshape(hk, F32),
                   tshape(hv, F32), tshape(hv, F32), tshape(2 * hk, F32)],
        compiler_params=_cparams(("parallel", "parallel")),
        name="even_proj",
    )(h, mod, n1, win, qn, wq, kvn, wk, wv, wg, gb, ct, st, ctt, stt)


def _attn_kernel(qt_ref, k_ref, vt_ref, *rest):
    o_ref = rest[-1]
    tk_all = k_ref.shape[0]
    tk = 256 if tk_all % 256 == 0 else 128
    heads = range(2)
    nkb = tk_all // tk
    qts = [qt_ref[hh * HEAD_SLOT:(hh + 1) * HEAD_SLOT, :] for hh in heads]

    def scores(kb):
        return [jnp.dot(k_ref[kb * tk:(kb + 1) * tk, hh * HEAD_SLOT:(hh + 1) * HEAD_SLOT], qts[hh],
                        preferred_element_type=F32) for hh in heads]

    s_next = scores(0)
    m = acc = None
    for kb in range(nkb):
        s = s_next
        if kb + 1 < nkb:
            s_next = scores(kb + 1)
        bm = [jnp.max(x, axis=0, keepdims=True) for x in s]
        if m is None:
            m_new = bm
        else:
            m_new = [jnp.maximum(a, b_) for a, b_ in zip(m, bm)]
        p = [jnp.exp2(x - mn) for x, mn in zip(s, m_new)]
        pv = [jnp.dot(vt_ref[hh * V_SLOT:(hh + 1) * V_SLOT, kb * tk:(kb + 1) * tk], p[hh].astype(BF16),
                      preferred_element_type=F32) for hh in heads]
        if m is None:
            acc = pv
        else:
            acc = [jnp.exp2(a - b_) * x + y for a, b_, x, y in zip(m, m_new, acc, pv)]
        m = m_new
    o_ref[...] = jnp.concatenate([(a[:MLA_V] / a[MLA_V:MLA_V + 1]).T for a in acc], axis=-1).astype(o_ref.dtype)


def _attention(qt, k, vt, n_lat, n_ctx, need_ctx):
    b, t, _ = k.shape
    hp = MLA_HEADS // 2
    tq = 1024 if n_lat % 1024 == 0 else 128
    out_shape = jax.ShapeDtypeStruct((b, t, MLA_HEADS * MLA_V), BF16)
    att = pl.pallas_call(
        _attn_kernel,
        grid=(b, hp, n_lat // tq),
        in_specs=[pl.BlockSpec((None, 2 * HEAD_SLOT, tq), lambda i, h, j: (i, h, j)),
                  pl.BlockSpec((None, t, 2 * HEAD_SLOT), lambda i, h, j: (i, 0, h)),
                  pl.BlockSpec((None, 2 * V_SLOT, t), lambda i, h, j: (i, h, 0))],
        out_specs=pl.BlockSpec((None, tq, 2 * MLA_V), lambda i, h, j: (i, j, h)),
        out_shape=out_shape,
        compiler_params=_cparams(("parallel", "parallel", "arbitrary")),
        name="mla_attn_lat",
    )(qt, k, vt)
    if not need_ctx:
        return att
    cb = t // n_ctx - 1
    return pl.pallas_call(
        _attn_kernel,
        grid=(b, hp),
        in_specs=[pl.BlockSpec((None, 2 * HEAD_SLOT, n_ctx), lambda i, h: (i, h, cb)),
                  pl.BlockSpec((None, n_ctx, 2 * HEAD_SLOT), lambda i, h: (i, cb, h)),
                  pl.BlockSpec((None, 2 * V_SLOT, n_ctx), lambda i, h: (i, h, cb)),
                  pl.BlockSpec(memory_space=pl.ANY)],
        out_specs=pl.BlockSpec((None, n_ctx, 2 * MLA_V), lambda i, h: (i, cb, h)),
        out_shape=out_shape,
        input_output_aliases={3: 0},
        compiler_params=_cparams(("parallel", "parallel")),
        name="mla_attn_ctx",
    )(qt, k, vt, att)


def _scan_maps(nlb, ncb):
    fwd = lambda j: jnp.where(j < ncb, nlb + j, j - ncb)
    bwd = lambda j: nlb + ncb - 1 - j
    return fwd, bwd


def _scan_blk(n_lat, n_ctx):
    for blk in (256, 128, 64):
        if n_lat % blk == 0 and n_ctx % blk == 0:
            return blk
    raise ValueError("sequence lengths must be multiples of the 64-token chunk")


def _gla_kernel(qf, kf, vf, lf, qb, kb, vb, lb, of, ob, sf, sb):
    @pl.when(pl.program_id(1) == 0)
    def _():
        sf[...] = jnp.zeros_like(sf)
        sb[...] = jnp.zeros_like(sb)

    hk, hv = GLA_HEADS * GLA_DK, GLA_HEADS * GLA_DV
    i32 = jnp.int32
    io = lambda shape, d: lax.broadcasted_iota(i32, shape, d)
    blk = qf.shape[0]
    nc = blk // CHUNK
    aj = io((CHUNK, GLA_HEADS * CHUNK), 1) % CHUNK
    ai = io((CHUNK, GLA_HEADS * CHUNK), 0)
    a_incl = (ai >= aj, ai <= aj)
    klane = io((CHUNK, hk), 1) // GLA_DK
    vlane = io((CHUNK, hv), 1) // GLA_DV
    sdiag = (io((hv, hk), 0) // GLA_DV) == (io((hv, hk), 1) // GLA_DK)
    bi, bj = io((blk, blk), 0), io((blk, blk), 1)
    same_chunk = (bi // CHUNK) == (bj // CHUNK)

    zs, cs, qs, ks, vs, bcs, bts = ([] for _ in range(7))
    for z, (q, k, v, lg) in enumerate(((qf, kf, vf, lf), (qb, kb, vb, lb))):
        tri = jnp.where(jnp.logical_and(same_chunk, (bi <= bj) if z else (bi >= bj)), 1.0, 0.0).astype(BF16)
        bcum = _dot_exact_lhs(tri, lg[:, z * hk:(z + 1) * hk])
        for c in range(nc):
            r = slice(c * CHUNK, (c + 1) * CHUNK)
            last = c * CHUNK if z else (c + 1) * CHUNK - 1
            zs.append(z), cs.append(c)
            qs.append(q[r, :]), ks.append(k[r, :]), vs.append(v[r, :])
            bcs.append(bcum[r, :]), bts.append(bcum[last:last + 1, :])

    qds = _stage(lambda q, b_: q * (GLA_DK ** -0.5) * jnp.exp(b_), qs, bcs)
    kbs = _stage(lambda k, b_: jnp.concatenate(
        [jnp.where(klane == h, k * jnp.exp(-b_), 0.0) for h in range(GLA_HEADS)], axis=0), ks, bcs)
    a_s = _stage(lambda z, qd, kb_: jnp.where(a_incl[z], _dot_nt(qd, kb_), 0.0), zs, qds, kbs)
    vbs = _stage(lambda v: jnp.concatenate([jnp.where(vlane == h, v, 0.0) for h in range(GLA_HEADS)], axis=0), vs)
    ois = _stage(_dot, a_s, vbs)
    upds = _stage(lambda v, k, b_, bt: jnp.where(sdiag, _dot_tn(v, k * jnp.exp(bt - b_)), 0.0),
                  vs, ks, bcs, bts)
    ges = _stage(jnp.exp, bts)

    chain = {(z, c): i for i, (z, c) in enumerate(zip(zs, cs))}
    state = [sf[...], sb[...]]
    outs = (of, ob)
    for step in range(nc):
        for z in range(2):
            c = nc - 1 - step if z else step
            i = chain[(z, c)]
            outs[z][c * CHUNK:(c + 1) * CHUNK, :] = ois[i] + _dot_nt(qds[i], state[z])
            state[z] = state[z] * ges[i] + upds[i]
    sf[...] = state[0]
    sb[...] = state[1]


def _gla_scan(gq, gk, gv, lg, n_lat, n_ctx):
    b, t, hk = gq.shape
    hv = gv.shape[-1]
    blk = _scan_blk(n_lat, n_ctx)
    nlb, ncb = n_lat // blk, n_ctx // blk
    fwd, bwd = _scan_maps(nlb, ncb)
    spec = lambda n, m: pl.BlockSpec((None, blk, n), lambda i, j: (i, m(j), 0))
    out = jax.ShapeDtypeStruct((b, t, hv), F32)
    return pl.pallas_call(
        _gla_kernel,
        grid=(b, nlb + ncb),
        in_specs=[spec(hk, fwd), spec(hk, fwd), spec(hv, fwd), spec(2 * hk, fwd),
                  spec(hk, bwd), spec(hk, bwd), spec(hv, bwd), spec(2 * hk, bwd)],
        out_specs=[spec(hv, fwd), spec(hv, bwd)],
        out_shape=[out, out],
        scratch_shapes=[pltpu.VMEM((hv, hk), F32), pltpu.VMEM((hv, hk), F32)],
        compiler_params=_cparams(("parallel", "arbitrary")),
        name="gla_scan",
    )(gq, gk, gv, lg, gq, gk, gv, lg)


def _cast_kernel(x_ref, o_ref):
    o_ref[...] = x_ref[...].astype(o_ref.dtype)


def _to_bf16(w):
    depth, r, c = w.shape
    tr = 512 if r % 512 == 0 else r
    spec = pl.BlockSpec((None, tr, c), lambda l, i: (l, i, 0))
    return pl.pallas_call(
        _cast_kernel,
        grid=(depth, r // tr),
        in_specs=[spec],
        out_specs=spec,
        out_shape=jax.ShapeDtypeStruct(w.shape, BF16),
        compiler_params=_cparams(("parallel", "parallel")),
        name="cast_bf16",
    )(w)


def _head_norm_gate(of_ref, ob_ref, g_ref, on_ref, n_heads):
    x = of_ref[...] + ob_ref[...]
    g = g_ref[...]
    outs = []
    for h in range(n_heads):
        s = slice(h * LANES, (h + 1) * LANES)
        outs.append(_rms(x[:, s], on_ref[...]) * _silu(g[:, s]))
    return jnp.concatenate(outs, axis=-1)


def _merge_mlp_kernel(*refs, n_parts, final):
    parts = refs[:n_parts]
    on_ref, wo_ref, h_ref, mod_ref, n2_ref, w1_ref, w2_ref = refs[n_parts:n_parts + 7]
    o_ref = refs[-1]
    mod = mod_ref[...]
    if n_parts == 4:
        att_ref = parts[0]
        na = att_ref.shape[-1]
        y = _head_norm_gate(*parts[1:], on_ref, GLA_HEADS)
        o = jnp.dot(att_ref[...], wo_ref[0:na, :], preferred_element_type=F32) + _dot(y, wo_ref[na:, :])
    else:
        o = _dot(_head_norm_gate(*parts, on_ref, GDN_HEADS), wo_ref[...])
    h = h_ref[...] + mod[2:3] * o
    a = (_rms(h, n2_ref[...]) * (1.0 + mod[4:5]) + mod[3:4]).astype(BF16)
    ff = w1_ref.shape[1]
    hc = min(ff, MLP_HIDDEN_CHUNK)
    acc = None
    for c in range(ff // hc):
        u = jnp.maximum(jnp.dot(a, w1_ref[:, c * hc:(c + 1) * hc], preferred_element_type=F32), 0.0)
        t = _dot(u * u, w2_ref[c * hc:(c + 1) * hc, :])
        acc = t if acc is None else acc + t
    out = h + mod[5:6] * acc
    if final:
        out = _rms(out, refs[n_parts + 7][...])
    o_ref[...] = out


def _merge_mlp(parts, on, wo, h, mod, n2, w1, w2, layer, tm, n_lat, n_tok, final_norm=None):
    b, _, d = h.shape
    nlb = n_lat // tm
    tok = lambda n: pl.BlockSpec((None, tm, n), lambda i, j: (i, j, 0))
    once = lambda a: pl.BlockSpec(a.shape, lambda *_: (0,) * a.ndim, pipeline_mode=pl.Buffered(1))
    slab = lambda a: pl.BlockSpec((None,) + a.shape[1:], lambda *_: (layer, 0, 0), pipeline_mode=pl.Buffered(1))
    args = [*parts, on, wo, h, mod, n2, w1, w2]
    specs = [tok(p.shape[-1]) for p in parts] + [
        _full(on.shape), once(wo), tok(d),
        pl.BlockSpec((None, None, N_MOD, d), lambda i, j: (i, j // nlb, 0, 0)),
        _full(n2.shape), slab(w1), slab(w2)]
    if final_norm is not None:
        args.append(final_norm)
        specs.append(_full(final_norm.shape))
    return pl.pallas_call(
        functools.partial(_merge_mlp_kernel, n_parts=len(parts), final=final_norm is not None),
        grid=(b, n_tok // tm),
        in_specs=specs,
        out_specs=tok(d),
        out_shape=jax.ShapeDtypeStruct((b, n_tok, d), F32),
        compiler_params=_cparams(("parallel", "parallel")),
        name="merge_mlp",
    )(*args)


_O_QKV = 2 * GDN_HEADS * GDN_DK + GDN_HEADS * GDN_DV
_O_QK = 2 * GDN_HEADS * GDN_DK
_O_G = GDN_HEADS * GDN_DV
_O_N = _O_QKV + _O_G + LANES
HALO = 8


def _odd_proj_kernel(h_ref, hp_ref, hn_ref, mod_ref, n1_ref, win_ref, cw_ref, e_ref, et_ref, rs_ref,
                     al_ref, dt_ref, q_out, k_out, v_out, g_out, gt_out, *, nlb, nb):
    j = pl.program_id(1)
    mod = mod_ref[...]
    n1 = n1_ref[...]
    pre = lambda x: (_rms(x, n1) * (1.0 + mod[1:2]) + mod[0:1]).astype(BF16)
    a = pre(h_ref[...])
    first = jnp.logical_or(j == 0, j == nlb)
    last = jnp.logical_or(j == nlb - 1, j == nb - 1)
    zero_halo = jnp.zeros((HALO, a.shape[1]), BF16)
    a_ext = jnp.concatenate([a, jnp.where(last, zero_halo, pre(hn_ref[...])),
                             jnp.where(first, zero_halo, pre(hp_ref[...]))], axis=0)
    tm = a.shape[0]
    te = tm + 2 * HALO
    hq = GDN_HEADS * GDN_DK
    gw = hq
    mm = lambda lhs, c0, c1: jnp.dot(lhs, win_ref[:, c0:c1], preferred_element_type=F32)
    project = lambda g: mm(a_ext, g * gw, (g + 1) * gw)

    pending = project(0)
    for g in range(_O_QKV // gw):
        xe = pending
        if (g + 1) * gw < _O_QKV:
            pending = project(g + 1)
        else:
            y_tail = mm(a, _O_QKV, _O_N)
        cw = cw_ref[:, g * gw:(g + 1) * gw]
        s = _silu(cw[0:1] * pltpu.roll(xe, 1, axis=0)[:tm] + cw[1:2] * xe[:tm]
                  + cw[2:3] * pltpu.roll(xe, te - 1, axis=0)[:tm])
        if g < 2:
            ss = _dot(s * s, e_ref[g * gw:(g + 1) * gw, :])
            r = lax.rsqrt(ss + EPS) * rs_ref[...]
            et = et_ref[:, g * gw:(g + 1) * gw]
            rf = sum(jnp.dot(p, et, preferred_element_type=F32) for p in _split(r, 2))
            (q_out, k_out)[g][...] = s * rf
        else:
            v_out[:, (g - 2) * gw:(g - 1) * gw] = s
    g_out[...] = y_tail[:, 0:_O_G]
    t = y_tail[:, _O_G:]
    log_a = -jnp.exp(al_ref[...]) * _softplus(t + dt_ref[...])
    lane = lax.broadcasted_iota(jnp.int32, t.shape, 1)
    gt_out[...] = jnp.where(lane < 2 * GDN_HEADS, log_a, _sigmoid(t))


def _odd_project(h, mod, n1, w_in, conv_w, a_log, dt_bias, tm, n_lat):
    b, t, d = h.shape
    nlb, nb = n_lat // tm, t // tm
    win = jnp.concatenate([w_in, jnp.zeros((d, _O_N - w_in.shape[1]), F32)], axis=1).astype(BF16)
    nqk = 2 * GDN_HEADS
    e = (jnp.arange(_O_QK)[:, None] // GDN_DK == jnp.arange(LANES)[None, :]).astype(BF16)
    et = e.T
    rs = jnp.concatenate([jnp.full((GDN_HEADS,), GDN_DK ** -0.5, F32), jnp.ones((LANES - GDN_HEADS,), F32)])[None, :]
    padl = lambda v: jnp.concatenate([v.reshape(-1), jnp.zeros((LANES - v.size,), F32)])[None, :]
    al, dt = padl(a_log), padl(dt_bias)
    tok = lambda n: pl.BlockSpec((None, tm, n), lambda i, j: (i, j, 0))
    hb = tm // HALO
    outs = [GDN_HEADS * GDN_DK, GDN_HEADS * GDN_DK, GDN_HEADS * GDN_DV, _O_G, LANES]
    return pl.pallas_call(
        functools.partial(_odd_proj_kernel, nlb=nlb, nb=nb),
        grid=(b, nb),
        in_specs=[tok(d),
                  pl.BlockSpec((None, HALO, d), lambda i, j: (i, jnp.maximum(j * hb - 1, 0), 0)),
                  pl.BlockSpec((None, HALO, d), lambda i, j: (i, jnp.minimum((j + 1) * hb, t // HALO - 1), 0)),
                  pl.BlockSpec((None, None, N_MOD, d), lambda i, j: (i, j // nlb, 0, 0)),
                  _full(n1.shape), _full(win.shape), _full(conv_w.shape), _full(e.shape), _full(et.shape),
                  _full(rs.shape), _full(al.shape), _full(dt.shape)],
        out_specs=[tok(n) for n in outs],
        out_shape=[jax.ShapeDtypeStruct((b, t, n), F32) for n in outs],
        compiler_params=_cparams(("parallel", "parallel")),
        name="odd_proj",
    )(h, h, h, mod, n1, win, conv_w, e, et, rs, al, dt)


def _gdn_kernel(qf, kf, vf, gcf, grf, qb, kb, vb, gcb, grb, of, ob, s_ref):
    @pl.when(pl.program_id(2) == 0)
    def _():
        s_ref[...] = jnp.zeros_like(s_ref)

    blk = qf.shape[0]
    nc = blk // CHUNK
    nh = qf.shape[1] // GDN_DK
    i32 = jnp.int32
    ri = lax.broadcasted_iota(i32, (CHUNK, CHUNK), 0)
    ci = lax.broadcasted_iota(i32, (CHUNK, CHUNK), 1)
    same = [(ri // w) == (ci // w) for w in (2, 4, 8, 16, 32)]
    lvl = [same[0]] + [jnp.logical_and(same[i], jnp.logical_not(same[i - 1])) for i in range(1, 5)]
    lvl.append(jnp.logical_not(same[4]))
    eye = jnp.where(ri == ci, 1.0, 0.0).astype(F32)
    incl = (ri >= ci, ri <= ci)
    strict = (ri > ci, ri < ci)
    bi = lax.broadcasted_iota(i32, (blk, blk), 0)
    bj = lax.broadcasted_iota(i32, (blk, blk), 1)
    same_chunk = (bi // CHUNK) == (bj // CHUNK)

    zs, cs, hs, qs, ks, vs, gbs, bbs, grows, brows, gls = ([] for _ in range(11))
    sq = (CHUNK, CHUNK)
    for z, (q, k, v, gc, gr) in enumerate(((qf, kf, vf, gcf, grf), (qb, kb, vb, gcb, grb))):
        tri = jnp.where(jnp.logical_and(same_chunk, (bi <= bj) if z else (bi >= bj)), 1.0, 0.0).astype(BF16)
        gcv, grv = gc[...], gr[...]
        cum_col = _dot_exact_lhs(tri, gcv)
        cum_row = _dot_exact_lhs(tri, grv, nt=True)
        for c in range(nc):
            r = slice(c * CHUNK, (c + 1) * CHUNK)
            last = c * CHUNK if z else (c + 1) * CHUNK - 1
            for hh in range(nh):
                idx = nh * z + hh
                zs.append(z), cs.append(c), hs.append(hh)
                qs.append(q[r, hh * GDN_DK:(hh + 1) * GDN_DK])
                ks.append(k[r, hh * GDN_DK:(hh + 1) * GDN_DK])
                vs.append(v[r, hh * GDN_DV:(hh + 1) * GDN_DV])
                gbs.append(jnp.broadcast_to(cum_col[r, idx:idx + 1], sq))
                bbs.append(jnp.broadcast_to(gcv[r, 2 * nh + idx:2 * nh + idx + 1], sq))
                grows.append(cum_row[idx:idx + 1, r])
                brows.append(grv[2 * nh + idx:2 * nh + idx + 1, r])
                gls.append(cum_col[last:last + 1, idx:idx + 1])

    decays = _stage(lambda z, gb, gr_: jnp.where(incl[z], jnp.exp(jnp.where(incl[z], gb - gr_, 0.0)), 0.0),
                    zs, gbs, grows)
    qkk = _stage(lambda q, k: _dot_nt(jnp.concatenate([q, k], axis=0), k), qs, ks)
    ns = _stage(lambda z, x, bb, d_: jnp.where(strict[z], x[CHUNK:] * bb * d_, 0.0), zs, qkk, bbs, decays)
    ts = _stage(lambda n: eye - jnp.where(lvl[0], n, 0.0), ns)
    for off in lvl[1:]:
        xs = _stage(lambda t, n: _dot(t, jnp.where(off, n, 0.0)), ts, ns)
        ts = _stage(lambda t, x: t - _dot(x, t), ts, xs)
    egs = _stage(jnp.exp, gbs)
    uws = _stage(lambda t, br, v, k, e_: _dot(t * br, jnp.concatenate([v, k * e_], axis=-1)),
                 ts, brows, vs, ks, egs)
    qks = _stage(lambda x, d_: x[:CHUNK] * d_, qkk, decays)
    kes = _stage(lambda k, gl, gb: k * jnp.exp(gl - gb), ks, gls, gbs)
    kuws = _stage(_dot_tn, kes, uws)
    quws = _stage(_dot, qks, uws)
    lhs = _stage(lambda q, e_, qu, ku: jnp.concatenate([q * e_ - qu[:, GDN_DV:], -ku[:, GDN_DV:]], axis=0),
                 qs, egs, quws, kuws)
    ges = _stage(jnp.exp, gls)

    chain = {(z, c, hh): i for i, (z, c, hh) in enumerate(zip(zs, cs, hs))}
    state = {(z, hh): s_ref[z, hh] for z in range(2) for hh in range(nh)}
    outs = (of, ob)
    for step in range(nc):
        for z in range(2):
            c = nc - 1 - step if z else step
            for hh in range(nh):
                i = chain[(z, c, hh)]
                s = state[(z, hh)]
                r = _dot(lhs[i], s)
                outs[z][c * CHUNK:(c + 1) * CHUNK, hh * GDN_DV:(hh + 1) * GDN_DV] = (
                    r[:CHUNK] + quws[i][:, :GDN_DV])
                state[(z, hh)] = s * ges[i] + (kuws[i][:, :GDN_DV] + r[CHUNK:])
    for (z, hh), s in state.items():
        s_ref[z, hh] = s


def _gdn_scan(q, k, v, gates, n_lat, n_ctx):
    b, t, _ = q.shape
    nh = GDN_SCAN_HEADS
    ng = GDN_HEADS // nh
    g = gates[..., :4 * GDN_HEADS].reshape(b, t, 2, 2, ng, nh)
    gcol = g.transpose(0, 4, 1, 2, 3, 5).reshape(b, ng, t, 4 * nh)
    grow = gcol.transpose(0, 1, 3, 2)
    blk = _scan_blk(n_lat, n_ctx)
    nlb, ncb = n_lat // blk, n_ctx // blk
    fwd, bwd = _scan_maps(nlb, ncb)
    d = None
    tok = lambda n, m: pl.BlockSpec((d, blk, n), lambda i, h, j: (i, m(j), h))
    gcs = lambda m: pl.BlockSpec((d, d, blk, 4 * nh), lambda i, h, j: (i, h, m(j), 0))
    grs = lambda m: pl.BlockSpec((d, d, 4 * nh, blk), lambda i, h, j: (i, h, 0, m(j)))
    out = jax.ShapeDtypeStruct((b, t, GDN_HEADS * GDN_DV), F32)
    dk2, dv2 = nh * GDN_DK, nh * GDN_DV
    return pl.pallas_call(
        _gdn_kernel,
        grid=(b, ng, nlb + ncb),
        in_specs=[tok(dk2, fwd), tok(dk2, fwd), tok(dv2, fwd), gcs(fwd), grs(fwd),
                  tok(dk2, bwd), tok(dk2, bwd), tok(dv2, bwd), gcs(bwd), grs(bwd)],
        out_specs=[tok(dv2, fwd), tok(dv2, bwd)],
        out_shape=[out, out],
        scratch_shapes=[pltpu.VMEM((2, nh, GDN_DK, GDN_DV), F32)],
        compiler_params=_cparams(("parallel", "parallel", "arbitrary")),
        name="gdn_scan",
    )(q, k, v, gcol, grow, q, k, v, gcol, grow)


def kernel(x, c, ctx, c_ctx, ada_w, ada_b, norm1_w, norm2_w, mlp_w1, mlp_w2, even_w_in, mla_q_norm,
           mla_w_uq, mla_kv_norm, mla_w_ukv, gla_gate_up, gla_gate_bias, gla_o_norm, even_w_out,
           gdn_w_in, gdn_conv_w, gdn_a_log, gdn_dt_bias, gdn_o_norm, gdn_w_out, final_norm):
    b, n_lat, d = x.shape
    n_ctx = ctx.shape[1]
    depth = ada_w.shape[0]
    tm = 256 if (n_lat % 256 == 0 and n_ctx % 256 == 0) else 128
    assert n_lat % tm == 0 and n_ctx % tm == 0 and n_lat % n_ctx == 0 and n_lat % GRID_W == 0

    rows = -(-(b + 1) // 8) * 8
    cc = jnp.concatenate([c, c_ctx[None, :], jnp.zeros((rows - b - 1, d), F32)], axis=0)
    mods = _ada_all(cc, ada_w, ada_b).reshape(depth, rows, N_MOD, d)
    mod_tok = jnp.stack([mods[:, :b], jnp.broadcast_to(mods[:, b:b + 1], (depth, b, N_MOD, d))], axis=2)

    h = jnp.concatenate([x, ctx], axis=1)
    tabs = _rope_tables(n_lat, n_ctx)
    w1_bf, w2_bf = _to_bf16(mlp_w1), _to_bf16(mlp_w2)
    row = lambda v: v[None, :]
    for layer in range(depth):
        need_ctx = layer < depth - 1
        n_tok = n_lat + n_ctx if need_ctx else n_lat
        mod = mod_tok[layer]
        n1 = row(norm1_w[layer])
        i = layer // 2
        if layer % 2 == 0:
            ew = _even_weights(even_w_in[i], mla_w_uq[i], mla_w_ukv[i], gla_gate_up[i], gla_gate_bias[i])
            q, k, v, gq, gk, gv, gg, lg = _even_project(
                h, mod, n1, ew, row(mla_q_norm[i]), row(mla_kv_norm[i]), tabs, tm, n_lat)
            att = _attention(q, k, v, n_lat, n_ctx, need_ctx)
            o_f, o_b = _gla_scan(gq, gk, gv, lg, n_lat, n_ctx)
            parts, on, wo = [att, o_f, o_b, gg], gla_o_norm[i], even_w_out[i]
        else:
            q, k, v, g, gates = _odd_project(h, mod, n1, gdn_w_in[i], gdn_conv_w[i], gdn_a_log[i],
                                             gdn_dt_bias[i], tm, n_lat)
            o_f, o_b = _gdn_scan(q, k, v, gates, n_lat, n_ctx)
            parts, on, wo = [o_f, o_b, g], gdn_o_norm[i], gdn_w_out[i]
        h = _merge_mlp(parts, row(on), wo.astype(BF16), h, mod, row(norm2_w[layer]), w1_bf, w2_bf, layer,
                       tm, n_lat, n_tok, final_norm=None if need_ctx else row(final_norm))
    return h
```

```python
import functools

import jax
import jax.numpy as jnp
from jax import lax
from jax.experimental import pallas as pl
from jax.experimental.pallas import tpu as pltpu

F32 = jnp.float32
BF16 = jnp.bfloat16
EPS = 1e-6
LOG2E = 1.4426950408889634

GRID_W = 64
N_MOD = 6
MLA_HEADS = 8
MLA_NOPE = 64
MLA_ROPE = 32
MLA_V = 64
MLA_QK = MLA_NOPE + MLA_ROPE
MLA_Q_RANK = 384
MLA_KV_RANK = 256
ROPE_THETA = 10000.0
GLA_HEADS = 4
GLA_DK = 64
GLA_DV = 128
GLA_GATE_RANK = 16
GLA_GATE_NORM = 16.0
GDN_HEADS = 8
GDN_DK = 64
GDN_DV = 128
CHUNK = 64
MLP_HIDDEN_CHUNK = 1024
GDN_SCAN_HEADS = 4
LANES = 128
HEAD_SLOT = 128
V_SLOT = 80
VMEM_LIMIT = 56 * 1024 * 1024


def _cparams(sem):
    return pltpu.CompilerParams(dimension_semantics=sem, vmem_limit_bytes=VMEM_LIMIT)


def _dot(a, b):
    return jnp.dot(a.astype(BF16), b.astype(BF16), preferred_element_type=F32)


def _dot_nt(a, b):
    return lax.dot_general(a.astype(BF16), b.astype(BF16), (((1,), (1,)), ((), ())),
                           preferred_element_type=F32)


def _dot_tn(a, b):
    return lax.dot_general(a.astype(BF16), b.astype(BF16), (((0,), (0,)), ((), ())),
                           preferred_element_type=F32)


def _split(x, n):
    out = []
    for _ in range(n - 1):
        p = x.astype(BF16)
        out.append(p)
        x = x - p.astype(F32)
    out.append(x.astype(BF16))
    return out


def _dot3(a, b):
    a1, a2 = _split(a, 2)
    b1, b2 = _split(b, 2)
    d = functools.partial(jnp.dot, preferred_element_type=F32)
    return d(a1, b1) + (d(a1, b2) + d(a2, b1))


def _dot_exact_lhs(m_bf, x, nt=False):
    acc = None
    for p in _split(x, 3):
        if nt:
            t = lax.dot_general(p, m_bf, (((1,), (1,)), ((), ())), preferred_element_type=F32)
        else:
            t = jnp.dot(m_bf, p, preferred_element_type=F32)
        acc = t if acc is None else acc + t
    return acc


def _rms(x, w):
    return x * lax.rsqrt(jnp.mean(x * x, axis=-1, keepdims=True) + EPS) * w


def _sigmoid(x):
    return 1.0 / (1.0 + jnp.exp(-x))


def _silu(x):
    return x * _sigmoid(x)


def _softplus(x):
    return jnp.maximum(x, 0.0) + jnp.log1p(jnp.exp(-jnp.abs(x)))


def _tile_lanes(x, n):
    return jnp.concatenate([x] * n, axis=-1)


def _stream_specs(stream, tm, nlb):
    d = stream[0].shape[-1]
    if len(stream) == 1:
        return [pl.BlockSpec((None, tm, d), lambda i, j: (i, j, 0))]
    return [pl.BlockSpec((None, tm, d), lambda i, j: (i, jnp.minimum(j, nlb - 1), 0)),
            pl.BlockSpec((None, tm, d), lambda i, j: (i, jnp.maximum(j - nlb, 0), 0))]


def _stream_block(refs, nlb):
    if len(refs) == 1:
        return refs[0][...]
    return jnp.where(pl.program_id(1) >= nlb, refs[1][...], refs[0][...])


def _stage(fn, *cols):
    return [fn(*a) for a in zip(*cols)]


def _ada_kernel(cc_ref, w_ref, b_ref, o_ref):
    o_ref[0] = _dot(_silu(cc_ref[...]), w_ref[0]) + b_ref[0]


def _ada_all(cc, ada_w, ada_b):
    depth, d, n = ada_w.shape
    rows = cc.shape[0]
    tn = 1024
    return pl.pallas_call(
        _ada_kernel,
        grid=(depth, n // tn),
        in_specs=[pl.BlockSpec((rows, d), lambda l, j: (0, 0)),
                  pl.BlockSpec((1, d, tn), lambda l, j: (l, 0, j)),
                  pl.BlockSpec((1, 1, tn), lambda l, j: (l, 0, j))],
        out_specs=pl.BlockSpec((1, rows, tn), lambda l, j: (l, 0, j)),
        out_shape=jax.ShapeDtypeStruct((depth, rows, n), F32),
        compiler_params=_cparams(("parallel", "parallel")),
        name="ada_mod",
    )(cc, ada_w, ada_b.reshape(depth, 1, n))


_E_CQ = (0, 384)
_E_CKV = (384, 640)
_E_ROT = (640, 768)
_E_GQ = (768, 1024)
_E_GK = (1024, 1280)
_E_GV = (1280, 1792)
_E_GG = (1792, 2304)
_E_N = 2304


def _even_proj_kernel(*refs, n_stream, nlb):
    (mod_ref, n1_ref, win_ref, qn_ref, wq_ref, kvn_ref, wk_ref, wv_ref, wg_ref, gb_ref, c_ref, s_ref,
     ctt_ref, stt_ref, q_out, k_out, v_out, gq_out, gk_out, gv_out, gg_out, lg_out) = refs[n_stream:]
    mod = mod_ref[...]
    a = _rms(_stream_block(refs[:n_stream], nlb), n1_ref[...]) * (1.0 + mod[1:2]) + mod[0:1]
    y = _dot(a, win_ref[...])
    sl = lambda r: y[:, r[0]:r[1]]
    ct, st = c_ref[...], s_ref[...]
    nh = MLA_HEADS
    qq = _dot_nt(wq_ref[...], _rms(sl(_E_CQ), qn_ref[...]))
    c32, s32 = ctt_ref[...], stt_ref[...]
    rot0 = nh * MLA_QK
    zero_pad = jnp.zeros((HEAD_SLOT - MLA_QK, qq.shape[1]), F32)
    pieces = []
    for h in range(nh):
        nope = qq[h * MLA_QK:h * MLA_QK + MLA_NOPE]
        pe = qq[h * MLA_QK + MLA_NOPE:(h + 1) * MLA_QK]
        pe_rot = qq[rot0 + h * MLA_ROPE:rot0 + (h + 1) * MLA_ROPE]
        pieces += [nope, pe * c32 + pe_rot * s32, zero_pad]
    q_out[...] = jnp.concatenate(pieces, axis=0).astype(q_out.dtype)
    ckvn = _rms(sl(_E_CKV), kvn_ref[...])
    yr = sl(_E_ROT)
    kpe = pltpu.roll(yr, MLA_NOPE, axis=1) * ct + pltpu.roll(yr, MLA_NOPE - MLA_ROPE, axis=1) * st
    k_out[...] = (_dot(ckvn, wk_ref[...]) + _tile_lanes(kpe, nh)).astype(k_out.dtype)
    vt = _dot_nt(wv_ref[...], ckvn)
    pad_rows = lax.broadcasted_iota(jnp.int32, (V_SLOT - MLA_V, vt.shape[1]), 0)
    ones_row = jnp.where(pad_rows == 0, 1.0, 0.0)
    v_out[...] = jnp.concatenate(
        [x for h in range(nh) for x in (vt[h * MLA_V:(h + 1) * MLA_V], ones_row)], axis=0).astype(v_out.dtype)
    gq_out[...] = sl(_E_GQ)
    gk_out[...] = sl(_E_GK)
    gv_out[...] = sl(_E_GV)
    gg_out[...] = sl(_E_GG)
    logit = _dot(yr, wg_ref[...]) + gb_ref[...]
    lg_out[...] = (jnp.minimum(logit, 0.0) - jnp.log1p(jnp.exp(-jnp.abs(logit)))) * (1.0 / GLA_GATE_NORM)


def _rot_half_cols(w):
    f = MLA_ROPE // 4
    return jnp.concatenate([-w[..., f:2 * f], w[..., 0:f], -w[..., 3 * f:4 * f], w[..., 2 * f:3 * f]], axis=-1)


def _even_weights(w_in, w_uq, w_ukv, gate_up, gate_bias):
    d = w_in.shape[0]
    cq, ckv, kpe, gq, gk, gv, gg, glow = jnp.split(
        w_in, [384, 640, 672, 928, 1184, 1696, 2208], axis=1)
    lowr = 2 * GLA_GATE_RANK
    win = jnp.concatenate([cq, ckv, kpe, _rot_half_cols(kpe), glow, jnp.zeros((d, LANES - 2 * MLA_ROPE - lowr), F32),
                           gq, gk, gv, gg], axis=1).astype(BF16)
    r = w_uq.shape[0]
    wq = (w_uq * (MLA_QK ** -0.5 * LOG2E)).reshape(r, MLA_HEADS, MLA_QK)
    wq_rot = _rot_half_cols(wq[..., MLA_NOPE:])
    wq_all = jnp.concatenate([wq.reshape(r, -1), wq_rot.reshape(r, -1)], axis=1).T.astype(BF16)
    rk = w_ukv.shape[0]
    wkv = w_ukv.reshape(rk, MLA_HEADS, MLA_NOPE + MLA_V)
    wk = jnp.concatenate([wkv[..., :MLA_NOPE], jnp.zeros((rk, MLA_HEADS, HEAD_SLOT - MLA_NOPE), F32)],
                         axis=-1).reshape(rk, -1).astype(BF16)
    wv = wkv[..., MLA_NOPE:].reshape(rk, -1).T.astype(BF16)
    hk = GLA_HEADS * GLA_DK
    g0 = 2 * MLA_ROPE
    wg = jnp.zeros((LANES, 2 * hk), F32)
    wg = wg.at[g0:g0 + GLA_GATE_RANK, 0:hk].set(gate_up[0])
    wg = wg.at[g0 + GLA_GATE_RANK:g0 + 2 * GLA_GATE_RANK, hk:].set(gate_up[1])
    gb = jnp.concatenate([gate_bias[0], gate_bias[1]])[None, :]
    return win, wq_all, wk, wv, wg.astype(BF16), gb


def _rope_tables(n_lat, n_ctx):
    rows = n_lat // GRID_W
    row = jnp.repeat(jnp.arange(rows, dtype=F32), GRID_W)
    col = jnp.tile(jnp.arange(GRID_W, dtype=F32), rows)
    axis_dim = MLA_ROPE // 2
    inv_freq = ROPE_THETA ** (-jnp.arange(0, axis_dim, 2, dtype=F32) / axis_dim)
    ar, ac = row[:, None] * inv_freq, col[:, None] * inv_freq
    c32 = jnp.concatenate([jnp.cos(ar), jnp.cos(ar), jnp.cos(ac), jnp.cos(ac)], axis=1)
    s32 = jnp.concatenate([jnp.sin(ar), jnp.sin(ar), jnp.sin(ac), jnp.sin(ac)], axis=1)
    c32 = jnp.concatenate([c32, jnp.ones((n_ctx, MLA_ROPE), F32)], axis=0)
    s32 = jnp.concatenate([s32, jnp.zeros((n_ctx, MLA_ROPE), F32)], axis=0)
    slot = lambda x: jnp.pad(x, ((0, 0), (MLA_NOPE, HEAD_SLOT - MLA_QK)))
    return slot(c32), slot(s32), c32.T, s32.T


def _full(shape):
    nd = len(shape)
    return pl.BlockSpec(shape, lambda *_: (0,) * nd)


def _even_project(stream, mod, n1, ew, qn, kvn, tabs, tm, n_lat):
    b, d = stream[0].shape[0], stream[0].shape[-1]
    t = sum(s.shape[1] for s in stream)
    win, wq, wk, wv, wg, gb = ew
    ct, st, ctt, stt = tabs
    nlb = n_lat // tm
    tok = lambda n: pl.BlockSpec((None, tm, n), lambda i, j: (i, j, 0))
    tab = pl.BlockSpec((tm, HEAD_SLOT), lambda i, j: (j, 0))
    tabt = pl.BlockSpec((MLA_ROPE, tm), lambda i, j: (0, j))
    hk, hv = GLA_HEADS * GLA_DK, GLA_HEADS * GLA_DV
    feat = lambda n: pl.BlockSpec((None, n, tm), lambda i, j: (i, 0, j))
    tshape = lambda n, dt: jax.ShapeDtypeStruct((b, t, n), dt)
    fshape = lambda n, dt: jax.ShapeDtypeStruct((b, n, t), dt)
    nq, nv = MLA_HEADS * HEAD_SLOT, MLA_HEADS * V_SLOT
    return pl.pallas_call(
        functools.partial(_even_proj_kernel, n_stream=len(stream), nlb=nlb),
        grid=(b, t // tm),
        in_specs=_stream_specs(stream, tm, nlb) + [
                  pl.BlockSpec((None, None, N_MOD, d), lambda i, j: (i, j // nlb, 0, 0)),
                  _full(n1.shape), _full(win.shape), _full(qn.shape), _full(wq.shape), _full(kvn.shape),
                  _full(wk.shape), _full(wv.shape), _full(wg.shape), _full(gb.shape), tab, tab, tabt, tabt],
        out_specs=[feat(nq), tok(nq), feat(nv), tok(hk), tok(hk), tok(hv), tok(hv), tok(2 * hk)],
        out_shape=[fshape(nq, BF16), tshape(nq, BF16), fshape(nv, BF16), tshape(hk, F32), tshape(hk, F32),
                   tshape(hv, F32), tshape(hv, F32), tshape(2 * hk, F32)],
        compiler_params=_cparams(("parallel", "parallel")),
        name="even_proj",
    )(*stream, mod, n1, win, qn, wq, kvn, wk, wv, wg, gb, ct, st, ctt, stt)


def _attn_kernel(qt_ref, k_ref, vt_ref, o_ref):
    tk_all = k_ref.shape[0]
    tk = 256 if tk_all % 256 == 0 else 128
    heads = range(2)
    nkb = tk_all // tk
    qts = [qt_ref[hh * HEAD_SLOT:(hh + 1) * HEAD_SLOT, :] for hh in heads]

    def scores(kb):
        return [jnp.dot(k_ref[kb * tk:(kb + 1) * tk, hh * HEAD_SLOT:(hh + 1) * HEAD_SLOT], qts[hh],
                        preferred_element_type=F32) for hh in heads]

    s_next = scores(0)
    m = acc = None
    for kb in range(nkb):
        s = s_next
        if kb + 1 < nkb:
            s_next = scores(kb + 1)
        bm = [jnp.max(x, axis=0, keepdims=True) for x in s]
        if m is None:
            m_new = bm
        else:
            m_new = [jnp.maximum(a, b_) for a, b_ in zip(m, bm)]
        p = [jnp.exp2(x - mn) for x, mn in zip(s, m_new)]
        pv = [jnp.dot(vt_ref[hh * V_SLOT:(hh + 1) * V_SLOT, kb * tk:(kb + 1) * tk], p[hh].astype(BF16),
                      preferred_element_type=F32) for hh in heads]
        if m is None:
            acc = pv
        else:
            acc = [jnp.exp2(a - b_) * x + y for a, b_, x, y in zip(m, m_new, acc, pv)]
        m = m_new
    o_ref[...] = jnp.concatenate([(a[:MLA_V] / a[MLA_V:MLA_V + 1]).T for a in acc], axis=-1).astype(o_ref.dtype)


def _attention(qt, k, vt, n_lat, n_ctx, need_ctx):
    b, t, _ = k.shape
    hp = MLA_HEADS // 2
    tq = 1024 if n_lat % 1024 == 0 else 128
    nv = MLA_HEADS * MLA_V
    att = pl.pallas_call(
        _attn_kernel,
        grid=(b, hp, n_lat // tq),
        in_specs=[pl.BlockSpec((None, 2 * HEAD_SLOT, tq), lambda i, h, j: (i, h, j)),
                  pl.BlockSpec((None, t, 2 * HEAD_SLOT), lambda i, h, j: (i, 0, h)),
                  pl.BlockSpec((None, 2 * V_SLOT, t), lambda i, h, j: (i, h, 0))],
        out_specs=pl.BlockSpec((None, tq, 2 * MLA_V), lambda i, h, j: (i, j, h)),
        out_shape=jax.ShapeDtypeStruct((b, n_lat, nv), BF16),
        compiler_params=_cparams(("parallel", "parallel", "arbitrary")),
        name="mla_attn_lat",
    )(qt, k, vt)
    if not need_ctx:
        return att, None
    cb = t // n_ctx - 1
    att_ctx = pl.pallas_call(
        _attn_kernel,
        grid=(b, hp),
        in_specs=[pl.BlockSpec((None, 2 * HEAD_SLOT, n_ctx), lambda i, h: (i, h, cb)),
                  pl.BlockSpec((None, n_ctx, 2 * HEAD_SLOT), lambda i, h: (i, cb, h)),
                  pl.BlockSpec((None, 2 * V_SLOT, n_ctx), lambda i, h: (i, h, cb))],
        out_specs=pl.BlockSpec((None, n_ctx, 2 * MLA_V), lambda i, h: (i, 0, h)),
        out_shape=jax.ShapeDtypeStruct((b, n_ctx, nv), BF16),
        compiler_params=_cparams(("parallel", "parallel")),
        name="mla_attn_ctx",
    )(qt, k, vt)
    return att, att_ctx


def _scan_maps(nlb, ncb):
    fwd = lambda j: jnp.where(j < ncb, nlb + j, j - ncb)
    bwd = lambda j: nlb + ncb - 1 - j
    return fwd, bwd


def _scan_blk(n_lat, n_ctx):
    for blk in (256, 128, 64):
        if n_lat % blk == 0 and n_ctx % blk == 0:
            return blk
    raise ValueError("sequence lengths must be multiples of the 64-token chunk")


def _gla_kernel(qf, kf, vf, lf, qb, kb, vb, lb, of, ob, sf, sb):
    @pl.when(pl.program_id(1) == 0)
    def _():
        sf[...] = jnp.zeros_like(sf)
        sb[...] = jnp.zeros_like(sb)

    hk, hv = GLA_HEADS * GLA_DK, GLA_HEADS * GLA_DV
    i32 = jnp.int32
    io = lambda shape, d: lax.broadcasted_iota(i32, shape, d)
    blk = qf.shape[0]
    nc = blk // CHUNK
    aj = io((CHUNK, GLA_HEADS * CHUNK), 1) % CHUNK
    ai = io((CHUNK, GLA_HEADS * CHUNK), 0)
    a_incl = (ai >= aj, ai <= aj)
    klane = io((CHUNK, hk), 1) // GLA_DK
    bi, bj = io((blk, blk), 0), io((blk, blk), 1)
    same_chunk = (bi // CHUNK) == (bj // CHUNK)

    zs, cs, qs, ks, vs, bcs, bts = ([] for _ in range(7))
    for z, (q, k, v, lg) in enumerate(((qf, kf, vf, lf), (qb, kb, vb, lb))):
        tri = jnp.where(jnp.logical_and(same_chunk, (bi <= bj) if z else (bi >= bj)), 1.0, 0.0).astype(BF16)
        bcum = _dot_exact_lhs(tri, lg[:, z * hk:(z + 1) * hk])
        for c in range(nc):
            r = slice(c * CHUNK, (c + 1) * CHUNK)
            last = c * CHUNK if z else (c + 1) * CHUNK - 1
            zs.append(z), cs.append(c)
            qs.append(q[r, :]), ks.append(k[r, :]), vs.append(v[r, :])
            bcs.append(bcum[r, :]), bts.append(bcum[last:last + 1, :])

    heads = range(GLA_HEADS)
    hd = lambda x, h, w: x[:, h * w:(h + 1) * w]
    zero_k = jnp.zeros((CHUNK, hk), BF16)
    zero_v = jnp.zeros((CHUNK, GLA_DV), BF16)
    qds = _stage(lambda q, b_: q * (GLA_DK ** -0.5) * jnp.exp(b_), qs, bcs)
    kinv = _stage(lambda k, b_: (k * jnp.exp(-b_)).astype(BF16), ks, bcs)
    kbs = _stage(lambda x: jnp.concatenate([jnp.where(klane == h, x, zero_k) for h in heads], axis=0), kinv)
    a_s = _stage(lambda z, qd, kb_: jnp.where(a_incl[z], _dot_nt(qd, kb_), 0.0), zs, qds, kbs)
    vbs = _stage(lambda v: jnp.concatenate(
        [jnp.concatenate([hd(v, j, GLA_DV).astype(BF16) if j == h else zero_v for j in heads], axis=1)
         for h in heads], axis=0), vs)
    ois = _stage(_dot, a_s, vbs)
    kes = _stage(lambda k, b_, bt: k * jnp.exp(bt - b_), ks, bcs, bts)
    upds = _stage(lambda v, ke: [_dot_tn(hd(v, h, GLA_DV), hd(ke, h, GLA_DK)) for h in heads], vs, kes)
    ges = _stage(jnp.exp, bts)

    chain = {(z, c): i for i, (z, c) in enumerate(zip(zs, cs))}
    state = [[s_[h] for h in heads] for s_ in (sf, sb)]
    outs = (of, ob)
    for step in range(nc):
        for z in range(2):
            c = nc - 1 - step if z else step
            i = chain[(z, c)]
            inter = [_dot_nt(hd(qds[i], h, GLA_DK), state[z][h]) for h in heads]
            outs[z][c * CHUNK:(c + 1) * CHUNK, :] = ois[i] + jnp.concatenate(inter, axis=1)
            state[z] = [state[z][h] * hd(ges[i], h, GLA_DK) + upds[i][h] for h in heads]
    for s_, st in zip((sf, sb), state):
        for h in heads:
            s_[h] = st[h]


def _gla_scan(gq, gk, gv, lg, n_lat, n_ctx):
    b, t, hk = gq.shape
    hv = gv.shape[-1]
    blk = _scan_blk(n_lat, n_ctx)
    nlb, ncb = n_lat // blk, n_ctx // blk
    fwd, bwd = _scan_maps(nlb, ncb)
    spec = lambda n, m: pl.BlockSpec((None, blk, n), lambda i, j: (i, m(j), 0))
    out = jax.ShapeDtypeStruct((b, t, hv), F32)
    return pl.pallas_call(
        _gla_kernel,
        grid=(b, nlb + ncb),
        in_specs=[spec(hk, fwd), spec(hk, fwd), spec(hv, fwd), spec(2 * hk, fwd),
                  spec(hk, bwd), spec(hk, bwd), spec(hv, bwd), spec(2 * hk, bwd)],
        out_specs=[spec(hv, fwd), spec(hv, bwd)],
        out_shape=[out, out],
        scratch_shapes=[pltpu.VMEM((GLA_HEADS, GLA_DV, GLA_DK), F32)] * 2,
        compiler_params=_cparams(("parallel", "arbitrary")),
        name="gla_scan",
    )(gq, gk, gv, lg, gq, gk, gv, lg)


def _cast_kernel(x_ref, o_ref):
    o_ref[...] = x_ref[...].astype(o_ref.dtype)


def _to_bf16(w):
    depth, r, c = w.shape
    tr = 512 if r % 512 == 0 else r
    spec = pl.BlockSpec((None, tr, c), lambda l, i: (l, i, 0))
    return pl.pallas_call(
        _cast_kernel,
        grid=(depth, r // tr),
        in_specs=[spec],
        out_specs=spec,
        out_shape=jax.ShapeDtypeStruct(w.shape, BF16),
        compiler_params=_cparams(("parallel", "parallel")),
        name="cast_bf16",
    )(w)


def _head_norm_gate(of_ref, ob_ref, g_ref, on_ref, n_heads):
    x = of_ref[...] + ob_ref[...]
    g = g_ref[...]
    outs = []
    for h in range(n_heads):
        s = slice(h * LANES, (h + 1) * LANES)
        outs.append(_rms(x[:, s], on_ref[...]) * _silu(g[:, s]))
    return jnp.concatenate(outs, axis=-1)


def _merge_mlp_kernel(*refs, n_att, n_stream, nlb, final):
    atts, parts = refs[:n_att], refs[n_att:n_att + 3]
    n_parts = n_att + 3 + n_stream
    stream = refs[n_att + 3:n_parts]
    on_ref, wo_ref, mod_ref, n2_ref, w1_ref, w2_ref = refs[n_parts:n_parts + 6]
    o_ref = refs[-1]
    mod = mod_ref[...]
    if n_att:
        att = atts[0][...]
        if n_att == 2:
            att = jnp.where(pl.program_id(1) >= nlb, atts[1][...], att)
        na = att.shape[-1]
        y = _head_norm_gate(*parts, on_ref, GLA_HEADS)
        o = jnp.dot(att, wo_ref[0:na, :], preferred_element_type=F32) + _dot(y, wo_ref[na:, :])
    else:
        o = _dot(_head_norm_gate(*parts, on_ref, GDN_HEADS), wo_ref[...])
    h = _stream_block(stream, nlb) + mod[2:3] * o
    a = (_rms(h, n2_ref[...]) * (1.0 + mod[4:5]) + mod[3:4]).astype(BF16)
    ff = w1_ref.shape[1]
    hc = min(ff, MLP_HIDDEN_CHUNK)
    acc = None
    for c in range(ff // hc):
        u = jnp.maximum(jnp.dot(a, w1_ref[:, c * hc:(c + 1) * hc], preferred_element_type=F32), 0.0)
        t = _dot(u * u, w2_ref[c * hc:(c + 1) * hc, :])
        acc = t if acc is None else acc + t
    out = h + mod[5:6] * acc
    if final:
        out = _rms(out, refs[n_parts + 6][...])
    o_ref[...] = out


def _merge_mlp(atts, parts, on, wo, stream, mod, n2, w1, w2, layer, tm, n_lat, n_tok, final_norm=None):
    b, d = stream[0].shape[0], stream[0].shape[-1]
    nlb = n_lat // tm
    tok = lambda n: pl.BlockSpec((None, tm, n), lambda i, j: (i, j, 0))
    once = lambda a: pl.BlockSpec(a.shape, lambda *_: (0,) * a.ndim, pipeline_mode=pl.Buffered(1))
    slab = lambda a: pl.BlockSpec((None,) + a.shape[1:], lambda *_: (layer, 0, 0), pipeline_mode=pl.Buffered(1))
    att_maps = (lambda i, j: (i, jnp.minimum(j, nlb - 1), 0), lambda i, j: (i, jnp.maximum(j - nlb, 0), 0))
    args = [*atts, *parts, *stream, on, wo, mod, n2, w1, w2]
    specs = [pl.BlockSpec((None, tm, a.shape[-1]), m) for a, m in zip(atts, att_maps)] + [
        tok(p.shape[-1]) for p in parts] + _stream_specs(stream, tm, nlb) + [
        _full(on.shape), once(wo),
        pl.BlockSpec((None, None, N_MOD, d), lambda i, j: (i, j // nlb, 0, 0)),
        _full(n2.shape), slab(w1), slab(w2)]
    if final_norm is not None:
        args.append(final_norm)
        specs.append(_full(final_norm.shape))
    return pl.pallas_call(
        functools.partial(_merge_mlp_kernel, n_att=len(atts), n_stream=len(stream), nlb=nlb,
                          final=final_norm is not None),
        grid=(b, n_tok // tm),
        in_specs=specs,
        out_specs=tok(d),
        out_shape=jax.ShapeDtypeStruct((b, n_tok, d), F32),
        compiler_params=_cparams(("parallel", "parallel")),
        name="merge_mlp",
    )(*args)


_O_QKV = 2 * GDN_HEADS * GDN_DK + GDN_HEADS * GDN_DV
_O_QK = 2 * GDN_HEADS * GDN_DK
_O_G = GDN_HEADS * GDN_DV
_O_N = _O_QKV + _O_G + LANES
HALO = 8


def _odd_proj_kernel(h_ref, hp_ref, hn_ref, mod_ref, n1_ref, win_ref, cw_ref, e_ref, et_ref, rs_ref,
                     al_ref, dt_ref, q_out, k_out, v_out, g_out, gt_out, *, nlb, nb):
    j = pl.program_id(1)
    mod = mod_ref[...]
    n1 = n1_ref[...]
    pre = lambda x: (_rms(x, n1) * (1.0 + mod[1:2]) + mod[0:1]).astype(BF16)
    a = pre(h_ref[...])
    first = jnp.logical_or(j == 0, j == nlb)
    last = jnp.logical_or(j == nlb - 1, j == nb - 1)
    zero_halo = jnp.zeros((HALO, a.shape[1]), BF16)
    a_ext = jnp.concatenate([a, jnp.where(last, zero_halo, pre(hn_ref[...])),
                             jnp.where(first, zero_halo, pre(hp_ref[...]))], axis=0)
    tm = a.shape[0]
    te = tm + 2 * HALO
    hq = GDN_HEADS * GDN_DK
    gw = hq
    mm = lambda lhs, c0, c1: jnp.dot(lhs, win_ref[:, c0:c1], preferred_element_type=F32)
    project = lambda g: mm(a_ext, g * gw, (g + 1) * gw)

    pending = project(0)
    for g in range(_O_QKV // gw):
        xe = pending
        if (g + 1) * gw < _O_QKV:
            pending = project(g + 1)
        else:
            y_tail = mm(a, _O_QKV, _O_N)
        cw = cw_ref[:, g * gw:(g + 1) * gw]
        s = _silu(cw[0:1] * pltpu.roll(xe, 1, axis=0)[:tm] + cw[1:2] * xe[:tm]
                  + cw[2:3] * pltpu.roll(xe, te - 1, axis=0)[:tm])
        if g < 2:
            ss = _dot(s * s, e_ref[g * gw:(g + 1) * gw, :])
            r = lax.rsqrt(ss + EPS) * rs_ref[...]
            et = et_ref[:, g * gw:(g + 1) * gw]
            rf = sum(jnp.dot(p, et, preferred_element_type=F32) for p in _split(r, 2))
            (q_out, k_out)[g][...] = s * rf
        else:
            v_out[:, (g - 2) * gw:(g - 1) * gw] = s
    g_out[...] = y_tail[:, 0:_O_G]
    t = y_tail[:, _O_G:]
    log_a = -jnp.exp(al_ref[...]) * _softplus(t + dt_ref[...])
    lane = lax.broadcasted_iota(jnp.int32, t.shape, 1)
    gt_out[...] = jnp.where(lane < 2 * GDN_HEADS, log_a, _sigmoid(t))


def _odd_project(h, mod, n1, w_in, conv_w, a_log, dt_bias, tm, n_lat):
    b, t, d = h.shape
    nlb, nb = n_lat // tm, t // tm
    win = jnp.concatenate([w_in, jnp.zeros((d, _O_N - w_in.shape[1]), F32)], axis=1).astype(BF16)
    nqk = 2 * GDN_HEADS
    e = (jnp.arange(_O_QK)[:, None] // GDN_DK == jnp.arange(LANES)[None, :]).astype(BF16)
    et = e.T
    rs = jnp.concatenate([jnp.full((GDN_HEADS,), GDN_DK ** -0.5, F32), jnp.ones((LANES - GDN_HEADS,), F32)])[None, :]
    padl = lambda v: jnp.concatenate([v.reshape(-1), jnp.zeros((LANES - v.size,), F32)])[None, :]
    al, dt = padl(a_log), padl(dt_bias)
    tok = lambda n: pl.BlockSpec((None, tm, n), lambda i, j: (i, j, 0))
    hb = tm // HALO
    outs = [GDN_HEADS * GDN_DK, GDN_HEADS * GDN_DK, GDN_HEADS * GDN_DV, _O_G, LANES]
    return pl.pallas_call(
        functools.partial(_odd_proj_kernel, nlb=nlb, nb=nb),
        grid=(b, nb),
        in_specs=[tok(d),
                  pl.BlockSpec((None, HALO, d), lambda i, j: (i, jnp.maximum(j * hb - 1, 0), 0)),
                  pl.BlockSpec((None, HALO, d), lambda i, j: (i, jnp.minimum((j + 1) * hb, t // HALO - 1), 0)),
                  pl.BlockSpec((None, None, N_MOD, d), lambda i, j: (i, j // nlb, 0, 0)),
                  _full(n1.shape), _full(win.shape), _full(conv_w.shape), _full(e.shape), _full(et.shape),
                  _full(rs.shape), _full(al.shape), _full(dt.shape)],
        out_specs=[tok(n) for n in outs],
        out_shape=[jax.ShapeDtypeStruct((b, t, n), F32) for n in outs],
        compiler_params=_cparams(("parallel", "parallel")),
        name="odd_proj",
    )(h, h, h, mod, n1, win, conv_w, e, et, rs, al, dt)


def _gdn_kernel(qf, kf, vf, gcf, grf, qb, kb, vb, gcb, grb, of, ob, s_ref):
    @pl.when(pl.program_id(2) == 0)
    def _():
        s_ref[...] = jnp.zeros_like(s_ref)

    blk = qf.shape[0]
    nc = blk // CHUNK
    nh = qf.shape[1] // GDN_DK
    i32 = jnp.int32
    ri = lax.broadcasted_iota(i32, (CHUNK, CHUNK), 0)
    ci = lax.broadcasted_iota(i32, (CHUNK, CHUNK), 1)
    same = [(ri // w) == (ci // w) for w in (2, 4, 8, 16, 32)]
    lvl = [same[0]] + [jnp.logical_and(same[i], jnp.logical_not(same[i - 1])) for i in range(1, 5)]
    lvl.append(jnp.logical_not(same[4]))
    eye = jnp.where(ri == ci, 1.0, 0.0).astype(F32)
    incl = (ri >= ci, ri <= ci)
    strict = (ri > ci, ri < ci)
    bi = lax.broadcasted_iota(i32, (blk, blk), 0)
    bj = lax.broadcasted_iota(i32, (blk, blk), 1)
    same_chunk = (bi // CHUNK) == (bj // CHUNK)

    zs, cs, hs, qs, ks, vs, gbs, bbs, grows, brows, gls = ([] for _ in range(11))
    sq = (CHUNK, CHUNK)
    for z, (q, k, v, gc, gr) in enumerate(((qf, kf, vf, gcf, grf), (qb, kb, vb, gcb, grb))):
        tri = jnp.where(jnp.logical_and(same_chunk, (bi <= bj) if z else (bi >= bj)), 1.0, 0.0).astype(BF16)
        gcv, grv = gc[...], gr[...]
        cum_col = _dot_exact_lhs(tri, gcv)
        cum_row = _dot_exact_lhs(tri, grv, nt=True)
        for c in range(nc):
            r = slice(c * CHUNK, (c + 1) * CHUNK)
            last = c * CHUNK if z else (c + 1) * CHUNK - 1
            for hh in range(nh):
                idx = nh * z + hh
                zs.append(z), cs.append(c), hs.append(hh)
                qs.append(q[r, hh * GDN_DK:(hh + 1) * GDN_DK])
                ks.append(k[r, hh * GDN_DK:(hh + 1) * GDN_DK])
                vs.append(v[r, hh * GDN_DV:(hh + 1) * GDN_DV])
                gbs.append(jnp.broadcast_to(cum_col[r, idx:idx + 1], sq))
                bbs.append(jnp.broadcast_to(gcv[r, 2 * nh + idx:2 * nh + idx + 1], sq))
                grows.append(cum_row[idx:idx + 1, r])
                brows.append(grv[2 * nh + idx:2 * nh + idx + 1, r])
                gls.append(cum_col[last:last + 1, idx:idx + 1])

    decays = _stage(lambda z, gb, gr_: jnp.where(incl[z], jnp.exp(jnp.where(incl[z], gb - gr_, 0.0)), 0.0),
                    zs, gbs, grows)
    qkk = _stage(lambda q, k: _dot_nt(jnp.concatenate([q, k], axis=0), k), qs, ks)
    ns = _stage(lambda z, x, bb, d_: jnp.where(strict[z], x[CHUNK:] * bb * d_, 0.0), zs, qkk, bbs, decays)
    ts = _stage(lambda n: eye - jnp.where(lvl[0], n, 0.0), ns)
    for off in lvl[1:]:
        xs = _stage(lambda t, n: _dot(t, jnp.where(off, n, 0.0)), ts, ns)
        ts = _stage(lambda t, x: t - _dot(x, t), ts, xs)
    egs = _stage(jnp.exp, gbs)
    uws = _stage(lambda t, br, v, k, e_: _dot(t * br, jnp.concatenate([v, k * e_], axis=-1)),
                 ts, brows, vs, ks, egs)
    qks = _stage(lambda x, d_: x[:CHUNK] * d_, qkk, decays)
    kes = _stage(lambda k, gl, gb: k * jnp.exp(gl - gb), ks, gls, gbs)
    kuws = _stage(_dot_tn, kes, uws)
    quws = _stage(_dot, qks, uws)
    lhs = _stage(lambda q, e_, qu, ku: jnp.concatenate([q * e_ - qu[:, GDN_DV:], -ku[:, GDN_DV:]], axis=0),
                 qs, egs, quws, kuws)
    ges = _stage(jnp.exp, gls)

    chain = {(z, c, hh): i for i, (z, c, hh) in enumerate(zip(zs, cs, hs))}
    state = {(z, hh): s_ref[z, hh] for z in range(2) for hh in range(nh)}
    outs = (of, ob)
    for step in range(nc):
        for z in range(2):
            c = nc - 1 - step if z else step
            for hh in range(nh):
                i = chain[(z, c, hh)]
                s = state[(z, hh)]
                r = _dot(lhs[i], s)
                outs[z][c * CHUNK:(c + 1) * CHUNK, hh * GDN_DV:(hh + 1) * GDN_DV] = (
                    r[:CHUNK] + quws[i][:, :GDN_DV])
                state[(z, hh)] = s * ges[i] + (kuws[i][:, :GDN_DV] + r[CHUNK:])
    for (z, hh), s in state.items():
        s_ref[z, hh] = s


def _gdn_scan(q, k, v, gates, n_lat, n_ctx):
    b, t, _ = q.shape
    nh = GDN_SCAN_HEADS
    ng = GDN_HEADS // nh
    g = gates[..., :4 * GDN_HEADS].reshape(b, t, 2, 2, ng, nh)
    gcol = g.transpose(0, 4, 1, 2, 3, 5).reshape(b, ng, t, 4 * nh)
    grow = gcol.transpose(0, 1, 3, 2)
    blk = _scan_blk(n_lat, n_ctx)
    nlb, ncb = n_lat // blk, n_ctx // blk
    fwd, bwd = _scan_maps(nlb, ncb)
    d = None
    tok = lambda n, m: pl.BlockSpec((d, blk, n), lambda i, h, j: (i, m(j), h))
    gcs = lambda m: pl.BlockSpec((d, d, blk, 4 * nh), lambda i, h, j: (i, h, m(j), 0))
    grs = lambda m: pl.BlockSpec((d, d, 4 * nh, blk), lambda i, h, j: (i, h, 0, m(j)))
    out = jax.ShapeDtypeStruct((b, t, GDN_HEADS * GDN_DV), F32)
    dk2, dv2 = nh * GDN_DK, nh * GDN_DV
    return pl.pallas_call(
        _gdn_kernel,
        grid=(b, ng, nlb + ncb),
        in_specs=[tok(dk2, fwd), tok(dk2, fwd), tok(dv2, fwd), gcs(fwd), grs(fwd),
                  tok(dk2, bwd), tok(dk2, bwd), tok(dv2, bwd), gcs(bwd), grs(bwd)],
        out_specs=[tok(dv2, fwd), tok(dv2, bwd)],
        out_shape=[out, out],
        scratch_shapes=[pltpu.VMEM((2, nh, GDN_DK, GDN_DV), F32)],
        compiler_params=_cparams(("parallel", "parallel", "arbitrary")),
        name="gdn_scan",
    )(q, k, v, gcol, grow, q, k, v, gcol, grow)


def kernel(x, c, ctx, c_ctx, ada_w, ada_b, norm1_w, norm2_w, mlp_w1, mlp_w2, even_w_in, mla_q_norm,
           mla_w_uq, mla_kv_norm, mla_w_ukv, gla_gate_up, gla_gate_bias, gla_o_norm, even_w_out,
           gdn_w_in, gdn_conv_w, gdn_a_log, gdn_dt_bias, gdn_o_norm, gdn_w_out, final_norm):
    b, n_lat, d = x.shape
    n_ctx = ctx.shape[1]
    depth = ada_w.shape[0]
    tm = 256 if (n_lat % 256 == 0 and n_ctx % 256 == 0) else 128
    assert n_lat % tm == 0 and n_ctx % tm == 0 and n_lat % n_ctx == 0 and n_lat % GRID_W == 0

    rows = -(-(b + 1) // 8) * 8
    cc = jnp.concatenate([c, c_ctx[None, :], jnp.zeros((rows - b - 1, d), F32)], axis=0)
    mods = _ada_all(cc, ada_w, ada_b).reshape(depth, rows, N_MOD, d)
    mod_tok = jnp.stack([mods[:, :b], jnp.broadcast_to(mods[:, b:b + 1], (depth, b, N_MOD, d))], axis=2)

    stream = (x, ctx)
    tabs = _rope_tables(n_lat, n_ctx)
    w1_bf, w2_bf = _to_bf16(mlp_w1), _to_bf16(mlp_w2)
    row = lambda v: v[None, :]
    for layer in range(depth):
        need_ctx = layer < depth - 1
        n_tok = n_lat + n_ctx if need_ctx else n_lat
        mod = mod_tok[layer]
        n1 = row(norm1_w[layer])
        i = layer // 2
        if layer % 2 == 0:
            ew = _even_weights(even_w_in[i], mla_w_uq[i], mla_w_ukv[i], gla_gate_up[i], gla_gate_bias[i])
            q, k, v, gq, gk, gv, gg, lg = _even_project(
                stream, mod, n1, ew, row(mla_q_norm[i]), row(mla_kv_norm[i]), tabs, tm, n_lat)
            atts = [a for a in _attention(q, k, v, n_lat, n_ctx, need_ctx) if a is not None]
            o_f, o_b = _gla_scan(gq, gk, gv, lg, n_lat, n_ctx)
            parts, on, wo = [o_f, o_b, gg], gla_o_norm[i], even_w_out[i]
        else:
            q, k, v, g, gates = _odd_project(stream[0], mod, n1, gdn_w_in[i], gdn_conv_w[i], gdn_a_log[i],
                                             gdn_dt_bias[i], tm, n_lat)
            o_f, o_b = _gdn_scan(q, k, v, gates, n_lat, n_ctx)
            atts, parts, on, wo = [], [o_f, o_b, g], gdn_o_norm[i], gdn_w_out[i]
        stream = (_merge_mlp(atts, parts, row(on), wo.astype(BF16), stream, mod, row(norm2_w[layer]), w1_bf, w2_bf,
                             layer, tm, n_lat, n_tok, final_norm=None if need_ctx else row(final_norm)),)
    return stream[0]
```

```python
import functools

import jax
import jax.numpy as jnp
from jax import lax
from jax.experimental import pallas as pl
from jax.experimental.pallas import tpu as pltpu

F32 = jnp.float32
BF16 = jnp.bfloat16
EPS = 1e-6
LOG2E = 1.4426950408889634

GRID_W = 64
N_MOD = 6
MLA_HEADS = 8
MLA_NOPE = 64
MLA_ROPE = 32
MLA_V = 64
MLA_QK = MLA_NOPE + MLA_ROPE
MLA_Q_RANK = 384
MLA_KV_RANK = 256
ROPE_THETA = 10000.0
GLA_HEADS = 4
GLA_DK = 64
GLA_DV = 128
GLA_GATE_RANK = 16
GLA_GATE_NORM = 16.0
GDN_HEADS = 8
GDN_DK = 64
GDN_DV = 128
CHUNK = 64
MLP_HIDDEN_CHUNK = 1024
GDN_SCAN_HEADS = 4
LANES = 128
HEAD_SLOT = 128
V_SLOT = 80
VMEM_LIMIT = 56 * 1024 * 1024


def _cparams(sem):
    return pltpu.CompilerParams(dimension_semantics=sem, vmem_limit_bytes=VMEM_LIMIT)


def _dot(a, b):
    return jnp.dot(a.astype(BF16), b.astype(BF16), preferred_element_type=F32)


def _dot_nt(a, b):
    return lax.dot_general(a.astype(BF16), b.astype(BF16), (((1,), (1,)), ((), ())),
                           preferred_element_type=F32)


def _dot_tn(a, b):
    return lax.dot_general(a.astype(BF16), b.astype(BF16), (((0,), (0,)), ((), ())),
                           preferred_element_type=F32)


def _split(x, n):
    out = []
    for _ in range(n - 1):
        p = x.astype(BF16)
        out.append(p)
        x = x - p.astype(F32)
    out.append(x.astype(BF16))
    return out


def _dot3(a, b):
    a1, a2 = _split(a, 2)
    b1, b2 = _split(b, 2)
    d = functools.partial(jnp.dot, preferred_element_type=F32)
    return d(a1, b1) + (d(a1, b2) + d(a2, b1))


def _dot_exact_lhs(m_bf, x, nt=False):
    acc = None
    for p in _split(x, 2):
        if nt:
            t = lax.dot_general(p, m_bf, (((1,), (1,)), ((), ())), preferred_element_type=F32)
        else:
            t = jnp.dot(m_bf, p, preferred_element_type=F32)
        acc = t if acc is None else acc + t
    return acc


def _rms(x, w):
    return x * lax.rsqrt(jnp.mean(x * x, axis=-1, keepdims=True) + EPS) * w


def _sigmoid(x):
    return 1.0 / (1.0 + jnp.exp(-x))


def _silu(x):
    return x * _sigmoid(x)


def _softplus(x):
    return jnp.maximum(x, 0.0) + jnp.log1p(jnp.exp(-jnp.abs(x)))


def _tile_lanes(x, n):
    return jnp.concatenate([x] * n, axis=-1)


def _stream_specs(stream, tm, nlb):
    d = stream[0].shape[-1]
    if len(stream) == 1:
        return [pl.BlockSpec((None, tm, d), lambda i, j: (i, j, 0))]
    return [pl.BlockSpec((None, tm, d), lambda i, j: (i, jnp.minimum(j, nlb - 1), 0)),
            pl.BlockSpec((None, tm, d), lambda i, j: (i, jnp.maximum(j - nlb, 0), 0))]


def _stream_block(refs, nlb):
    if len(refs) == 1:
        return refs[0][...]
    return jnp.where(pl.program_id(1) >= nlb, refs[1][...], refs[0][...])


def _stage(fn, *cols):
    return [fn(*a) for a in zip(*cols)]


def _ada_kernel(cc_ref, w_ref, b_ref, o_ref):
    o_ref[0] = _dot(_silu(cc_ref[...]), w_ref[0]) + b_ref[0]


def _ada_all(cc, ada_w, ada_b):
    depth, d, n = ada_w.shape
    rows = cc.shape[0]
    tn = 1024
    return pl.pallas_call(
        _ada_kernel,
        grid=(depth, n // tn),
        in_specs=[pl.BlockSpec((rows, d), lambda l, j: (0, 0)),
                  pl.BlockSpec((1, d, tn), lambda l, j: (l, 0, j)),
                  pl.BlockSpec((1, 1, tn), lambda l, j: (l, 0, j))],
        out_specs=pl.BlockSpec((1, rows, tn), lambda l, j: (l, 0, j)),
        out_shape=jax.ShapeDtypeStruct((depth, rows, n), F32),
        compiler_params=_cparams(("parallel", "parallel")),
        name="ada_mod",
    )(cc, ada_w, ada_b.reshape(depth, 1, n))


_E_CQ = (0, 384)
_E_CKV = (384, 640)
_E_ROT = (640, 768)
_E_GQ = (768, 1024)
_E_GK = (1024, 1280)
_E_GV = (1280, 1792)
_E_GG = (1792, 2304)
_E_N = 2304


def _even_proj_kernel(*refs, n_stream, nlb):
    (mod_ref, n1_ref, win_ref, qn_ref, wq_ref, kvn_ref, wk_ref, wv_ref, wg_ref, gb_ref, c_ref, s_ref,
     ctt_ref, stt_ref, q_out, k_out, v_out, gq_out, gk_out, gv_out, gg_out, lg_out) = refs[n_stream:]
    mod = mod_ref[...]
    a = _rms(_stream_block(refs[:n_stream], nlb), n1_ref[...]) * (1.0 + mod[1:2]) + mod[0:1]
    y = _dot(a, win_ref[...])
    sl = lambda r: y[:, r[0]:r[1]]
    ct, st = c_ref[...], s_ref[...]
    nh = MLA_HEADS
    qq = _dot_nt(wq_ref[...], _rms(sl(_E_CQ), qn_ref[...]))
    c32, s32 = ctt_ref[...], stt_ref[...]
    rot0 = nh * MLA_QK
    zero_pad = jnp.zeros((HEAD_SLOT - MLA_QK, qq.shape[1]), F32)
    pieces = []
    for h in range(nh):
        nope = qq[h * MLA_QK:h * MLA_QK + MLA_NOPE]
        pe = qq[h * MLA_QK + MLA_NOPE:(h + 1) * MLA_QK]
        pe_rot = qq[rot0 + h * MLA_ROPE:rot0 + (h + 1) * MLA_ROPE]
        pieces += [nope, pe * c32 + pe_rot * s32, zero_pad]
    q_out[...] = jnp.concatenate(pieces, axis=0).astype(q_out.dtype)
    ckvn = _rms(sl(_E_CKV), kvn_ref[...])
    yr = sl(_E_ROT)
    kpe = pltpu.roll(yr, MLA_NOPE, axis=1) * ct + pltpu.roll(yr, MLA_NOPE - MLA_ROPE, axis=1) * st
    k_out[...] = (_dot(ckvn, wk_ref[...]) + _tile_lanes(kpe, nh)).astype(k_out.dtype)
    vt = _dot_nt(wv_ref[...], ckvn)
    pad_rows = lax.broadcasted_iota(jnp.int32, (V_SLOT - MLA_V, vt.shape[1]), 0)
    ones_row = jnp.where(pad_rows == 0, 1.0, 0.0)
    v_out[...] = jnp.concatenate(
        [x for h in range(nh) for x in (vt[h * MLA_V:(h + 1) * MLA_V], ones_row)], axis=0).astype(v_out.dtype)
    gq_out[...] = sl(_E_GQ)
    gk_out[...] = sl(_E_GK)
    gv_out[...] = sl(_E_GV)
    gg_out[...] = sl(_E_GG)
    logit = _dot(yr, wg_ref[...]) + gb_ref[...]
    lg_out[...] = (jnp.minimum(logit, 0.0) - jnp.log1p(jnp.exp(-jnp.abs(logit)))) * (1.0 / GLA_GATE_NORM)


def _rot_half_cols(w):
    f = MLA_ROPE // 4
    return jnp.concatenate([-w[..., f:2 * f], w[..., 0:f], -w[..., 3 * f:4 * f], w[..., 2 * f:3 * f]], axis=-1)


def _even_weights(w_in, w_uq, w_ukv, gate_up, gate_bias):
    d = w_in.shape[0]
    cq, ckv, kpe, gq, gk, gv, gg, glow = jnp.split(
        w_in, [384, 640, 672, 928, 1184, 1696, 2208], axis=1)
    lowr = 2 * GLA_GATE_RANK
    win = jnp.concatenate([cq, ckv, kpe, _rot_half_cols(kpe), glow, jnp.zeros((d, LANES - 2 * MLA_ROPE - lowr), F32),
                           gq, gk, gv, gg], axis=1).astype(BF16)
    r = w_uq.shape[0]
    wq = (w_uq * (MLA_QK ** -0.5 * LOG2E)).reshape(r, MLA_HEADS, MLA_QK)
    wq_rot = _rot_half_cols(wq[..., MLA_NOPE:])
    wq_all = jnp.concatenate([wq.reshape(r, -1), wq_rot.reshape(r, -1)], axis=1).T.astype(BF16)
    rk = w_ukv.shape[0]
    wkv = w_ukv.reshape(rk, MLA_HEADS, MLA_NOPE + MLA_V)
    wk = jnp.concatenate([wkv[..., :MLA_NOPE], jnp.zeros((rk, MLA_HEADS, HEAD_SLOT - MLA_NOPE), F32)],
                         axis=-1).reshape(rk, -1).astype(BF16)
    wv = wkv[..., MLA_NOPE:].reshape(rk, -1).T.astype(BF16)
    hk = GLA_HEADS * GLA_DK
    g0 = 2 * MLA_ROPE
    wg = jnp.zeros((LANES, 2 * hk), F32)
    wg = wg.at[g0:g0 + GLA_GATE_RANK, 0:hk].set(gate_up[0])
    wg = wg.at[g0 + GLA_GATE_RANK:g0 + 2 * GLA_GATE_RANK, hk:].set(gate_up[1])
    gb = jnp.concatenate([gate_bias[0], gate_bias[1]])[None, :]
    return win, wq_all, wk, wv, wg.astype(BF16), gb


def _rope_tables(n_lat, n_ctx):
    rows = n_lat // GRID_W
    row = jnp.repeat(jnp.arange(rows, dtype=F32), GRID_W)
    col = jnp.tile(jnp.arange(GRID_W, dtype=F32), rows)
    axis_dim = MLA_ROPE // 2
    inv_freq = ROPE_THETA ** (-jnp.arange(0, axis_dim, 2, dtype=F32) / axis_dim)
    ar, ac = row[:, None] * inv_freq, col[:, None] * inv_freq
    c32 = jnp.concatenate([jnp.cos(ar), jnp.cos(ar), jnp.cos(ac), jnp.cos(ac)], axis=1)
    s32 = jnp.concatenate([jnp.sin(ar), jnp.sin(ar), jnp.sin(ac), jnp.sin(ac)], axis=1)
    c32 = jnp.concatenate([c32, jnp.ones((n_ctx, MLA_ROPE), F32)], axis=0)
    s32 = jnp.concatenate([s32, jnp.zeros((n_ctx, MLA_ROPE), F32)], axis=0)
    slot = lambda x: jnp.pad(x, ((0, 0), (MLA_NOPE, HEAD_SLOT - MLA_QK)))
    return slot(c32), slot(s32), c32.T, s32.T


def _full(shape):
    nd = len(shape)
    return pl.BlockSpec(shape, lambda *_: (0,) * nd)


def _even_project(stream, mod, n1, ew, qn, kvn, tabs, tm, n_lat):
    b, d = stream[0].shape[0], stream[0].shape[-1]
    t = sum(s.shape[1] for s in stream)
    win, wq, wk, wv, wg, gb = ew
    ct, st, ctt, stt = tabs
    nlb = n_lat // tm
    tok = lambda n: pl.BlockSpec((None, tm, n), lambda i, j: (i, j, 0))
    tab = pl.BlockSpec((tm, HEAD_SLOT), lambda i, j: (j, 0))
    tabt = pl.BlockSpec((MLA_ROPE, tm), lambda i, j: (0, j))
    hk, hv = GLA_HEADS * GLA_DK, GLA_HEADS * GLA_DV
    feat = lambda n: pl.BlockSpec((None, n, tm), lambda i, j: (i, 0, j))
    tshape = lambda n, dt: jax.ShapeDtypeStruct((b, t, n), dt)
    fshape = lambda n, dt: jax.ShapeDtypeStruct((b, n, t), dt)
    nq, nv = MLA_HEADS * HEAD_SLOT, MLA_HEADS * V_SLOT
    return pl.pallas_call(
        functools.partial(_even_proj_kernel, n_stream=len(stream), nlb=nlb),
        grid=(b, t // tm),
        in_specs=_stream_specs(stream, tm, nlb) + [
                  pl.BlockSpec((None, None, N_MOD, d), lambda i, j: (i, j // nlb, 0, 0)),
                  _full(n1.shape), _full(win.shape), _full(qn.shape), _full(wq.shape), _full(kvn.shape),
                  _full(wk.shape), _full(wv.shape), _full(wg.shape), _full(gb.shape), tab, tab, tabt, tabt],
        out_specs=[feat(nq), tok(nq), feat(nv), tok(hk), tok(hk), tok(hv), tok(hv), tok(2 * hk)],
        out_shape=[fshape(nq, BF16), tshape(nq, BF16), fshape(nv, BF16), tshape(hk, F32), tshape(hk, F32),
                   tshape(hv, F32), tshape(hv, F32), tshape(2 * hk, F32)],
        compiler_params=_cparams(("parallel", "parallel")),
        name="even_proj",
    )(*stream, mod, n1, win, qn, wq, kvn, wk, wv, wg, gb, ct, st, ctt, stt)


def _attn_kernel(qt_ref, k_ref, vt_ref, o_ref):
    tk_all = k_ref.shape[0]
    tk = 256 if tk_all % 256 == 0 else 128
    heads = range(qt_ref.shape[0] // HEAD_SLOT)
    nkb = tk_all // tk
    qts = [qt_ref[hh * HEAD_SLOT:(hh + 1) * HEAD_SLOT, :] for hh in heads]

    def scores(kb):
        return [jnp.dot(k_ref[kb * tk:(kb + 1) * tk, hh * HEAD_SLOT:(hh + 1) * HEAD_SLOT], qts[hh],
                        preferred_element_type=F32) for hh in heads]

    s_next = scores(0)
    m = acc = None
    for kb in range(nkb):
        s = s_next
        if kb + 1 < nkb:
            s_next = scores(kb + 1)
        bm = [jnp.max(x, axis=0, keepdims=True) for x in s]
        if m is None:
            m_new = bm
        else:
            m_new = [jnp.maximum(a, b_) for a, b_ in zip(m, bm)]
        p = [jnp.exp2(x - mn) for x, mn in zip(s, m_new)]
        pv = [jnp.dot(vt_ref[hh * V_SLOT:(hh + 1) * V_SLOT, kb * tk:(kb + 1) * tk], p[hh].astype(BF16),
                      preferred_element_type=F32) for hh in heads]
        if m is None:
            acc = pv
        else:
            acc = [jnp.exp2(a - b_) * x + y for a, b_, x, y in zip(m, m_new, acc, pv)]
        m = m_new
    o_ref[...] = jnp.concatenate([(a[:MLA_V] / a[MLA_V:MLA_V + 1]).T for a in acc], axis=-1).astype(o_ref.dtype)


def _attention(qt, k, vt, n_lat, n_ctx, need_ctx):
    b, t, _ = k.shape
    hp = MLA_HEADS // 2
    tq = 1024 if n_lat % 1024 == 0 else 128
    nv = MLA_HEADS * MLA_V
    att = pl.pallas_call(
        _attn_kernel,
        grid=(b, hp, n_lat // tq),
        in_specs=[pl.BlockSpec((None, 2 * HEAD_SLOT, tq), lambda i, h, j: (i, h, j)),
                  pl.BlockSpec((None, t, 2 * HEAD_SLOT), lambda i, h, j: (i, 0, h)),
                  pl.BlockSpec((None, 2 * V_SLOT, t), lambda i, h, j: (i, h, 0))],
        out_specs=pl.BlockSpec((None, tq, 2 * MLA_V), lambda i, h, j: (i, j, h)),
        out_shape=jax.ShapeDtypeStruct((b, n_lat, nv), BF16),
        compiler_params=_cparams(("parallel", "parallel", "arbitrary")),
        name="mla_attn_lat",
    )(qt, k, vt)
    if not need_ctx:
        return att, None
    cb = t // n_ctx - 1
    att_ctx = pl.pallas_call(
        _attn_kernel,
        grid=(b,),
        in_specs=[pl.BlockSpec((None, MLA_HEADS * HEAD_SLOT, n_ctx), lambda i: (i, 0, cb)),
                  pl.BlockSpec((None, n_ctx, MLA_HEADS * HEAD_SLOT), lambda i: (i, cb, 0)),
                  pl.BlockSpec((None, MLA_HEADS * V_SLOT, n_ctx), lambda i: (i, 0, cb))],
        out_specs=pl.BlockSpec((None, n_ctx, nv), lambda i: (i, 0, 0)),
        out_shape=jax.ShapeDtypeStruct((b, n_ctx, nv), BF16),
        compiler_params=_cparams(("parallel",)),
        name="mla_attn_ctx",
    )(qt, k, vt)
    return att, att_ctx


def _scan_maps(nlb, ncb):
    fwd = lambda j: jnp.where(j < ncb, nlb + j, j - ncb)
    bwd = lambda j: nlb + ncb - 1 - j
    return fwd, bwd


def _scan_blk(n_lat, n_ctx):
    for blk in (256, 128, 64):
        if n_lat % blk == 0 and n_ctx % blk == 0:
            return blk
    raise ValueError("sequence lengths must be multiples of the 64-token chunk")


def _gla_kernel(qf, kf, vf, lf, qb, kb, vb, lb, of, ob, sf, sb):
    @pl.when(pl.program_id(1) == 0)
    def _():
        sf[...] = jnp.zeros_like(sf)
        sb[...] = jnp.zeros_like(sb)

    hk, hv = GLA_HEADS * GLA_DK, GLA_HEADS * GLA_DV
    i32 = jnp.int32
    io = lambda shape, d: lax.broadcasted_iota(i32, shape, d)
    blk = qf.shape[0]
    nc = blk // CHUNK
    aj = io((CHUNK, GLA_HEADS * CHUNK), 1) % CHUNK
    ai = io((CHUNK, GLA_HEADS * CHUNK), 0)
    a_incl = (ai >= aj, ai <= aj)
    klane = io((CHUNK, hk), 1) // GLA_DK
    bi, bj = io((blk, blk), 0), io((blk, blk), 1)
    same_chunk = (bi // CHUNK) == (bj // CHUNK)

    zs, cs, qs, ks, vs, bcs, bts = ([] for _ in range(7))
    for z, (q, k, v, lg) in enumerate(((qf, kf, vf, lf), (qb, kb, vb, lb))):
        tri = jnp.where(jnp.logical_and(same_chunk, (bi <= bj) if z else (bi >= bj)), 1.0, 0.0).astype(BF16)
        bcum = _dot_exact_lhs(tri, lg[:, z * hk:(z + 1) * hk])
        for c in range(nc):
            r = slice(c * CHUNK, (c + 1) * CHUNK)
            last = c * CHUNK if z else (c + 1) * CHUNK - 1
            zs.append(z), cs.append(c)
            qs.append(q[r, :]), ks.append(k[r, :]), vs.append(v[r, :])
            bcs.append(bcum[r, :]), bts.append(bcum[last:last + 1, :])

    heads = range(GLA_HEADS)
    hd = lambda x, h, w: x[:, h * w:(h + 1) * w]
    zero_k = jnp.zeros((CHUNK, hk), BF16)
    zero_v = jnp.zeros((CHUNK, GLA_DV), BF16)
    qds = _stage(lambda q, b_: q * (GLA_DK ** -0.5) * jnp.exp(b_), qs, bcs)
    kinv = _stage(lambda k, b_: (k * jnp.exp(-b_)).astype(BF16), ks, bcs)
    kbs = _stage(lambda x: jnp.concatenate([jnp.where(klane == h, x, zero_k) for h in heads], axis=0), kinv)
    a_s = _stage(lambda z, qd, kb_: jnp.where(a_incl[z], _dot_nt(qd, kb_), 0.0), zs, qds, kbs)
    vbs = _stage(lambda v: jnp.concatenate(
        [jnp.concatenate([hd(v, j, GLA_DV).astype(BF16) if j == h else zero_v for j in heads], axis=1)
         for h in heads], axis=0), vs)
    ois = _stage(_dot, a_s, vbs)
    kes = _stage(lambda k, b_, bt: k * jnp.exp(bt - b_), ks, bcs, bts)
    upds = _stage(lambda v, ke: [_dot_tn(hd(v, h, GLA_DV), hd(ke, h, GLA_DK)) for h in heads], vs, kes)
    ges = _stage(jnp.exp, bts)

    chain = {(z, c): i for i, (z, c) in enumerate(zip(zs, cs))}
    state = [[s_[h] for h in heads] for s_ in (sf, sb)]
    outs = (of, ob)
    for step in range(nc):
        for z in range(2):
            c = nc - 1 - step if z else step
            i = chain[(z, c)]
            inter = [_dot_nt(hd(qds[i], h, GLA_DK), state[z][h]) for h in heads]
            outs[z][c * CHUNK:(c + 1) * CHUNK, :] = ois[i] + jnp.concatenate(inter, axis=1)
            state[z] = [state[z][h] * hd(ges[i], h, GLA_DK) + upds[i][h] for h in heads]
    for s_, st in zip((sf, sb), state):
        for h in heads:
            s_[h] = st[h]


def _gla_scan(gq, gk, gv, lg, n_lat, n_ctx):
    b, t, hk = gq.shape
    hv = gv.shape[-1]
    blk = _scan_blk(n_lat, n_ctx)
    nlb, ncb = n_lat // blk, n_ctx // blk
    fwd, bwd = _scan_maps(nlb, ncb)
    spec = lambda n, m: pl.BlockSpec((None, blk, n), lambda i, j: (i, m(j), 0))
    out = jax.ShapeDtypeStruct((b, t, hv), F32)
    return pl.pallas_call(
        _gla_kernel,
        grid=(b, nlb + ncb),
        in_specs=[spec(hk, fwd), spec(hk, fwd), spec(hv, fwd), spec(2 * hk, fwd),
                  spec(hk, bwd), spec(hk, bwd), spec(hv, bwd), spec(2 * hk, bwd)],
        out_specs=[spec(hv, fwd), spec(hv, bwd)],
        out_shape=[out, out],
        scratch_shapes=[pltpu.VMEM((GLA_HEADS, GLA_DV, GLA_DK), F32)] * 2,
        compiler_params=_cparams(("parallel", "arbitrary")),
        name="gla_scan",
    )(gq, gk, gv, lg, gq, gk, gv, lg)


def _cast_kernel(x_ref, o_ref):
    o_ref[...] = x_ref[...].astype(o_ref.dtype)


def _to_bf16(w):
    depth, r, c = w.shape
    tr = 512 if r % 512 == 0 else r
    spec = pl.BlockSpec((None, tr, c), lambda l, i: (l, i, 0))
    return pl.pallas_call(
        _cast_kernel,
        grid=(depth, r // tr),
        in_specs=[spec],
        out_specs=spec,
        out_shape=jax.ShapeDtypeStruct(w.shape, BF16),
        compiler_params=_cparams(("parallel", "parallel")),
        name="cast_bf16",
    )(w)


def _head_norm_gate(of_ref, ob_ref, g_ref, on_ref, n_heads):
    x = of_ref[...] + ob_ref[...]
    g = g_ref[...]
    outs = []
    for h in range(n_heads):
        s = slice(h * LANES, (h + 1) * LANES)
        outs.append(_rms(x[:, s], on_ref[...]) * _silu(g[:, s]))
    return jnp.concatenate(outs, axis=-1)


def _merge_mlp_kernel(*refs, n_att, n_stream, nlb, final):
    atts, parts = refs[:n_att], refs[n_att:n_att + 3]
    n_parts = n_att + 3 + n_stream
    stream = refs[n_att + 3:n_parts]
    on_ref, wo_ref, mod_ref, n2_ref, w1_ref, w2_ref = refs[n_parts:n_parts + 6]
    o_ref = refs[-1]
    mod = mod_ref[...]
    if n_att:
        att = atts[0][...]
        if n_att == 2:
            att = jnp.where(pl.program_id(1) >= nlb, atts[1][...], att)
        na = att.shape[-1]
        y = _head_norm_gate(*parts, on_ref, GLA_HEADS)
        o = jnp.dot(att, wo_ref[0:na, :], preferred_element_type=F32) + _dot(y, wo_ref[na:, :])
    else:
        o = _dot(_head_norm_gate(*parts, on_ref, GDN_HEADS), wo_ref[...])
    h = _stream_block(stream, nlb) + mod[2:3] * o
    a = (_rms(h, n2_ref[...]) * (1.0 + mod[4:5]) + mod[3:4]).astype(BF16)
    ff = w1_ref.shape[1]
    hc = min(ff, MLP_HIDDEN_CHUNK)
    acc = None
    for c in range(ff // hc):
        u = jnp.maximum(jnp.dot(a, w1_ref[:, c * hc:(c + 1) * hc], preferred_element_type=F32), 0.0)
        t = _dot(u * u, w2_ref[c * hc:(c + 1) * hc, :])
        acc = t if acc is None else acc + t
    out = h + mod[5:6] * acc
    if final:
        out = _rms(out, refs[n_parts + 6][...])
    o_ref[...] = out


def _merge_mlp(atts, parts, on, wo, stream, mod, n2, w1, w2, layer, tm, n_lat, n_tok, final_norm=None):
    b, d = stream[0].shape[0], stream[0].shape[-1]
    nlb = n_lat // tm
    tok = lambda n: pl.BlockSpec((None, tm, n), lambda i, j: (i, j, 0))
    once = lambda a: pl.BlockSpec(a.shape, lambda *_: (0,) * a.ndim, pipeline_mode=pl.Buffered(1))
    slab = lambda a: pl.BlockSpec((None,) + a.shape[1:], lambda *_: (layer, 0, 0), pipeline_mode=pl.Buffered(1))
    att_maps = (lambda i, j: (i, jnp.minimum(j, nlb - 1), 0), lambda i, j: (i, jnp.maximum(j - nlb, 0), 0))
    args = [*atts, *parts, *stream, on, wo, mod, n2, w1, w2]
    specs = [pl.BlockSpec((None, tm, a.shape[-1]), m) for a, m in zip(atts, att_maps)] + [
        tok(p.shape[-1]) for p in parts] + _stream_specs(stream, tm, nlb) + [
        _full(on.shape), once(wo),
        pl.BlockSpec((None, None, N_MOD, d), lambda i, j: (i, j // nlb, 0, 0)),
        _full(n2.shape), slab(w1), slab(w2)]
    if final_norm is not None:
        args.append(final_norm)
        specs.append(_full(final_norm.shape))
    return pl.pallas_call(
        functools.partial(_merge_mlp_kernel, n_att=len(atts), n_stream=len(stream), nlb=nlb,
                          final=final_norm is not None),
        grid=(b, n_tok // tm),
        in_specs=specs,
        out_specs=tok(d),
        out_shape=jax.ShapeDtypeStruct((b, n_tok, d), F32),
        compiler_params=_cparams(("parallel", "parallel")),
        name="merge_mlp",
    )(*args)


_O_QKV = 2 * GDN_HEADS * GDN_DK + GDN_HEADS * GDN_DV
_O_QK = 2 * GDN_HEADS * GDN_DK
_O_G = GDN_HEADS * GDN_DV
_O_N = _O_QKV + _O_G + LANES
HALO = 8


def _odd_proj_kernel(h_ref, hp_ref, hn_ref, mod_ref, n1_ref, win_ref, cw_ref, e_ref, et_ref, rs_ref,
                     al_ref, dt_ref, sel_ref, q_out, k_out, v_out, g_out, gc_out, gr_out, *, nlb, nb):
    j = pl.program_id(1)
    mod = mod_ref[...]
    n1 = n1_ref[...]
    pre = lambda x: (_rms(x, n1) * (1.0 + mod[1:2]) + mod[0:1]).astype(BF16)
    a = pre(h_ref[...])
    first = jnp.logical_or(j == 0, j == nlb)
    last = jnp.logical_or(j == nlb - 1, j == nb - 1)
    zero_halo = jnp.zeros((HALO, a.shape[1]), BF16)
    a_ext = jnp.concatenate([a, jnp.where(last, zero_halo, pre(hn_ref[...])),
                             jnp.where(first, zero_halo, pre(hp_ref[...]))], axis=0)
    tm = a.shape[0]
    te = tm + 2 * HALO
    hq = GDN_HEADS * GDN_DK
    gw = hq
    mm = lambda lhs, c0, c1: jnp.dot(lhs, win_ref[:, c0:c1], preferred_element_type=F32)
    project = lambda g: mm(a_ext, g * gw, (g + 1) * gw)

    pending = project(0)
    for g in range(_O_QKV // gw):
        xe = pending
        if (g + 1) * gw < _O_QKV:
            pending = project(g + 1)
        else:
            y_tail = mm(a, _O_QKV, _O_N)
        cw = cw_ref[:, g * gw:(g + 1) * gw]
        s = _silu(cw[0:1] * pltpu.roll(xe, 1, axis=0)[:tm] + cw[1:2] * xe[:tm]
                  + cw[2:3] * pltpu.roll(xe, te - 1, axis=0)[:tm])
        if g < 2:
            ss = _dot(s * s, e_ref[g * gw:(g + 1) * gw, :])
            r = lax.rsqrt(ss + EPS) * rs_ref[...]
            et = et_ref[:, g * gw:(g + 1) * gw]
            rf = sum(jnp.dot(p, et, preferred_element_type=F32) for p in _split(r, 2))
            (q_out, k_out)[g][...] = s * rf
        else:
            v_out[:, (g - 2) * gw:(g - 1) * gw] = s
    g_out[...] = y_tail[:, 0:_O_G]
    t = y_tail[:, _O_G:]
    log_a = -jnp.exp(al_ref[...]) * _softplus(t + dt_ref[...])
    lane = lax.broadcasted_iota(jnp.int32, t.shape, 1)
    gw = gc_out.shape[-1]
    gates = jnp.where(lane % gw < gw // 2, log_a, _sigmoid(t))
    rows_t = sum(lax.dot_general(sel_ref[...], p, (((1,), (1,)), ((), ())), preferred_element_type=F32)
                 for p in _split(gates, 2))
    for g in range(gc_out.shape[0]):
        gc_out[g] = gates[:, g * gw:(g + 1) * gw]
        gr_out[g] = rows_t[g * gw:(g + 1) * gw]


def _odd_project(h, mod, n1, w_in, conv_w, a_log, dt_bias, tm, n_lat):
    b, t, d = h.shape
    nlb, nb = n_lat // tm, t // tm
    nh, hh_ = GDN_SCAN_HEADS, GDN_HEADS
    ng, gw = hh_ // nh, 4 * GDN_SCAN_HEADS
    src = [kind * 2 * hh_ + z * hh_ + g * nh + i for g in range(ng) for kind in range(2) for z in range(2)
           for i in range(nh)]
    n_main = _O_QKV + _O_G
    win = jnp.concatenate([w_in[:, :n_main], w_in[:, n_main:][:, jnp.array(src)],
                           jnp.zeros((d, _O_N - w_in.shape[1]), F32)], axis=1).astype(BF16)
    e = (jnp.arange(_O_QK)[:, None] // GDN_DK == jnp.arange(LANES)[None, :]).astype(BF16)
    et = e.T
    rs = jnp.concatenate([jnp.full((GDN_HEADS,), GDN_DK ** -0.5, F32), jnp.ones((LANES - GDN_HEADS,), F32)])[None, :]
    la_src = jnp.array([s if s < 2 * hh_ else 2 * hh_ for s in src] + [2 * hh_] * (LANES - len(src)))
    padl = lambda v: jnp.concatenate([v.reshape(-1), jnp.zeros((1,), F32)])[la_src][None, :]
    al, dt = padl(a_log), padl(dt_bias)
    sel = jnp.eye(ng * gw, LANES, dtype=BF16)
    tok = lambda n: pl.BlockSpec((None, tm, n), lambda i, j: (i, j, 0))
    hb = tm // HALO
    outs = [GDN_HEADS * GDN_DK, GDN_HEADS * GDN_DK, GDN_HEADS * GDN_DV, _O_G]
    return pl.pallas_call(
        functools.partial(_odd_proj_kernel, nlb=nlb, nb=nb),
        grid=(b, nb),
        in_specs=[tok(d),
                  pl.BlockSpec((None, HALO, d), lambda i, j: (i, jnp.maximum(j * hb - 1, 0), 0)),
                  pl.BlockSpec((None, HALO, d), lambda i, j: (i, jnp.minimum((j + 1) * hb, t // HALO - 1), 0)),
                  pl.BlockSpec((None, None, N_MOD, d), lambda i, j: (i, j // nlb, 0, 0)),
                  _full(n1.shape), _full(win.shape), _full(conv_w.shape), _full(e.shape), _full(et.shape),
                  _full(rs.shape), _full(al.shape), _full(dt.shape), _full(sel.shape)],
        out_specs=[tok(n) for n in outs] + [
            pl.BlockSpec((None, ng, tm, gw), lambda i, j: (i, 0, j, 0)),
            pl.BlockSpec((None, ng, gw, tm), lambda i, j: (i, 0, 0, j))],
        out_shape=[jax.ShapeDtypeStruct((b, t, n), F32) for n in outs] + [
            jax.ShapeDtypeStruct((b, ng, t, gw), F32), jax.ShapeDtypeStruct((b, ng, gw, t), F32)],
        compiler_params=_cparams(("parallel", "parallel")),
        name="odd_proj",
    )(h, h, h, mod, n1, win, conv_w, e, et, rs, al, dt, sel)


def _gdn_kernel(qf, kf, vf, gcf, grf, qb, kb, vb, gcb, grb, of, ob, s_ref):
    @pl.when(pl.program_id(2) == 0)
    def _():
        s_ref[...] = jnp.zeros_like(s_ref)

    blk = qf.shape[0]
    nc = blk // CHUNK
    nh = qf.shape[1] // GDN_DK
    i32 = jnp.int32
    ri = lax.broadcasted_iota(i32, (CHUNK, CHUNK), 0)
    ci = lax.broadcasted_iota(i32, (CHUNK, CHUNK), 1)
    same = [(ri // w) == (ci // w) for w in (2, 4, 8, 16, 32)]
    lvl = [same[0]] + [jnp.logical_and(same[i], jnp.logical_not(same[i - 1])) for i in range(1, 5)]
    lvl.append(jnp.logical_not(same[4]))
    eye = jnp.where(ri == ci, 1.0, 0.0).astype(F32)
    incl = (ri >= ci, ri <= ci)
    strict = (ri > ci, ri < ci)
    bi = lax.broadcasted_iota(i32, (blk, blk), 0)
    bj = lax.broadcasted_iota(i32, (blk, blk), 1)
    same_chunk = (bi // CHUNK) == (bj // CHUNK)

    zs, cs, hs, qs, ks, vs, gbs, bbs, grows, brows, gls = ([] for _ in range(11))
    sq = (CHUNK, CHUNK)
    for z, (q, k, v, gc, gr) in enumerate(((qf, kf, vf, gcf, grf), (qb, kb, vb, gcb, grb))):
        tri = jnp.where(jnp.logical_and(same_chunk, (bi <= bj) if z else (bi >= bj)), 1.0, 0.0).astype(BF16)
        gcv, grv = gc[...], gr[...]
        cum_col = _dot_exact_lhs(tri, gcv)
        cum_row = _dot_exact_lhs(tri, grv, nt=True)
        for c in range(nc):
            r = slice(c * CHUNK, (c + 1) * CHUNK)
            last = c * CHUNK if z else (c + 1) * CHUNK - 1
            for hh in range(nh):
                idx = nh * z + hh
                zs.append(z), cs.append(c), hs.append(hh)
                qs.append(q[r, hh * GDN_DK:(hh + 1) * GDN_DK])
                ks.append(k[r, hh * GDN_DK:(hh + 1) * GDN_DK])
                vs.append(v[r, hh * GDN_DV:(hh + 1) * GDN_DV])
                gbs.append(jnp.broadcast_to(cum_col[r, idx:idx + 1], sq))
                bbs.append(jnp.broadcast_to(gcv[r, 2 * nh + idx:2 * nh + idx + 1], sq))
                grows.append(cum_row[idx:idx + 1, r])
                brows.append(grv[2 * nh + idx:2 * nh + idx + 1, r])
                gls.append(cum_col[last:last + 1, idx:idx + 1])

    decays = _stage(lambda z, gb, gr_: jnp.where(incl[z], jnp.exp(jnp.where(incl[z], gb - gr_, 0.0)), 0.0),
                    zs, gbs, grows)
    qkk = _stage(lambda q, k: _dot_nt(jnp.concatenate([q, k], axis=0), k), qs, ks)
    ns = _stage(lambda z, x, bb, d_: jnp.where(strict[z], x[CHUNK:] * bb * d_, 0.0), zs, qkk, bbs, decays)
    ts = _stage(lambda n: eye - jnp.where(lvl[0], n, 0.0), ns)
    for off in lvl[1:]:
        xs = _stage(lambda t, n: _dot(t, jnp.where(off, n, 0.0)), ts, ns)
        ts = _stage(lambda t, x: t - _dot(x, t), ts, xs)
    egs = _stage(jnp.exp, gbs)
    uws = _stage(lambda t, br, v, k, e_: _dot(t * br, jnp.concatenate([v, k * e_], axis=-1)),
                 ts, brows, vs, ks, egs)
    qks = _stage(lambda x, d_: x[:CHUNK] * d_, qkk, decays)
    kes = _stage(lambda k, gl, gb: k * jnp.exp(gl - gb), ks, gls, gbs)
    kuws = _stage(_dot_tn, kes, uws)
    quws = _stage(_dot, qks, uws)
    lhs = _stage(lambda q, e_, qu, ku: jnp.concatenate([q * e_ - qu[:, GDN_DV:], -ku[:, GDN_DV:]], axis=0),
                 qs, egs, quws, kuws)
    ges = _stage(jnp.exp, gls)

    chain = {(z, c, hh): i for i, (z, c, hh) in enumerate(zip(zs, cs, hs))}
    state = {(z, hh): s_ref[z, hh] for z in range(2) for hh in range(nh)}
    outs = (of, ob)
    for step in range(nc):
        for z in range(2):
            c = nc - 1 - step if z else step
            for hh in range(nh):
                i = chain[(z, c, hh)]
                s = state[(z, hh)]
                r = _dot(lhs[i], s)
                outs[z][c * CHUNK:(c + 1) * CHUNK, hh * GDN_DV:(hh + 1) * GDN_DV] = (
                    r[:CHUNK] + quws[i][:, :GDN_DV])
                state[(z, hh)] = s * ges[i] + (kuws[i][:, :GDN_DV] + r[CHUNK:])
    for (z, hh), s in state.items():
        s_ref[z, hh] = s


def _gdn_scan(q, k, v, gcol, grow, n_lat, n_ctx):
    b, t, _ = q.shape
    nh = GDN_SCAN_HEADS
    ng = GDN_HEADS // nh
    blk = _scan_blk(n_lat, n_ctx)
    nlb, ncb = n_lat // blk, n_ctx // blk
    fwd, bwd = _scan_maps(nlb, ncb)
    d = None
    tok = lambda n, m: pl.BlockSpec((d, blk, n), lambda i, h, j: (i, m(j), h))
    gcs = lambda m: pl.BlockSpec((d, d, blk, 4 * nh), lambda i, h, j: (i, h, m(j), 0))
    grs = lambda m: pl.BlockSpec((d, d, 4 * nh, blk), lambda i, h, j: (i, h, 0, m(j)))
    out = jax.ShapeDtypeStruct((b, t, GDN_HEADS * GDN_DV), F32)
    dk2, dv2 = nh * GDN_DK, nh * GDN_DV
    return pl.pallas_call(
        _gdn_kernel,
        grid=(b, ng, nlb + ncb),
        in_specs=[tok(dk2, fwd), tok(dk2, fwd), tok(dv2, fwd), gcs(fwd), grs(fwd),
                  tok(dk2, bwd), tok(dk2, bwd), tok(dv2, bwd), gcs(bwd), grs(bwd)],
        out_specs=[tok(dv2, fwd), tok(dv2, bwd)],
        out_shape=[out, out],
        scratch_shapes=[pltpu.VMEM((2, nh, GDN_DK, GDN_DV), F32)],
        compiler_params=_cparams(("parallel", "parallel", "arbitrary")),
        name="gdn_scan",
    )(q, k, v, gcol, grow, q, k, v, gcol, grow)


def kernel(x, c, ctx, c_ctx, ada_w, ada_b, norm1_w, norm2_w, mlp_w1, mlp_w2, even_w_in, mla_q_norm,
           mla_w_uq, mla_kv_norm, mla_w_ukv, gla_gate_up, gla_gate_bias, gla_o_norm, even_w_out,
           gdn_w_in, gdn_conv_w, gdn_a_log, gdn_dt_bias, gdn_o_norm, gdn_w_out, final_norm):
    b, n_lat, d = x.shape
    n_ctx = ctx.shape[1]
    depth = ada_w.shape[0]
    tm = 256 if (n_lat % 256 == 0 and n_ctx % 256 == 0) else 128
    assert n_lat % tm == 0 and n_ctx % tm == 0 and n_lat % n_ctx == 0 and n_lat % GRID_W == 0

    rows = -(-(b + 1) // 8) * 8
    cc = jnp.concatenate([c, c_ctx[None, :], jnp.zeros((rows - b - 1, d), F32)], axis=0)
    mods = _ada_all(cc, ada_w, ada_b).reshape(depth, rows, N_MOD, d)
    mod_tok = jnp.stack([mods[:, :b], jnp.broadcast_to(mods[:, b:b + 1], (depth, b, N_MOD, d))], axis=2)

    stream = (x, ctx)
    tabs = _rope_tables(n_lat, n_ctx)
    w1_bf, w2_bf = _to_bf16(mlp_w1), _to_bf16(mlp_w2)
    row = lambda v: v[None, :]
    for layer in range(depth):
        need_ctx = layer < depth - 1
        n_tok = n_lat + n_ctx if need_ctx else n_lat
        mod = mod_tok[layer]
        n1 = row(norm1_w[layer])
        i = layer // 2
        if layer % 2 == 0:
            ew = _even_weights(even_w_in[i], mla_w_uq[i], mla_w_ukv[i], gla_gate_up[i], gla_gate_bias[i])
            q, k, v, gq, gk, gv, gg, lg = _even_project(
                stream, mod, n1, ew, row(mla_q_norm[i]), row(mla_kv_norm[i]), tabs, tm, n_lat)
            atts = [a for a in _attention(q, k, v, n_lat, n_ctx, need_ctx) if a is not None]
            o_f, o_b = _gla_scan(gq, gk, gv, lg, n_lat, n_ctx)
            parts, on, wo = [o_f, o_b, gg], gla_o_norm[i], even_w_out[i]
        else:
            q, k, v, g, gcol, grow = _odd_project(stream[0], mod, n1, gdn_w_in[i], gdn_conv_w[i], gdn_a_log[i],
                                                  gdn_dt_bias[i], tm, n_lat)
            o_f, o_b = _gdn_scan(q, k, v, gcol, grow, n_lat, n_ctx)
            atts, parts, on, wo = [], [o_f, o_b, g], gdn_o_norm[i], gdn_w_out[i]
        stream = (_merge_mlp(atts, parts, row(on), wo.astype(BF16), stream, mod, row(norm2_w[layer]), w1_bf, w2_bf,
                             layer, tm, n_lat, n_tok, final_norm=None if need_ctx else row(final_norm)),)
    return stream[0]
```

```python
import functools

import jax
import jax.numpy as jnp
from jax import lax
from jax.experimental import pallas as pl
from jax.experimental.pallas import tpu as pltpu

F32 = jnp.float32
BF16 = jnp.bfloat16
EPS = 1e-6
LOG2E = 1.4426950408889634

GRID_W = 64
N_MOD = 6
MLA_HEADS = 8
MLA_NOPE = 64
MLA_ROPE = 32
MLA_V = 64
MLA_QK = MLA_NOPE + MLA_ROPE
MLA_Q_RANK = 384
MLA_KV_RANK = 256
ROPE_THETA = 10000.0
GLA_HEADS = 4
GLA_DK = 64
GLA_DV = 128
GLA_GATE_RANK = 16
GLA_GATE_NORM = 16.0
GDN_HEADS = 8
GDN_DK = 64
GDN_DV = 128
CHUNK = 64
MLP_HIDDEN_CHUNK = 1024
GDN_SCAN_HEADS = 4
LANES = 128
HEAD_SLOT = 128
V_SLOT = 80
VMEM_LIMIT = 56 * 1024 * 1024


def _cparams(sem):
    return pltpu.CompilerParams(dimension_semantics=sem, vmem_limit_bytes=VMEM_LIMIT)


def _dot(a, b):
    return jnp.dot(a.astype(BF16), b.astype(BF16), preferred_element_type=F32)


def _dot_nt(a, b):
    return lax.dot_general(a.astype(BF16), b.astype(BF16), (((1,), (1,)), ((), ())),
                           preferred_element_type=F32)


def _dot_tn(a, b):
    return lax.dot_general(a.astype(BF16), b.astype(BF16), (((0,), (0,)), ((), ())),
                           preferred_element_type=F32)


def _split(x, n):
    out = []
    for _ in range(n - 1):
        p = x.astype(BF16)
        out.append(p)
        x = x - p.astype(F32)
    out.append(x.astype(BF16))
    return out


def _dot3(a, b):
    a1, a2 = _split(a, 2)
    b1, b2 = _split(b, 2)
    d = functools.partial(jnp.dot, preferred_element_type=F32)
    return d(a1, b1) + (d(a1, b2) + d(a2, b1))


def _dot_exact_lhs(m_bf, x, nt=False):
    acc = None
    for p in _split(x, 2):
        if nt:
            t = lax.dot_general(p, m_bf, (((1,), (1,)), ((), ())), preferred_element_type=F32)
        else:
            t = jnp.dot(m_bf, p, preferred_element_type=F32)
        acc = t if acc is None else acc + t
    return acc


def _rms(x, w):
    return x * lax.rsqrt(jnp.mean(x * x, axis=-1, keepdims=True) + EPS) * w


def _sigmoid(x):
    return 1.0 / (1.0 + jnp.exp(-x))


def _silu(x):
    return x * _sigmoid(x)


def _softplus(x):
    return jnp.maximum(x, 0.0) + jnp.log1p(jnp.exp(-jnp.abs(x)))


def _tile_lanes(x, n):
    return jnp.concatenate([x] * n, axis=-1)


def _stream_specs(stream, tm, nlb):
    d = stream[0].shape[-1]
    if len(stream) == 1:
        return [pl.BlockSpec((None, tm, d), lambda i, j: (i, j, 0))]
    return [pl.BlockSpec((None, tm, d), lambda i, j: (i, jnp.minimum(j, nlb - 1), 0)),
            pl.BlockSpec((None, tm, d), lambda i, j: (i, jnp.maximum(j - nlb, 0), 0))]


def _stream_block(refs, nlb):
    if len(refs) == 1:
        return refs[0][...]
    return jnp.where(pl.program_id(1) >= nlb, refs[1][...], refs[0][...])


def _stage(fn, *cols):
    return [fn(*a) for a in zip(*cols)]


def _ada_kernel(cc_ref, w_ref, b_ref, o_ref):
    o_ref[0] = _dot(_silu(cc_ref[...]), w_ref[0]) + b_ref[0]


def _ada_all(cc, ada_w, ada_b):
    depth, d, n = ada_w.shape
    rows = cc.shape[0]
    tn = 1024
    return pl.pallas_call(
        _ada_kernel,
        grid=(depth, n // tn),
        in_specs=[pl.BlockSpec((rows, d), lambda l, j: (0, 0)),
                  pl.BlockSpec((1, d, tn), lambda l, j: (l, 0, j)),
                  pl.BlockSpec((1, 1, tn), lambda l, j: (l, 0, j))],
        out_specs=pl.BlockSpec((1, rows, tn), lambda l, j: (l, 0, j)),
        out_shape=jax.ShapeDtypeStruct((depth, rows, n), F32),
        compiler_params=_cparams(("parallel", "parallel")),
        name="ada_mod",
    )(cc, ada_w, ada_b.reshape(depth, 1, n))


_E_CQ = (0, 384)
_E_CKV = (384, 640)
_E_ROT = (640, 768)
_E_GQ = (768, 1024)
_E_GK = (1024, 1280)
_E_GV = (1280, 1792)
_E_GG = (1792, 2304)
_E_N = 2304


def _even_proj_kernel(*refs, n_stream, nlb):
    (mod_ref, n1_ref, win_ref, qn_ref, wq_ref, kvn_ref, wk_ref, wv_ref, wg_ref, gb_ref, c_ref, s_ref,
     ctt_ref, stt_ref, q_out, k_out, v_out, gq_out, gk_out, gv_out, gg_out, lg_out) = refs[n_stream:]
    mod = mod_ref[...]
    a = _rms(_stream_block(refs[:n_stream], nlb), n1_ref[...]) * (1.0 + mod[1:2]) + mod[0:1]
    y = _dot(a, win_ref[...])
    sl = lambda r: y[:, r[0]:r[1]]
    ct, st = c_ref[...], s_ref[...]
    nh = MLA_HEADS
    qq = _dot_nt(wq_ref[...], _rms(sl(_E_CQ), qn_ref[...]))
    c32, s32 = ctt_ref[...], stt_ref[...]
    rot0 = nh * MLA_QK
    zero_pad = jnp.zeros((HEAD_SLOT - MLA_QK, qq.shape[1]), F32)
    pieces = []
    for h in range(nh):
        nope = qq[h * MLA_QK:h * MLA_QK + MLA_NOPE]
        pe = qq[h * MLA_QK + MLA_NOPE:(h + 1) * MLA_QK]
        pe_rot = qq[rot0 + h * MLA_ROPE:rot0 + (h + 1) * MLA_ROPE]
        pieces += [nope, pe * c32 + pe_rot * s32, zero_pad]
    q_out[...] = jnp.concatenate(pieces, axis=0).astype(q_out.dtype)
    ckvn = _rms(sl(_E_CKV), kvn_ref[...])
    yr = sl(_E_ROT)
    kpe = pltpu.roll(yr, MLA_NOPE, axis=1) * ct + pltpu.roll(yr, MLA_NOPE - MLA_ROPE, axis=1) * st
    k_out[...] = (_dot(ckvn, wk_ref[...]) + _tile_lanes(kpe, nh)).astype(k_out.dtype)
    vt = _dot_nt(wv_ref[...], ckvn)
    pad_rows = lax.broadcasted_iota(jnp.int32, (V_SLOT - MLA_V, vt.shape[1]), 0)
    ones_row = jnp.where(pad_rows == 0, 1.0, 0.0)
    v_out[...] = jnp.concatenate(
        [x for h in range(nh) for x in (vt[h * MLA_V:(h + 1) * MLA_V], ones_row)], axis=0).astype(v_out.dtype)
    gq_out[...] = sl(_E_GQ)
    gk_out[...] = sl(_E_GK)
    gv_out[...] = sl(_E_GV)
    gg_out[...] = sl(_E_GG)
    logit = _dot(yr, wg_ref[...]) + gb_ref[...]
    lg_out[...] = (jnp.minimum(logit, 0.0) - jnp.log1p(jnp.exp(-jnp.abs(logit)))) * (1.0 / GLA_GATE_NORM)


def _rot_half_cols(w):
    f = MLA_ROPE // 4
    return jnp.concatenate([-w[..., f:2 * f], w[..., 0:f], -w[..., 3 * f:4 * f], w[..., 2 * f:3 * f]], axis=-1)


def _even_weights(w_in, w_uq, w_ukv, gate_up, gate_bias):
    d = w_in.shape[0]
    cq, ckv, kpe, gq, gk, gv, gg, glow = jnp.split(
        w_in, [384, 640, 672, 928, 1184, 1696, 2208], axis=1)
    lowr = 2 * GLA_GATE_RANK
    win = jnp.concatenate([cq, ckv, kpe, _rot_half_cols(kpe), glow, jnp.zeros((d, LANES - 2 * MLA_ROPE - lowr), F32),
                           gq, gk, gv, gg], axis=1).astype(BF16)
    r = w_uq.shape[0]
    wq = (w_uq * (MLA_QK ** -0.5 * LOG2E)).reshape(r, MLA_HEADS, MLA_QK)
    wq_rot = _rot_half_cols(wq[..., MLA_NOPE:])
    wq_all = jnp.concatenate([wq.reshape(r, -1), wq_rot.reshape(r, -1)], axis=1).T.astype(BF16)
    rk = w_ukv.shape[0]
    wkv = w_ukv.reshape(rk, MLA_HEADS, MLA_NOPE + MLA_V)
    wk = jnp.concatenate([wkv[..., :MLA_NOPE], jnp.zeros((rk, MLA_HEADS, HEAD_SLOT - MLA_NOPE), F32)],
                         axis=-1).reshape(rk, -1).astype(BF16)
    wv = wkv[..., MLA_NOPE:].reshape(rk, -1).T.astype(BF16)
    hk = GLA_HEADS * GLA_DK
    g0 = 2 * MLA_ROPE
    wg = jnp.zeros((LANES, 2 * hk), F32)
    wg = wg.at[g0:g0 + GLA_GATE_RANK, 0:hk].set(gate_up[0])
    wg = wg.at[g0 + GLA_GATE_RANK:g0 + 2 * GLA_GATE_RANK, hk:].set(gate_up[1])
    gb = jnp.concatenate([gate_bias[0], gate_bias[1]])[None, :]
    return win, wq_all, wk, wv, wg.astype(BF16), gb


def _rope_tables(n_lat, n_ctx):
    rows = n_lat // GRID_W
    row = jnp.repeat(jnp.arange(rows, dtype=F32), GRID_W)
    col = jnp.tile(jnp.arange(GRID_W, dtype=F32), rows)
    axis_dim = MLA_ROPE // 2
    inv_freq = ROPE_THETA ** (-jnp.arange(0, axis_dim, 2, dtype=F32) / axis_dim)
    ar, ac = row[:, None] * inv_freq, col[:, None] * inv_freq
    c32 = jnp.concatenate([jnp.cos(ar), jnp.cos(ar), jnp.cos(ac), jnp.cos(ac)], axis=1)
    s32 = jnp.concatenate([jnp.sin(ar), jnp.sin(ar), jnp.sin(ac), jnp.sin(ac)], axis=1)
    c32 = jnp.concatenate([c32, jnp.ones((n_ctx, MLA_ROPE), F32)], axis=0)
    s32 = jnp.concatenate([s32, jnp.zeros((n_ctx, MLA_ROPE), F32)], axis=0)
    slot = lambda x: jnp.pad(x, ((0, 0), (MLA_NOPE, HEAD_SLOT - MLA_QK)))
    return slot(c32), slot(s32), c32.T, s32.T


def _full(shape):
    nd = len(shape)
    return pl.BlockSpec(shape, lambda *_: (0,) * nd)


def _even_project(stream, mod, n1, ew, qn, kvn, tabs, tm, n_lat):
    b, d = stream[0].shape[0], stream[0].shape[-1]
    t = sum(s.shape[1] for s in stream)
    win, wq, wk, wv, wg, gb = ew
    ct, st, ctt, stt = tabs
    nlb = n_lat // tm
    tok = lambda n: pl.BlockSpec((None, tm, n), lambda i, j: (i, j, 0))
    tab = pl.BlockSpec((tm, HEAD_SLOT), lambda i, j: (j, 0))
    tabt = pl.BlockSpec((MLA_ROPE, tm), lambda i, j: (0, j))
    hk, hv = GLA_HEADS * GLA_DK, GLA_HEADS * GLA_DV
    feat = lambda n: pl.BlockSpec((None, n, tm), lambda i, j: (i, 0, j))
    tshape = lambda n, dt: jax.ShapeDtypeStruct((b, t, n), dt)
    fshape = lambda n, dt: jax.ShapeDtypeStruct((b, n, t), dt)
    nq, nv = MLA_HEADS * HEAD_SLOT, MLA_HEADS * V_SLOT
    return pl.pallas_call(
        functools.partial(_even_proj_kernel, n_stream=len(stream), nlb=nlb),
        grid=(b, t // tm),
        in_specs=_stream_specs(stream, tm, nlb) + [
                  pl.BlockSpec((None, None, N_MOD, d), lambda i, j: (i, j // nlb, 0, 0)),
                  _full(n1.shape), _full(win.shape), _full(qn.shape), _full(wq.shape), _full(kvn.shape),
                  _full(wk.shape), _full(wv.shape), _full(wg.shape), _full(gb.shape), tab, tab, tabt, tabt],
        out_specs=[feat(nq), tok(nq), feat(nv), tok(hk), tok(hk), tok(hv), tok(hv), tok(2 * hk)],
        out_shape=[fshape(nq, BF16), tshape(nq, BF16), fshape(nv, BF16), tshape(hk, F32), tshape(hk, F32),
                   tshape(hv, F32), tshape(hv, F32), tshape(2 * hk, F32)],
        compiler_params=_cparams(("parallel", "parallel")),
        name="even_proj",
    )(*stream, mod, n1, win, qn, wq, kvn, wk, wv, wg, gb, ct, st, ctt, stt)


def _attn_kernel(qt_ref, k_ref, vt_ref, o_ref):
    tk_all = k_ref.shape[0]
    tk = 256 if tk_all % 256 == 0 else 128
    heads = range(qt_ref.shape[0] // HEAD_SLOT)
    nkb = tk_all // tk
    qts = [qt_ref[hh * HEAD_SLOT:(hh + 1) * HEAD_SLOT, :] for hh in heads]

    def scores(kb):
        return [jnp.dot(k_ref[kb * tk:(kb + 1) * tk, hh * HEAD_SLOT:(hh + 1) * HEAD_SLOT], qts[hh],
                        preferred_element_type=F32) for hh in heads]

    s_next = scores(0)
    m = acc = None
    for kb in range(nkb):
        s = s_next
        if kb + 1 < nkb:
            s_next = scores(kb + 1)
        bm = [jnp.max(x, axis=0, keepdims=True) for x in s]
        if m is None:
            m_new = bm
        else:
            m_new = [jnp.maximum(a, b_) for a, b_ in zip(m, bm)]
        p = [jnp.exp2(x - mn) for x, mn in zip(s, m_new)]
        pv = [jnp.dot(vt_ref[hh * V_SLOT:(hh + 1) * V_SLOT, kb * tk:(kb + 1) * tk], p[hh].astype(BF16),
                      preferred_element_type=F32) for hh in heads]
        if m is None:
            acc = pv
        else:
            acc = [jnp.exp2(a - b_) * x + y for a, b_, x, y in zip(m, m_new, acc, pv)]
        m = m_new
    o_ref[...] = jnp.concatenate([(a[:MLA_V] / a[MLA_V:MLA_V + 1]).T for a in acc], axis=-1).astype(o_ref.dtype)


def _attention(qt, k, vt, n_lat, n_ctx, need_ctx):
    b, t, _ = k.shape
    hp = MLA_HEADS // 2
    tq = 1024 if n_lat % 1024 == 0 else 128
    nv = MLA_HEADS * MLA_V
    att = pl.pallas_call(
        _attn_kernel,
        grid=(b, hp, n_lat // tq),
        in_specs=[pl.BlockSpec((None, 2 * HEAD_SLOT, tq), lambda i, h, j: (i, h, j)),
                  pl.BlockSpec((None, t, 2 * HEAD_SLOT), lambda i, h, j: (i, 0, h)),
                  pl.BlockSpec((None, 2 * V_SLOT, t), lambda i, h, j: (i, h, 0))],
        out_specs=pl.BlockSpec((None, tq, 2 * MLA_V), lambda i, h, j: (i, j, h)),
        out_shape=jax.ShapeDtypeStruct((b, n_lat, nv), BF16),
        compiler_params=_cparams(("parallel", "parallel", "arbitrary")),
        name="mla_attn_lat",
    )(qt, k, vt)
    if not need_ctx:
        return att, None
    cb = t // n_ctx - 1
    att_ctx = pl.pallas_call(
        _attn_kernel,
        grid=(b,),
        in_specs=[pl.BlockSpec((None, MLA_HEADS * HEAD_SLOT, n_ctx), lambda i: (i, 0, cb)),
                  pl.BlockSpec((None, n_ctx, MLA_HEADS * HEAD_SLOT), lambda i: (i, cb, 0)),
                  pl.BlockSpec((None, MLA_HEADS * V_SLOT, n_ctx), lambda i: (i, 0, cb))],
        out_specs=pl.BlockSpec((None, n_ctx, nv), lambda i: (i, 0, 0)),
        out_shape=jax.ShapeDtypeStruct((b, n_ctx, nv), BF16),
        compiler_params=_cparams(("parallel",)),
        name="mla_attn_ctx",
    )(qt, k, vt)
    return att, att_ctx


def _scan_maps(nlb, ncb):
    fwd = lambda j: jnp.where(j < ncb, nlb + j, j - ncb)
    bwd = lambda j: nlb + ncb - 1 - j
    return fwd, bwd


def _scan_blk(n_lat, n_ctx):
    for blk in (256, 128, 64):
        if n_lat % blk == 0 and n_ctx % blk == 0:
            return blk
    raise ValueError("sequence lengths must be multiples of the 64-token chunk")


def _gla_kernel(qf, kf, vf, lf, qb, kb, vb, lb, of, ob, sf, sb):
    @pl.when(pl.program_id(1) == 0)
    def _():
        sf[...] = jnp.zeros_like(sf)
        sb[...] = jnp.zeros_like(sb)

    hk, hv = GLA_HEADS * GLA_DK, GLA_HEADS * GLA_DV
    i32 = jnp.int32
    io = lambda shape, d: lax.broadcasted_iota(i32, shape, d)
    blk = qf.shape[0]
    nc = blk // CHUNK
    aj = io((CHUNK, GLA_HEADS * CHUNK), 1) % CHUNK
    ai = io((CHUNK, GLA_HEADS * CHUNK), 0)
    a_incl = (ai >= aj, ai <= aj)
    klane = io((CHUNK, hk), 1) // GLA_DK
    bi, bj = io((blk, blk), 0), io((blk, blk), 1)
    same_chunk = (bi // CHUNK) == (bj // CHUNK)

    zs, cs, qs, ks, vs, bcs, bts = ([] for _ in range(7))
    for z, (q, k, v, lg) in enumerate(((qf, kf, vf, lf), (qb, kb, vb, lb))):
        tri = jnp.where(jnp.logical_and(same_chunk, (bi <= bj) if z else (bi >= bj)), 1.0, 0.0).astype(BF16)
        bcum = _dot_exact_lhs(tri, lg[:, z * hk:(z + 1) * hk])
        for c in range(nc):
            r = slice(c * CHUNK, (c + 1) * CHUNK)
            last = c * CHUNK if z else (c + 1) * CHUNK - 1
            zs.append(z), cs.append(c)
            qs.append(q[r, :]), ks.append(k[r, :]), vs.append(v[r, :])
            bcs.append(bcum[r, :]), bts.append(bcum[last:last + 1, :])

    heads = range(GLA_HEADS)
    hd = lambda x, h, w: x[:, h * w:(h + 1) * w]
    zero_k = jnp.zeros((CHUNK, hk), BF16)
    zero_v = jnp.zeros((CHUNK, GLA_DV), BF16)
    qds = _stage(lambda q, b_: q * (GLA_DK ** -0.5) * jnp.exp(b_), qs, bcs)
    kinv = _stage(lambda k, b_: (k * jnp.exp(-b_)).astype(BF16), ks, bcs)
    kbs = _stage(lambda x: jnp.concatenate([jnp.where(klane == h, x, zero_k) for h in heads], axis=0), kinv)
    a_s = _stage(lambda z, qd, kb_: jnp.where(a_incl[z], _dot_nt(qd, kb_), 0.0), zs, qds, kbs)
    vbs = _stage(lambda v: jnp.concatenate(
        [jnp.concatenate([hd(v, j, GLA_DV).astype(BF16) if j == h else zero_v for j in heads], axis=1)
         for h in heads], axis=0), vs)
    ois = _stage(_dot, a_s, vbs)
    kes = _stage(lambda k, b_, bt: k * jnp.exp(bt - b_), ks, bcs, bts)
    upds = _stage(lambda v, ke: [_dot_tn(hd(v, h, GLA_DV), hd(ke, h, GLA_DK)) for h in heads], vs, kes)
    ges = _stage(jnp.exp, bts)

    chain = {(z, c): i for i, (z, c) in enumerate(zip(zs, cs))}
    state = [[s_[h] for h in heads] for s_ in (sf, sb)]
    outs = (of, ob)
    for step in range(nc):
        for z in range(2):
            c = nc - 1 - step if z else step
            i = chain[(z, c)]
            inter = [_dot_nt(hd(qds[i], h, GLA_DK), state[z][h]) for h in heads]
            outs[z][c * CHUNK:(c + 1) * CHUNK, :] = ois[i] + jnp.concatenate(inter, axis=1)
            state[z] = [state[z][h] * hd(ges[i], h, GLA_DK) + upds[i][h] for h in heads]
    for s_, st in zip((sf, sb), state):
        for h in heads:
            s_[h] = st[h]


def _gla_scan(gq, gk, gv, lg, n_lat, n_ctx):
    b, t, hk = gq.shape
    hv = gv.shape[-1]
    blk = _scan_blk(n_lat, n_ctx)
    nlb, ncb = n_lat // blk, n_ctx // blk
    fwd, bwd = _scan_maps(nlb, ncb)
    spec = lambda n, m: pl.BlockSpec((None, blk, n), lambda i, j: (i, m(j), 0))
    out = jax.ShapeDtypeStruct((b, t, hv), F32)
    return pl.pallas_call(
        _gla_kernel,
        grid=(b, nlb + ncb),
        in_specs=[spec(hk, fwd), spec(hk, fwd), spec(hv, fwd), spec(2 * hk, fwd),
                  spec(hk, bwd), spec(hk, bwd), spec(hv, bwd), spec(2 * hk, bwd)],
        out_specs=[spec(hv, fwd), spec(hv, bwd)],
        out_shape=[out, out],
        scratch_shapes=[pltpu.VMEM((GLA_HEADS, GLA_DV, GLA_DK), F32)] * 2,
        compiler_params=_cparams(("parallel", "arbitrary")),
        name="gla_scan",
    )(gq, gk, gv, lg, gq, gk, gv, lg)


def _cast_kernel(x_ref, o_ref):
    o_ref[...] = x_ref[...].astype(o_ref.dtype)


def _to_bf16(w):
    depth, r, c = w.shape
    tr = 512 if r % 512 == 0 else r
    spec = pl.BlockSpec((None, tr, c), lambda l, i: (l, i, 0))
    return pl.pallas_call(
        _cast_kernel,
        grid=(depth, r // tr),
        in_specs=[spec],
        out_specs=spec,
        out_shape=jax.ShapeDtypeStruct(w.shape, BF16),
        compiler_params=_cparams(("parallel", "parallel")),
        name="cast_bf16",
    )(w)


def _head_norm_gate(of_ref, ob_ref, g_ref, on_ref, n_heads):
    x = of_ref[...] + ob_ref[...]
    g = g_ref[...]
    outs = []
    for h in range(n_heads):
        s = slice(h * LANES, (h + 1) * LANES)
        outs.append(_rms(x[:, s], on_ref[...]) * _silu(g[:, s]))
    return jnp.concatenate(outs, axis=-1)


def _merge_mlp_kernel(*refs, n_att, n_stream, nlb, final):
    atts, parts = refs[:n_att], refs[n_att:n_att + 3]
    n_parts = n_att + 3 + n_stream
    stream = refs[n_att + 3:n_parts]
    on_ref, wo_ref, mod_ref, n2_ref, w1_ref, w2_ref = refs[n_parts:n_parts + 6]
    o_ref = refs[-1]
    mod = mod_ref[...]
    if n_att:
        att = atts[0][...]
        if n_att == 2:
            att = jnp.where(pl.program_id(1) >= nlb, atts[1][...], att)
        na = att.shape[-1]
        y = _head_norm_gate(*parts, on_ref, GLA_HEADS)
        o = jnp.dot(att, wo_ref[0:na, :], preferred_element_type=F32) + _dot(y, wo_ref[na:, :])
    else:
        o = _dot(_head_norm_gate(*parts, on_ref, GDN_HEADS), wo_ref[...])
    h = _stream_block(stream, nlb) + mod[2:3] * o
    a = (_rms(h, n2_ref[...]) * (1.0 + mod[4:5]) + mod[3:4]).astype(BF16)
    ff = w1_ref.shape[1]
    hc = min(ff, MLP_HIDDEN_CHUNK)
    acc = None
    for c in range(ff // hc):
        u = jnp.maximum(jnp.dot(a, w1_ref[:, c * hc:(c + 1) * hc], preferred_element_type=F32), 0.0)
        t = _dot(u * u, w2_ref[c * hc:(c + 1) * hc, :])
        acc = t if acc is None else acc + t
    out = h + mod[5:6] * acc
    if final:
        out = _rms(out, refs[n_parts + 6][...])
    o_ref[...] = out


def _merge_mlp(atts, parts, on, wo, stream, mod, n2, w1, w2, layer, tm, n_lat, n_tok, final_norm=None):
    b, d = stream[0].shape[0], stream[0].shape[-1]
    nlb = n_lat // tm
    tok = lambda n: pl.BlockSpec((None, tm, n), lambda i, j: (i, j, 0))
    once = lambda a: pl.BlockSpec(a.shape, lambda *_: (0,) * a.ndim, pipeline_mode=pl.Buffered(1))
    slab = lambda a: pl.BlockSpec((None,) + a.shape[1:], lambda *_: (layer, 0, 0), pipeline_mode=pl.Buffered(1))
    att_maps = (lambda i, j: (i, jnp.minimum(j, nlb - 1), 0), lambda i, j: (i, jnp.maximum(j - nlb, 0), 0))
    args = [*atts, *parts, *stream, on, wo, mod, n2, w1, w2]
    specs = [pl.BlockSpec((None, tm, a.shape[-1]), m) for a, m in zip(atts, att_maps)] + [
        tok(p.shape[-1]) for p in parts] + _stream_specs(stream, tm, nlb) + [
        _full(on.shape), once(wo),
        pl.BlockSpec((None, None, N_MOD, d), lambda i, j: (i, j // nlb, 0, 0)),
        _full(n2.shape), slab(w1), slab(w2)]
    if final_norm is not None:
        args.append(final_norm)
        specs.append(_full(final_norm.shape))
    return pl.pallas_call(
        functools.partial(_merge_mlp_kernel, n_att=len(atts), n_stream=len(stream), nlb=nlb,
                          final=final_norm is not None),
        grid=(b, n_tok // tm),
        in_specs=specs,
        out_specs=tok(d),
        out_shape=jax.ShapeDtypeStruct((b, n_tok, d), F32),
        compiler_params=_cparams(("parallel", "parallel")),
        name="merge_mlp",
    )(*args)


_O_QKV = 2 * GDN_HEADS * GDN_DK + GDN_HEADS * GDN_DV
_O_QK = 2 * GDN_HEADS * GDN_DK
_O_G = GDN_HEADS * GDN_DV
_O_N = _O_QKV + _O_G + LANES
HALO = 8


def _odd_proj_kernel(h_ref, hp_ref, hn_ref, mod_ref, n1_ref, win_ref, cw_ref, e_ref, et_ref, rs_ref,
                     al_ref, dt_ref, q_out, k_out, v_out, g_out, gt_out, *, nlb, nb):
    j = pl.program_id(1)
    mod = mod_ref[...]
    n1 = n1_ref[...]
    pre = lambda x: (_rms(x, n1) * (1.0 + mod[1:2]) + mod[0:1]).astype(BF16)
    a = pre(h_ref[...])
    first = jnp.logical_or(j == 0, j == nlb)
    last = jnp.logical_or(j == nlb - 1, j == nb - 1)
    zero_halo = jnp.zeros((HALO, a.shape[1]), BF16)
    a_ext = jnp.concatenate([a, jnp.where(last, zero_halo, pre(hn_ref[...])),
                             jnp.where(first, zero_halo, pre(hp_ref[...]))], axis=0)
    tm = a.shape[0]
    te = tm + 2 * HALO
    hq = GDN_HEADS * GDN_DK
    gw = hq
    mm = lambda lhs, c0, c1: jnp.dot(lhs, win_ref[:, c0:c1], preferred_element_type=F32)
    project = lambda g: mm(a_ext, g * gw, (g + 1) * gw)

    pending = project(0)
    for g in range(_O_QKV // gw):
        xe = pending
        if (g + 1) * gw < _O_QKV:
            pending = project(g + 1)
        else:
            y_tail = mm(a, _O_QKV, _O_N)
        cw = cw_ref[:, g * gw:(g + 1) * gw]
        s = _silu(cw[0:1] * pltpu.roll(xe, 1, axis=0)[:tm] + cw[1:2] * xe[:tm]
                  + cw[2:3] * pltpu.roll(xe, te - 1, axis=0)[:tm])
        if g < 2:
            ss = _dot(s * s, e_ref[g * gw:(g + 1) * gw, :])
            r = lax.rsqrt(ss + EPS) * rs_ref[...]
            et = et_ref[:, g * gw:(g + 1) * gw]
            rf = sum(jnp.dot(p, et, preferred_element_type=F32) for p in _split(r, 2))
            (q_out, k_out)[g][...] = s * rf
        else:
            v_out[:, (g - 2) * gw:(g - 1) * gw] = s
    g_out[...] = y_tail[:, 0:_O_G]
    t = y_tail[:, _O_G:]
    log_a = -jnp.exp(al_ref[...]) * _softplus(t + dt_ref[...])
    lane = lax.broadcasted_iota(jnp.int32, t.shape, 1)
    gt_out[...] = jnp.where(lane < 2 * GDN_HEADS, log_a, _sigmoid(t))


def _odd_project(h, mod, n1, w_in, conv_w, a_log, dt_bias, tm, n_lat):
    b, t, d = h.shape
    nlb, nb = n_lat // tm, t // tm
    win = jnp.concatenate([w_in, jnp.zeros((d, _O_N - w_in.shape[1]), F32)], axis=1).astype(BF16)
    e = (jnp.arange(_O_QK)[:, None] // GDN_DK == jnp.arange(LANES)[None, :]).astype(BF16)
    et = e.T
    rs = jnp.concatenate([jnp.full((GDN_HEADS,), GDN_DK ** -0.5, F32), jnp.ones((LANES - GDN_HEADS,), F32)])[None, :]
    padl = lambda v: jnp.concatenate([v.reshape(-1), jnp.zeros((LANES - v.size,), F32)])[None, :]
    al, dt = padl(a_log), padl(dt_bias)
    tok = lambda n: pl.BlockSpec((None, tm, n), lambda i, j: (i, j, 0))
    hb = tm // HALO
    outs = [GDN_HEADS * GDN_DK, GDN_HEADS * GDN_DK, GDN_HEADS * GDN_DV, _O_G, LANES]
    return pl.pallas_call(
        functools.partial(_odd_proj_kernel, nlb=nlb, nb=nb),
        grid=(b, nb),
        in_specs=[tok(d),
                  pl.BlockSpec((None, HALO, d), lambda i, j: (i, jnp.maximum(j * hb - 1, 0), 0)),
                  pl.BlockSpec((None, HALO, d), lambda i, j: (i, jnp.minimum((j + 1) * hb, t // HALO - 1), 0)),
                  pl.BlockSpec((None, None, N_MOD, d), lambda i, j: (i, j // nlb, 0, 0)),
                  _full(n1.shape), _full(win.shape), _full(conv_w.shape), _full(e.shape), _full(et.shape),
                  _full(rs.shape), _full(al.shape), _full(dt.shape)],
        out_specs=[tok(n) for n in outs],
        out_shape=[jax.ShapeDtypeStruct((b, t, n), F32) for n in outs],
        compiler_params=_cparams(("parallel", "parallel")),
        name="odd_proj",
    )(h, h, h, mod, n1, win, conv_w, e, et, rs, al, dt)


def _gdn_kernel(qf, kf, vf, gcf, grf, qb, kb, vb, gcb, grb, of, ob, s_ref):
    @pl.when(pl.program_id(2) == 0)
    def _():
        s_ref[...] = jnp.zeros_like(s_ref)

    blk = qf.shape[0]
    nc = blk // CHUNK
    nh = qf.shape[1] // GDN_DK
    i32 = jnp.int32
    ri = lax.broadcasted_iota(i32, (CHUNK, CHUNK), 0)
    ci = lax.broadcasted_iota(i32, (CHUNK, CHUNK), 1)
    same = [(ri // w) == (ci // w) for w in (2, 4, 8, 16, 32)]
    lvl = [same[0]] + [jnp.logical_and(same[i], jnp.logical_not(same[i - 1])) for i in range(1, 5)]
    lvl.append(jnp.logical_not(same[4]))
    eye = jnp.where(ri == ci, 1.0, 0.0).astype(F32)
    incl = (ri >= ci, ri <= ci)
    strict = (ri > ci, ri < ci)
    bi = lax.broadcasted_iota(i32, (blk, blk), 0)
    bj = lax.broadcasted_iota(i32, (blk, blk), 1)
    same_chunk = (bi // CHUNK) == (bj // CHUNK)

    zs, cs, hs, qs, ks, vs, gbs, bbs, grows, brows, gls = ([] for _ in range(11))
    sq = (CHUNK, CHUNK)
    for z, (q, k, v, gc, gr) in enumerate(((qf, kf, vf, gcf, grf), (qb, kb, vb, gcb, grb))):
        tri = jnp.where(jnp.logical_and(same_chunk, (bi <= bj) if z else (bi >= bj)), 1.0, 0.0).astype(BF16)
        gcv, grv = gc[...], gr[...]
        cum_col = _dot_exact_lhs(tri, gcv)
        cum_row = _dot_exact_lhs(tri, grv, nt=True)
        for c in range(nc):
            r = slice(c * CHUNK, (c + 1) * CHUNK)
            last = c * CHUNK if z else (c + 1) * CHUNK - 1
            for hh in range(nh):
                idx = nh * z + hh
                zs.append(z), cs.append(c), hs.append(hh)
                qs.append(q[r, hh * GDN_DK:(hh + 1) * GDN_DK])
                ks.append(k[r, hh * GDN_DK:(hh + 1) * GDN_DK])
                vs.append(v[r, hh * GDN_DV:(hh + 1) * GDN_DV])
                gbs.append(jnp.broadcast_to(cum_col[r, idx:idx + 1], sq))
                bbs.append(jnp.broadcast_to(gcv[r, 2 * nh + idx:2 * nh + idx + 1], sq))
                grows.append(cum_row[idx:idx + 1, r])
                brows.append(grv[2 * nh + idx:2 * nh + idx + 1, r])
                gls.append(cum_col[last:last + 1, idx:idx + 1])

    decays = _stage(lambda z, gb, gr_: jnp.where(incl[z], jnp.exp(jnp.where(incl[z], gb - gr_, 0.0)), 0.0),
                    zs, gbs, grows)
    qkk = _stage(lambda q, k: _dot_nt(jnp.concatenate([q, k], axis=0), k), qs, ks)
    ns = _stage(lambda z, x, bb, d_: jnp.where(strict[z], x[CHUNK:] * bb * d_, 0.0), zs, qkk, bbs, decays)
    ts = _stage(lambda n: eye - jnp.where(lvl[0], n, 0.0), ns)
    for off in lvl[1:]:
        xs = _stage(lambda t, n: _dot(t, jnp.where(off, n, 0.0)), ts, ns)
        ts = _stage(lambda t, x: t - _dot(x, t), ts, xs)
    egs = _stage(jnp.exp, gbs)
    uws = _stage(lambda t, br, v, k, e_: _dot(t * br, jnp.concatenate([v, k * e_], axis=-1)),
                 ts, brows, vs, ks, egs)
    qks = _stage(lambda x, d_: x[:CHUNK] * d_, qkk, decays)
    kes = _stage(lambda k, gl, gb: k * jnp.exp(gl - gb), ks, gls, gbs)
    kuws = _stage(_dot_tn, kes, uws)
    quws = _stage(_dot, qks, uws)
    lhs = _stage(lambda q, e_, qu, ku: jnp.concatenate([q * e_ - qu[:, GDN_DV:], -ku[:, GDN_DV:]], axis=0),
                 qs, egs, quws, kuws)
    ges = _stage(jnp.exp, gls)

    chain = {(z, c, hh): i for i, (z, c, hh) in enumerate(zip(zs, cs, hs))}
    state = {(z, hh): s_ref[z, hh] for z in range(2) for hh in range(nh)}
    outs = (of, ob)
    for step in range(nc):
        for z in range(2):
            c = nc - 1 - step if z else step
            for hh in range(nh):
                i = chain[(z, c, hh)]
                s = state[(z, hh)]
                r = _dot(lhs[i], s)
                outs[z][c * CHUNK:(c + 1) * CHUNK, hh * GDN_DV:(hh + 1) * GDN_DV] = (
                    r[:CHUNK] + quws[i][:, :GDN_DV])
                state[(z, hh)] = s * ges[i] + (kuws[i][:, :GDN_DV] + r[CHUNK:])
    for (z, hh), s in state.items():
        s_ref[z, hh] = s


def _gdn_scan(q, k, v, gates, n_lat, n_ctx):
    b, t, _ = q.shape
    nh = GDN_SCAN_HEADS
    ng = GDN_HEADS // nh
    g = gates[..., :4 * GDN_HEADS].reshape(b, t, 2, 2, ng, nh)
    gcol = g.transpose(0, 4, 1, 2, 3, 5).reshape(b, ng, t, 4 * nh)
    grow = gcol.transpose(0, 1, 3, 2)
    blk = _scan_blk(n_lat, n_ctx)
    nlb, ncb = n_lat // blk, n_ctx // blk
    fwd, bwd = _scan_maps(nlb, ncb)
    d = None
    tok = lambda n, m: pl.BlockSpec((d, blk, n), lambda i, h, j: (i, m(j), h))
    gcs = lambda m: pl.BlockSpec((d, d, blk, 4 * nh), lambda i, h, j: (i, h, m(j), 0))
    grs = lambda m: pl.BlockSpec((d, d, 4 * nh, blk), lambda i, h, j: (i, h, 0, m(j)))
    out = jax.ShapeDtypeStruct((b, t, GDN_HEADS * GDN_DV), F32)
    dk2, dv2 = nh * GDN_DK, nh * GDN_DV
    return pl.pallas_call(
        _gdn_kernel,
        grid=(b, ng, nlb + ncb),
        in_specs=[tok(dk2, fwd), tok(dk2, fwd), tok(dv2, fwd), gcs(fwd), grs(fwd),
                  tok(dk2, bwd), tok(dk2, bwd), tok(dv2, bwd), gcs(bwd), grs(bwd)],
        out_specs=[tok(dv2, fwd), tok(dv2, bwd)],
        out_shape=[out, out],
        scratch_shapes=[pltpu.VMEM((2, nh, GDN_DK, GDN_DV), F32)],
        compiler_params=_cparams(("parallel", "parallel", "arbitrary")),
        name="gdn_scan",
    )(q, k, v, gcol, grow, q, k, v, gcol, grow)


def kernel(x, c, ctx, c_ctx, ada_w, ada_b, norm1_w, norm2_w, mlp_w1, mlp_w2, even_w_in, mla_q_norm,
           mla_w_uq, mla_kv_norm, mla_w_ukv, gla_gate_up, gla_gate_bias, gla_o_norm, even_w_out,
           gdn_w_in, gdn_conv_w, gdn_a_log, gdn_dt_bias, gdn_o_norm, gdn_w_out, final_norm):
    b, n_lat, d = x.shape
    n_ctx = ctx.shape[1]
    depth = ada_w.shape[0]
    tm = 256 if (n_lat % 256 == 0 and n_ctx % 256 == 0) else 128
    assert n_lat % tm == 0 and n_ctx % tm == 0 and n_lat % n_ctx == 0 and n_lat % GRID_W == 0

    rows = -(-(b + 1) // 8) * 8
    cc = jnp.concatenate([c, c_ctx[None, :], jnp.zeros((rows - b - 1, d), F32)], axis=0)
    mods = _ada_all(cc, ada_w, ada_b).reshape(depth, rows, N_MOD, d)
    mod_tok = jnp.stack([mods[:, :b], jnp.broadcast_to(mods[:, b:b + 1], (depth, b, N_MOD, d))], axis=2)

    stream = (x, ctx)
    tabs = _rope_tables(n_lat, n_ctx)
    w1_bf, w2_bf = _to_bf16(mlp_w1), _to_bf16(mlp_w2)
    row = lambda v: v[None, :]
    for layer in range(depth):
        need_ctx = layer < depth - 1
        n_tok = n_lat + n_ctx if need_ctx else n_lat
        mod = mod_tok[layer]
        n1 = row(norm1_w[layer])
        i = layer // 2
        if layer % 2 == 0:
            ew = _even_weights(even_w_in[i], mla_w_uq[i], mla_w_ukv[i], gla_gate_up[i], gla_gate_bias[i])
            q, k, v, gq, gk, gv, gg, lg = _even_project(
                stream, mod, n1, ew, row(mla_q_norm[i]), row(mla_kv_norm[i]), tabs, tm, n_lat)
            atts = [a for a in _attention(q, k, v, n_lat, n_ctx, need_ctx) if a is not None]
            o_f, o_b = _gla_scan(gq, gk, gv, lg, n_lat, n_ctx)
            parts, on, wo = [o_f, o_b, gg], gla_o_norm[i], even_w_out[i]
        else:
            q, k, v, g, gates = _odd_project(stream[0], mod, n1, gdn_w_in[i], gdn_conv_w[i], gdn_a_log[i],
                                             gdn_dt_bias[i], tm, n_lat)
            o_f, o_b = _gdn_scan(q, k, v, gates, n_lat, n_ctx)
            atts, parts, on, wo = [], [o_f, o_b, g], gdn_o_norm[i], gdn_w_out[i]
        stream = (_merge_mlp(atts, parts, row(on), wo.astype(BF16), stream, mod, row(norm2_w[layer]), w1_bf, w2_bf,
                             layer, tm, n_lat, n_tok, final_norm=None if need_ctx else row(final_norm)),)
    return stream[0]
```

```python
import functools

import jax
import jax.numpy as jnp
from jax import lax
from jax.experimental import pallas as pl
from jax.experimental.pallas import tpu as pltpu

F32 = jnp.float32
BF16 = jnp.bfloat16
EPS = 1e-6
LOG2E = 1.4426950408889634

GRID_W = 64
N_MOD = 6
MLA_HEADS = 8
MLA_NOPE = 64
MLA_ROPE = 32
MLA_V = 64
MLA_QK = MLA_NOPE + MLA_ROPE
MLA_Q_RANK = 384
MLA_KV_RANK = 256
ROPE_THETA = 10000.0
GLA_HEADS = 4
GLA_DK = 64
GLA_DV = 128
GLA_GATE_RANK = 16
GLA_GATE_NORM = 16.0
GDN_HEADS = 8
GDN_DK = 64
GDN_DV = 128
CHUNK = 64
MLP_HIDDEN_CHUNK = 1024
CAST_BLOCK_BYTES = 8 * 1024 * 1024
GDN_SCAN_HEADS = 4
LANES = 128
HEAD_SLOT = 128
V_SLOT = 80
VMEM_LIMIT = 56 * 1024 * 1024


def _cparams(sem):
    return pltpu.CompilerParams(dimension_semantics=sem, vmem_limit_bytes=VMEM_LIMIT)


def _dot(a, b):
    return jnp.dot(a.astype(BF16), b.astype(BF16), preferred_element_type=F32)


def _dot_nt(a, b):
    return lax.dot_general(a.astype(BF16), b.astype(BF16), (((1,), (1,)), ((), ())),
                           preferred_element_type=F32)


def _dot_tn(a, b):
    return lax.dot_general(a.astype(BF16), b.astype(BF16), (((0,), (0,)), ((), ())),
                           preferred_element_type=F32)


def _split(x, n):
    out = []
    for _ in range(n - 1):
        p = x.astype(BF16)
        out.append(p)
        x = x - p.astype(F32)
    out.append(x.astype(BF16))
    return out


def _dot_exact_lhs(m_bf, x, nt=False):
    acc = None
    for p in _split(x, 2):
        if nt:
            t = lax.dot_general(p, m_bf, (((1,), (1,)), ((), ())), preferred_element_type=F32)
        else:
            t = jnp.dot(m_bf, p, preferred_element_type=F32)
        acc = t if acc is None else acc + t
    return acc


def _rms(x, w):
    return x * lax.rsqrt(jnp.mean(x * x, axis=-1, keepdims=True) + EPS) * w


def _sigmoid(x):
    return 1.0 / (1.0 + jnp.exp(-x))


def _silu(x):
    return x * _sigmoid(x)


def _softplus(x):
    return jnp.maximum(x, 0.0) + jnp.log1p(jnp.exp(-jnp.abs(x)))


def _tile_lanes(x, n):
    return jnp.concatenate([x] * n, axis=-1)


def _stream_specs(stream, tm, nlb):
    d = stream[0].shape[-1]
    if len(stream) == 1:
        return [pl.BlockSpec((None, tm, d), lambda i, j: (i, j, 0))]
    return [pl.BlockSpec((None, tm, d), lambda i, j: (i, jnp.minimum(j, nlb - 1), 0)),
            pl.BlockSpec((None, tm, d), lambda i, j: (i, jnp.maximum(j - nlb, 0), 0))]


def _stream_block(refs, nlb):
    if len(refs) == 1:
        return refs[0][...]
    return jnp.where(pl.program_id(1) >= nlb, refs[1][...], refs[0][...])


def _stage(fn, *cols):
    return [fn(*a) for a in zip(*cols)]


def _ada_kernel(cc_ref, w_ref, b_ref, o_ref):
    o_ref[0] = _dot(_silu(cc_ref[...]), w_ref[0]) + b_ref[0]


def _ada_all(cc, ada_w, ada_b):
    depth, d, n = ada_w.shape
    rows = cc.shape[0]
    tn = 1024
    return pl.pallas_call(
        _ada_kernel,
        grid=(depth, n // tn),
        in_specs=[pl.BlockSpec((rows, d), lambda l, j: (0, 0)),
                  pl.BlockSpec((1, d, tn), lambda l, j: (l, 0, j)),
                  pl.BlockSpec((1, 1, tn), lambda l, j: (l, 0, j))],
        out_specs=pl.BlockSpec((1, rows, tn), lambda l, j: (l, 0, j)),
        out_shape=jax.ShapeDtypeStruct((depth, rows, n), F32),
        compiler_params=_cparams(("parallel", "parallel")),
        name="ada_mod",
    )(cc, ada_w, ada_b.reshape(depth, 1, n))


def _col_groups(*widths):
    edges = [0]
    for w in widths:
        edges.append(edges[-1] + w)
    return list(zip(edges[:-1], edges[1:]))


_E_CQ, _E_CKV, _E_ROT, _E_GQ, _E_GK, _E_GV, _E_GG = _col_groups(
    MLA_Q_RANK, MLA_KV_RANK, LANES, GLA_HEADS * GLA_DK, GLA_HEADS * GLA_DK, GLA_HEADS * GLA_DV, GLA_HEADS * GLA_DV)


def _even_proj_kernel(*refs, n_stream, nlb):
    (mod_ref, n1_ref, win_ref, qn_ref, wq_ref, kvn_ref, wk_ref, wv_ref, wg_ref, gb_ref, c_ref, s_ref,
     ctt_ref, stt_ref, q_out, k_out, v_out, gq_out, gk_out, gv_out, gg_out, lg_out) = refs[n_stream:]
    mod = mod_ref[...]
    a = _rms(_stream_block(refs[:n_stream], nlb), n1_ref[...]) * (1.0 + mod[1:2]) + mod[0:1]
    y = _dot(a, win_ref[...])
    sl = lambda r: y[:, r[0]:r[1]]
    ct, st = c_ref[...], s_ref[...]
    nh = MLA_HEADS
    qq = _dot_nt(wq_ref[...], _rms(sl(_E_CQ), qn_ref[...]))
    c32, s32 = ctt_ref[...], stt_ref[...]
    rot0 = nh * MLA_QK
    zero_pad = jnp.zeros((HEAD_SLOT - MLA_QK, qq.shape[1]), F32)
    pieces = []
    for h in range(nh):
        nope = qq[h * MLA_QK:h * MLA_QK + MLA_NOPE]
        pe = qq[h * MLA_QK + MLA_NOPE:(h + 1) * MLA_QK]
        pe_rot = qq[rot0 + h * MLA_ROPE:rot0 + (h + 1) * MLA_ROPE]
        pieces += [nope, pe * c32 + pe_rot * s32, zero_pad]
    q_out[...] = jnp.concatenate(pieces, axis=0).astype(q_out.dtype)
    ckvn = _rms(sl(_E_CKV), kvn_ref[...])
    yr = sl(_E_ROT)
    kpe = pltpu.roll(yr, MLA_NOPE, axis=1) * ct + pltpu.roll(yr, MLA_NOPE - MLA_ROPE, axis=1) * st
    k_out[...] = (_dot(ckvn, wk_ref[...]) + _tile_lanes(kpe, nh)).astype(k_out.dtype)
    vt = _dot_nt(wv_ref[...], ckvn)
    pad_rows = lax.broadcasted_iota(jnp.int32, (V_SLOT - MLA_V, vt.shape[1]), 0)
    ones_row = jnp.where(pad_rows == 0, 1.0, 0.0)
    v_out[...] = jnp.concatenate(
        [x for h in range(nh) for x in (vt[h * MLA_V:(h + 1) * MLA_V], ones_row)], axis=0).astype(v_out.dtype)
    gq_out[...] = sl(_E_GQ)
    gk_out[...] = sl(_E_GK)
    gv_out[...] = sl(_E_GV)
    gg_out[...] = sl(_E_GG)
    logit = _dot(yr, wg_ref[...]) + gb_ref[...]
    lg_out[...] = (jnp.minimum(logit, 0.0) - jnp.log1p(jnp.exp(-jnp.abs(logit)))) * (1.0 / GLA_GATE_NORM)


def _rot_half_cols(w):
    f = MLA_ROPE // 4
    return jnp.concatenate([-w[..., f:2 * f], w[..., 0:f], -w[..., 3 * f:4 * f], w[..., 2 * f:3 * f]], axis=-1)


def _even_weights(w_in, w_uq, w_ukv, gate_up, gate_bias):
    d = w_in.shape[0]
    hk_, hv_ = GLA_HEADS * GLA_DK, GLA_HEADS * GLA_DV
    cq, ckv, kpe, gq, gk, gv, gg, glow = (
        w_in[:, a:b_] for a, b_ in _col_groups(MLA_Q_RANK, MLA_KV_RANK, MLA_ROPE, hk_, hk_, hv_, hv_, 2 * GLA_GATE_RANK))
    lowr = 2 * GLA_GATE_RANK
    win = jnp.concatenate([cq, ckv, kpe, _rot_half_cols(kpe), glow, jnp.zeros((d, LANES - 2 * MLA_ROPE - lowr), F32),
                           gq, gk, gv, gg], axis=1).astype(BF16)
    r = w_uq.shape[0]
    wq = (w_uq * (MLA_QK ** -0.5 * LOG2E)).reshape(r, MLA_HEADS, MLA_QK)
    wq_rot = _rot_half_cols(wq[..., MLA_NOPE:])
    wq_all = jnp.concatenate([wq.reshape(r, -1), wq_rot.reshape(r, -1)], axis=1).T.astype(BF16)
    rk = w_ukv.shape[0]
    wkv = w_ukv.reshape(rk, MLA_HEADS, MLA_NOPE + MLA_V)
    wk = jnp.concatenate([wkv[..., :MLA_NOPE], jnp.zeros((rk, MLA_HEADS, HEAD_SLOT - MLA_NOPE), F32)],
                         axis=-1).reshape(rk, -1).astype(BF16)
    wv = wkv[..., MLA_NOPE:].reshape(rk, -1).T.astype(BF16)
    hk = GLA_HEADS * GLA_DK
    g0 = 2 * MLA_ROPE
    wg = jnp.zeros((LANES, 2 * hk), F32)
    wg = wg.at[g0:g0 + GLA_GATE_RANK, 0:hk].set(gate_up[0])
    wg = wg.at[g0 + GLA_GATE_RANK:g0 + 2 * GLA_GATE_RANK, hk:].set(gate_up[1])
    gb = jnp.concatenate([gate_bias[0], gate_bias[1]])[None, :]
    return win, wq_all, wk, wv, wg.astype(BF16), gb


def _rope_tables(n_lat, n_ctx):
    rows = n_lat // GRID_W
    row = jnp.repeat(jnp.arange(rows, dtype=F32), GRID_W)
    col = jnp.tile(jnp.arange(GRID_W, dtype=F32), rows)
    axis_dim = MLA_ROPE // 2
    inv_freq = ROPE_THETA ** (-jnp.arange(0, axis_dim, 2, dtype=F32) / axis_dim)
    ar, ac = row[:, None] * inv_freq, col[:, None] * inv_freq
    c32 = jnp.concatenate([jnp.cos(ar), jnp.cos(ar), jnp.cos(ac), jnp.cos(ac)], axis=1)
    s32 = jnp.concatenate([jnp.sin(ar), jnp.sin(ar), jnp.sin(ac), jnp.sin(ac)], axis=1)
    c32 = jnp.concatenate([c32, jnp.ones((n_ctx, MLA_ROPE), F32)], axis=0)
    s32 = jnp.concatenate([s32, jnp.zeros((n_ctx, MLA_ROPE), F32)], axis=0)
    slot = lambda x: jnp.pad(x, ((0, 0), (MLA_NOPE, HEAD_SLOT - MLA_QK)))
    return slot(c32), slot(s32), c32.T, s32.T


def _full(shape):
    nd = len(shape)
    return pl.BlockSpec(shape, lambda *_: (0,) * nd)


def _even_project(stream, mod, n1, ew, qn, kvn, tabs, tm, n_lat):
    b, d = stream[0].shape[0], stream[0].shape[-1]
    t = sum(s.shape[1] for s in stream)
    win, wq, wk, wv, wg, gb = ew
    ct, st, ctt, stt = tabs
    nlb = n_lat // tm
    tok = lambda n: pl.BlockSpec((None, tm, n), lambda i, j: (i, j, 0))
    tab = pl.BlockSpec((tm, HEAD_SLOT), lambda i, j: (j, 0))
    tabt = pl.BlockSpec((MLA_ROPE, tm), lambda i, j: (0, j))
    hk, hv = GLA_HEADS * GLA_DK, GLA_HEADS * GLA_DV
    feat = lambda n: pl.BlockSpec((None, n, tm), lambda i, j: (i, 0, j))
    tshape = lambda n, dt: jax.ShapeDtypeStruct((b, t, n), dt)
    fshape = lambda n, dt: jax.ShapeDtypeStruct((b, n, t), dt)
    nq, nv = MLA_HEADS * HEAD_SLOT, MLA_HEADS * V_SLOT
    return pl.pallas_call(
        functools.partial(_even_proj_kernel, n_stream=len(stream), nlb=nlb),
        grid=(b, t // tm),
        in_specs=_stream_specs(stream, tm, nlb) + [
                  pl.BlockSpec((None, None, N_MOD, d), lambda i, j: (i, j // nlb, 0, 0)),
                  _full(n1.shape), _full(win.shape), _full(qn.shape), _full(wq.shape), _full(kvn.shape),
                  _full(wk.shape), _full(wv.shape), _full(wg.shape), _full(gb.shape), tab, tab, tabt, tabt],
        out_specs=[feat(nq), tok(nq), feat(nv), tok(hk), tok(hk), tok(hv), tok(hv), tok(2 * hk)],
        out_shape=[fshape(nq, BF16), tshape(nq, BF16), fshape(nv, BF16), tshape(hk, F32), tshape(hk, F32),
                   tshape(hv, F32), tshape(hv, F32), tshape(2 * hk, F32)],
        compiler_params=_cparams(("parallel", "parallel")),
        name="even_proj",
    )(*stream, mod, n1, win, qn, wq, kvn, wk, wv, wg, gb, ct, st, ctt, stt)


def _attn_kernel(qt_ref, k_ref, vt_ref, o_ref):
    tk_all = k_ref.shape[0]
    tk = 256 if tk_all % 256 == 0 else 128
    heads = range(qt_ref.shape[0] // HEAD_SLOT)
    nkb = tk_all // tk
    qts = [qt_ref[hh * HEAD_SLOT:(hh + 1) * HEAD_SLOT, :] for hh in heads]

    def scores(kb):
        return [jnp.dot(k_ref[kb * tk:(kb + 1) * tk, hh * HEAD_SLOT:(hh + 1) * HEAD_SLOT], qts[hh],
                        preferred_element_type=F32) for hh in heads]

    s_next = scores(0)
    m = acc = None
    for kb in range(nkb):
        s = s_next
        if kb + 1 < nkb:
            s_next = scores(kb + 1)
        bm = [jnp.max(x, axis=0, keepdims=True) for x in s]
        if m is None:
            m_new = bm
        else:
            m_new = [jnp.maximum(a, b_) for a, b_ in zip(m, bm)]
        p = [jnp.exp2(x - mn) for x, mn in zip(s, m_new)]
        pv = [jnp.dot(vt_ref[hh * V_SLOT:(hh + 1) * V_SLOT, kb * tk:(kb + 1) * tk], p[hh].astype(BF16),
                      preferred_element_type=F32) for hh in heads]
        if m is None:
            acc = pv
        else:
            acc = [jnp.exp2(a - b_) * x + y for a, b_, x, y in zip(m, m_new, acc, pv)]
        m = m_new
    o_ref[...] = jnp.concatenate([(a[:MLA_V] / a[MLA_V:MLA_V + 1]).T for a in acc], axis=-1).astype(o_ref.dtype)


def _attention(qt, k, vt, n_lat, n_ctx, need_ctx):
    b, t, _ = k.shape
    hp = MLA_HEADS // 2
    tq = 1024 if n_lat % 1024 == 0 else 128
    nv = MLA_HEADS * MLA_V
    att = pl.pallas_call(
        _attn_kernel,
        grid=(b, hp, n_lat // tq),
        in_specs=[pl.BlockSpec((None, 2 * HEAD_SLOT, tq), lambda i, h, j: (i, h, j)),
                  pl.BlockSpec((None, t, 2 * HEAD_SLOT), lambda i, h, j: (i, 0, h)),
                  pl.BlockSpec((None, 2 * V_SLOT, t), lambda i, h, j: (i, h, 0))],
        out_specs=pl.BlockSpec((None, tq, 2 * MLA_V), lambda i, h, j: (i, j, h)),
        out_shape=jax.ShapeDtypeStruct((b, n_lat, nv), BF16),
        compiler_params=_cparams(("parallel", "parallel", "arbitrary")),
        name="mla_attn_lat",
    )(qt, k, vt)
    if not need_ctx:
        return att, None
    cb = t // n_ctx - 1
    att_ctx = pl.pallas_call(
        _attn_kernel,
        grid=(b,),
        in_specs=[pl.BlockSpec((None, MLA_HEADS * HEAD_SLOT, n_ctx), lambda i: (i, 0, cb)),
                  pl.BlockSpec((None, n_ctx, MLA_HEADS * HEAD_SLOT), lambda i: (i, cb, 0)),
                  pl.BlockSpec((None, MLA_HEADS * V_SLOT, n_ctx), lambda i: (i, 0, cb))],
        out_specs=pl.BlockSpec((None, n_ctx, nv), lambda i: (i, 0, 0)),
        out_shape=jax.ShapeDtypeStruct((b, n_ctx, nv), BF16),
        compiler_params=_cparams(("parallel",)),
        name="mla_attn_ctx",
    )(qt, k, vt)
    return att, att_ctx


def _scan_maps(nlb, ncb):
    fwd = lambda j: jnp.where(j < ncb, nlb + j, j - ncb)
    bwd = lambda j: nlb + ncb - 1 - j
    return fwd, bwd


def _scan_blk(n_lat, n_ctx):
    for blk in (256, 128, 64):
        if n_lat % blk == 0 and n_ctx % blk == 0:
            return blk
    raise ValueError("sequence lengths must be multiples of the 64-token chunk")


def _gla_kernel(qf, kf, vf, lf, qb, kb, vb, lb, of, ob, sf, sb):
    @pl.when(pl.program_id(1) == 0)
    def _():
        sf[...] = jnp.zeros_like(sf)
        sb[...] = jnp.zeros_like(sb)

    hk, hv = GLA_HEADS * GLA_DK, GLA_HEADS * GLA_DV
    i32 = jnp.int32
    io = lambda shape, d: lax.broadcasted_iota(i32, shape, d)
    blk = qf.shape[0]
    nc = blk // CHUNK
    aj = io((CHUNK, GLA_HEADS * CHUNK), 1) % CHUNK
    ai = io((CHUNK, GLA_HEADS * CHUNK), 0)
    a_incl = (ai >= aj, ai <= aj)
    klane = io((CHUNK, hk), 1) // GLA_DK
    bi, bj = io((blk, blk), 0), io((blk, blk), 1)
    same_chunk = (bi // CHUNK) == (bj // CHUNK)

    zs, cs, qs, ks, vs, bcs, bts = ([] for _ in range(7))
    for z, (q, k, v, lg) in enumerate(((qf, kf, vf, lf), (qb, kb, vb, lb))):
        tri = jnp.where(jnp.logical_and(same_chunk, (bi <= bj) if z else (bi >= bj)), 1.0, 0.0).astype(BF16)
        bcum = _dot_exact_lhs(tri, lg[:, z * hk:(z + 1) * hk])
        for c in range(nc):
            r = slice(c * CHUNK, (c + 1) * CHUNK)
            last = c * CHUNK if z else (c + 1) * CHUNK - 1
            zs.append(z), cs.append(c)
            qs.append(q[r, :]), ks.append(k[r, :]), vs.append(v[r, :])
            bcs.append(bcum[r, :]), bts.append(bcum[last:last + 1, :])

    heads = range(GLA_HEADS)
    hd = lambda x, h, w: x[:, h * w:(h + 1) * w]
    zero_k = jnp.zeros((CHUNK, hk), BF16)
    zero_v = jnp.zeros((CHUNK, GLA_DV), BF16)
    qds = _stage(lambda q, b_: q * (GLA_DK ** -0.5) * jnp.exp(b_), qs, bcs)
    kinv = _stage(lambda k, b_: (k * jnp.exp(-b_)).astype(BF16), ks, bcs)
    kbs = _stage(lambda x: jnp.concatenate([jnp.where(klane == h, x, zero_k) for h in heads], axis=0), kinv)
    a_s = _stage(lambda z, qd, kb_: jnp.where(a_incl[z], _dot_nt(qd, kb_), 0.0), zs, qds, kbs)
    vbs = _stage(lambda v: jnp.concatenate(
        [jnp.concatenate([hd(v, j, GLA_DV).astype(BF16) if j == h else zero_v for j in heads], axis=1)
         for h in heads], axis=0), vs)
    ois = _stage(_dot, a_s, vbs)
    kes = _stage(lambda k, b_, bt: k * jnp.exp(bt - b_), ks, bcs, bts)
    upds = _stage(lambda v, ke: [_dot_tn(hd(v, h, GLA_DV), hd(ke, h, GLA_DK)) for h in heads], vs, kes)
    ges = _stage(jnp.exp, bts)

    chain = {(z, c): i for i, (z, c) in enumerate(zip(zs, cs))}
    state = [[s_[h] for h in heads] for s_ in (sf, sb)]
    outs = (of, ob)
    for step in range(nc):
        for z in range(2):
            c = nc - 1 - step if z else step
            i = chain[(z, c)]
            inter = [_dot_nt(hd(qds[i], h, GLA_DK), state[z][h]) for h in heads]
            outs[z][c * CHUNK:(c + 1) * CHUNK, :] = ois[i] + jnp.concatenate(inter, axis=1)
            state[z] = [state[z][h] * hd(ges[i], h, GLA_DK) + upds[i][h] for h in heads]
    for s_, st in zip((sf, sb), state):
        for h in heads:
            s_[h] = st[h]


def _gla_scan(gq, gk, gv, lg, n_lat, n_ctx):
    b, t, hk = gq.shape
    hv = gv.shape[-1]
    blk = _scan_blk(n_lat, n_ctx)
    nlb, ncb = n_lat // blk, n_ctx // blk
    fwd, bwd = _scan_maps(nlb, ncb)
    spec = lambda n, m: pl.BlockSpec((None, blk, n), lambda i, j: (i, m(j), 0))
    out = jax.ShapeDtypeStruct((b, t, hv), F32)
    return pl.pallas_call(
        _gla_kernel,
        grid=(b, nlb + ncb),
        in_specs=[spec(hk, fwd), spec(hk, fwd), spec(hv, fwd), spec(2 * hk, fwd),
                  spec(hk, bwd), spec(hk, bwd), spec(hv, bwd), spec(2 * hk, bwd)],
        out_specs=[spec(hv, fwd), spec(hv, bwd)],
        out_shape=[out, out],
        scratch_shapes=[pltpu.VMEM((GLA_HEADS, GLA_DV, GLA_DK), F32)] * 2,
        compiler_params=_cparams(("parallel", "arbitrary")),
        name="gla_scan",
    )(gq, gk, gv, lg, gq, gk, gv, lg)


def _cast_kernel(x_ref, o_ref):
    o_ref[...] = x_ref[...].astype(o_ref.dtype)


def _to_bf16(w):
    depth, r, c = w.shape
    tr = min(r, CAST_BLOCK_BYTES // (4 * c))
    assert r % tr == 0
    spec = pl.BlockSpec((None, tr, c), lambda l, i: (l, i, 0))
    return pl.pallas_call(
        _cast_kernel,
        grid=(depth, r // tr),
        in_specs=[spec],
        out_specs=spec,
        out_shape=jax.ShapeDtypeStruct(w.shape, BF16),
        compiler_params=_cparams(("parallel", "parallel")),
        name="cast_bf16",
    )(w)


def _head_norm_gate(of_ref, ob_ref, g_ref, on_ref, n_heads):
    x = of_ref[...] + ob_ref[...]
    g = g_ref[...]
    outs = []
    for h in range(n_heads):
        s = slice(h * LANES, (h + 1) * LANES)
        outs.append(_rms(x[:, s], on_ref[...]) * _silu(g[:, s]))
    return jnp.concatenate(outs, axis=-1)


def _merge_mlp_kernel(*refs, n_att, n_stream, nlb, final):
    atts, parts = refs[:n_att], refs[n_att:n_att + 3]
    n_parts = n_att + 3 + n_stream
    stream = refs[n_att + 3:n_parts]
    on_ref, wo_ref, mod_ref, n2_ref, w1_ref, w2_ref = refs[n_parts:n_parts + 6]
    o_ref = refs[-1]
    mod = mod_ref[...]
    if n_att:
        att = atts[0][...]
        if n_att == 2:
            att = jnp.where(pl.program_id(1) >= nlb, atts[1][...], att)
        na = att.shape[-1]
        y = _head_norm_gate(*parts, on_ref, GLA_HEADS)
        o = jnp.dot(att, wo_ref[0:na, :], preferred_element_type=F32) + _dot(y, wo_ref[na:, :])
    else:
        o = _dot(_head_norm_gate(*parts, on_ref, GDN_HEADS), wo_ref[...])
    h = _stream_block(stream, nlb) + mod[2:3] * o
    a = (_rms(h, n2_ref[...]) * (1.0 + mod[4:5]) + mod[3:4]).astype(BF16)
    ff = w1_ref.shape[1]
    hc = min(ff, MLP_HIDDEN_CHUNK)
    acc = None
    for c in range(ff // hc):
        u = jnp.maximum(jnp.dot(a, w1_ref[:, c * hc:(c + 1) * hc], preferred_element_type=F32), 0.0)
        t = _dot(u * u, w2_ref[c * hc:(c + 1) * hc, :])
        acc = t if acc is None else acc + t
    out = h + mod[5:6] * acc
    if final:
        out = _rms(out, refs[n_parts + 6][...])
    o_ref[...] = out


def _merge_mlp(atts, parts, on, wo, stream, mod, n2, w1, w2, layer, tm, n_lat, n_tok, final_norm=None):
    b, d = stream[0].shape[0], stream[0].shape[-1]
    nlb = n_lat // tm
    tok = lambda n: pl.BlockSpec((None, tm, n), lambda i, j: (i, j, 0))
    once = lambda a: pl.BlockSpec(a.shape, lambda *_: (0,) * a.ndim, pipeline_mode=pl.Buffered(1))
    slab = lambda a: pl.BlockSpec((None,) + a.shape[1:], lambda *_: (layer, 0, 0), pipeline_mode=pl.Buffered(1))
    att_maps = (lambda i, j: (i, jnp.minimum(j, nlb - 1), 0), lambda i, j: (i, jnp.maximum(j - nlb, 0), 0))
    args = [*atts, *parts, *stream, on, wo, mod, n2, w1, w2]
    specs = [pl.BlockSpec((None, tm, a.shape[-1]), m) for a, m in zip(atts, att_maps)] + [
        tok(p.shape[-1]) for p in parts] + _stream_specs(stream, tm, nlb) + [
        _full(on.shape), once(wo),
        pl.BlockSpec((None, None, N_MOD, d), lambda i, j: (i, j // nlb, 0, 0)),
        _full(n2.shape), slab(w1), slab(w2)]
    if final_norm is not None:
        args.append(final_norm)
        specs.append(_full(final_norm.shape))
    return pl.pallas_call(
        functools.partial(_merge_mlp_kernel, n_att=len(atts), n_stream=len(stream), nlb=nlb,
                          final=final_norm is not None),
        grid=(b, n_tok // tm),
        in_specs=specs,
        out_specs=tok(d),
        out_shape=jax.ShapeDtypeStruct((b, n_tok, d), F32),
        compiler_params=_cparams(("parallel", "parallel")),
        name="merge_mlp",
    )(*args)


_O_QKV = 2 * GDN_HEADS * GDN_DK + GDN_HEADS * GDN_DV
_O_QK = 2 * GDN_HEADS * GDN_DK
_O_G = GDN_HEADS * GDN_DV
_O_N = _O_QKV + _O_G + LANES
HALO = 8


def _odd_proj_kernel(h_ref, hp_ref, hn_ref, mod_ref, n1_ref, win_ref, cw_ref, e_ref, et_ref, rs_ref,
                     al_ref, dt_ref, q_out, k_out, v_out, g_out, gt_out, *, nlb, nb):
    j = pl.program_id(1)
    mod = mod_ref[...]
    n1 = n1_ref[...]
    pre = lambda x: (_rms(x, n1) * (1.0 + mod[1:2]) + mod[0:1]).astype(BF16)
    a = pre(h_ref[...])
    first = jnp.logical_or(j == 0, j == nlb)
    last = jnp.logical_or(j == nlb - 1, j == nb - 1)
    zero_halo = jnp.zeros((HALO, a.shape[1]), BF16)
    a_ext = jnp.concatenate([a, jnp.where(last, zero_halo, pre(hn_ref[...])),
                             jnp.where(first, zero_halo, pre(hp_ref[...]))], axis=0)
    tm = a.shape[0]
    te = tm + 2 * HALO
    hq = GDN_HEADS * GDN_DK
    gw = hq
    mm = lambda lhs, c0, c1: jnp.dot(lhs, win_ref[:, c0:c1], preferred_element_type=F32)
    project = lambda g: mm(a_ext, g * gw, (g + 1) * gw)

    pending = project(0)
    for g in range(_O_QKV // gw):
        xe = pending
        if (g + 1) * gw < _O_QKV:
            pending = project(g + 1)
        else:
            y_tail = mm(a, _O_QKV, _O_N)
        cw = cw_ref[:, g * gw:(g + 1) * gw]
        s = _silu(cw[0:1] * pltpu.roll(xe, 1, axis=0)[:tm] + cw[1:2] * xe[:tm]
                  + cw[2:3] * pltpu.roll(xe, te - 1, axis=0)[:tm])
        if g < 2:
            ss = _dot(s * s, e_ref[g * gw:(g + 1) * gw, :])
            r = lax.rsqrt(ss + EPS) * rs_ref[...]
            et = et_ref[:, g * gw:(g + 1) * gw]
            rf = sum(jnp.dot(p, et, preferred_element_type=F32) for p in _split(r, 2))
            (q_out, k_out)[g][...] = s * rf
        else:
            v_out[:, (g - 2) * gw:(g - 1) * gw] = s
    g_out[...] = y_tail[:, 0:_O_G]
    t = y_tail[:, _O_G:]
    log_a = -jnp.exp(al_ref[...]) * _softplus(t + dt_ref[...])
    lane = lax.broadcasted_iota(jnp.int32, t.shape, 1)
    gt_out[...] = jnp.where(lane < 2 * GDN_HEADS, log_a, _sigmoid(t))


def _odd_project(h, mod, n1, w_in, conv_w, a_log, dt_bias, tm, n_lat):
    b, t, d = h.shape
    nlb, nb = n_lat // tm, t // tm
    win = jnp.concatenate([w_in, jnp.zeros((d, _O_N - w_in.shape[1]), F32)], axis=1).astype(BF16)
    e = (jnp.arange(_O_QK)[:, None] // GDN_DK == jnp.arange(LANES)[None, :]).astype(BF16)
    et = e.T
    rs = jnp.concatenate([jnp.full((GDN_HEADS,), GDN_DK ** -0.5, F32), jnp.ones((LANES - GDN_HEADS,), F32)])[None, :]
    padl = lambda v: jnp.concatenate([v.reshape(-1), jnp.zeros((LANES - v.size,), F32)])[None, :]
    al, dt = padl(a_log), padl(dt_bias)
    tok = lambda n: pl.BlockSpec((None, tm, n), lambda i, j: (i, j, 0))
    hb = tm // HALO
    outs = [GDN_HEADS * GDN_DK, GDN_HEADS * GDN_DK, GDN_HEADS * GDN_DV, _O_G, LANES]
    return pl.pallas_call(
        functools.partial(_odd_proj_kernel, nlb=nlb, nb=nb),
        grid=(b, nb),
        in_specs=[tok(d),
                  pl.BlockSpec((None, HALO, d), lambda i, j: (i, jnp.maximum(j * hb - 1, 0), 0)),
                  pl.BlockSpec((None, HALO, d), lambda i, j: (i, jnp.minimum((j + 1) * hb, t // HALO - 1), 0)),
                  pl.BlockSpec((None, None, N_MOD, d), lambda i, j: (i, j // nlb, 0, 0)),
                  _full(n1.shape), _full(win.shape), _full(conv_w.shape), _full(e.shape), _full(et.shape),
                  _full(rs.shape), _full(al.shape), _full(dt.shape)],
        out_specs=[tok(n) for n in outs],
        out_shape=[jax.ShapeDtypeStruct((b, t, n), F32) for n in outs],
        compiler_params=_cparams(("parallel", "parallel")),
        name="odd_proj",
    )(h, h, h, mod, n1, win, conv_w, e, et, rs, al, dt)


def _gdn_kernel(qf, kf, vf, gcf, grf, qb, kb, vb, gcb, grb, of, ob, s_ref):
    @pl.when(pl.program_id(2) == 0)
    def _():
        s_ref[...] = jnp.zeros_like(s_ref)

    blk = qf.shape[0]
    nc = blk // CHUNK
    nh = qf.shape[1] // GDN_DK
    i32 = jnp.int32
    ri = lax.broadcasted_iota(i32, (CHUNK, CHUNK), 0)
    ci = lax.broadcasted_iota(i32, (CHUNK, CHUNK), 1)
    same = [(ri // w) == (ci // w) for w in (2, 4, 8, 16, 32)]
    lvl = [same[0]] + [jnp.logical_and(same[i], jnp.logical_not(same[i - 1])) for i in range(1, 5)]
    lvl.append(jnp.logical_not(same[4]))
    eye = jnp.where(ri == ci, 1.0, 0.0).astype(F32)
    incl = (ri >= ci, ri <= ci)
    strict = (ri > ci, ri < ci)
    bi = lax.broadcasted_iota(i32, (blk, blk), 0)
    bj = lax.broadcasted_iota(i32, (blk, blk), 1)
    same_chunk = (bi // CHUNK) == (bj // CHUNK)

    zs, cs, hs, qs, ks, vs, gbs, bbs, grows, brows, gls = ([] for _ in range(11))
    sq = (CHUNK, CHUNK)
    for z, (q, k, v, gc, gr) in enumerate(((qf, kf, vf, gcf, grf), (qb, kb, vb, gcb, grb))):
        tri = jnp.where(jnp.logical_and(same_chunk, (bi <= bj) if z else (bi >= bj)), 1.0, 0.0).astype(BF16)
        gcv, grv = gc[...], gr[...]
        cum_col = _dot_exact_lhs(tri, gcv)
        cum_row = _dot_exact_lhs(tri, grv, nt=True)
        for c in range(nc):
            r = slice(c * CHUNK, (c + 1) * CHUNK)
            last = c * CHUNK if z else (c + 1) * CHUNK - 1
            for hh in range(nh):
                idx = nh * z + hh
                zs.append(z), cs.append(c), hs.append(hh)
                qs.append(q[r, hh * GDN_DK:(hh + 1) * GDN_DK])
                ks.append(k[r, hh * GDN_DK:(hh + 1) * GDN_DK])
                vs.append(v[r, hh * GDN_DV:(hh + 1) * GDN_DV])
                gbs.append(jnp.broadcast_to(cum_col[r, idx:idx + 1], sq))
                bbs.append(jnp.broadcast_to(gcv[r, 2 * nh + idx:2 * nh + idx + 1], sq))
                grows.append(cum_row[idx:idx + 1, r])
                brows.append(grv[2 * nh + idx:2 * nh + idx + 1, r])
                gls.append(cum_col[last:last + 1, idx:idx + 1])

    decays = _stage(lambda z, gb, gr_: jnp.where(incl[z], jnp.exp(jnp.where(incl[z], gb - gr_, 0.0)), 0.0),
                    zs, gbs, grows)
    qkk = _stage(lambda q, k: _dot_nt(jnp.concatenate([q, k], axis=0), k), qs, ks)
    ns = _stage(lambda z, x, bb, d_: jnp.where(strict[z], x[CHUNK:] * bb * d_, 0.0), zs, qkk, bbs, decays)
    ts = _stage(lambda n: eye - jnp.where(lvl[0], n, 0.0), ns)
    for off in lvl[1:]:
        xs = _stage(lambda t, n: _dot(t, jnp.where(off, n, 0.0)), ts, ns)
        ts = _stage(lambda t, x: t - _dot(x, t), ts, xs)
    egs = _stage(jnp.exp, gbs)
    uws = _stage(lambda t, br, v, k, e_: _dot(t * br, jnp.concatenate([v, k * e_], axis=-1)),
                 ts, brows, vs, ks, egs)
    qks = _stage(lambda x, d_: x[:CHUNK] * d_, qkk, decays)
    kes = _stage(lambda k, gl, gb: k * jnp.exp(gl - gb), ks, gls, gbs)
    kuws = _stage(_dot_tn, kes, uws)
    quws = _stage(_dot, qks, uws)
    lhs = _stage(lambda q, e_, qu, ku: jnp.concatenate([q * e_ - qu[:, GDN_DV:], -ku[:, GDN_DV:]], axis=0),
                 qs, egs, quws, kuws)
    ges = _stage(jnp.exp, gls)

    chain = {(z, c, hh): i for i, (z, c, hh) in enumerate(zip(zs, cs, hs))}
    state = {(z, hh): s_ref[z, hh] for z in range(2) for hh in range(nh)}
    outs = (of, ob)
    for step in range(nc):
        for z in range(2):
            c = nc - 1 - step if z else step
            for hh in range(nh):
                i = chain[(z, c, hh)]
                s = state[(z, hh)]
                r = _dot(lhs[i], s)
                outs[z][c * CHUNK:(c + 1) * CHUNK, hh * GDN_DV:(hh + 1) * GDN_DV] = (
                    r[:CHUNK] + quws[i][:, :GDN_DV])
                state[(z, hh)] = s * ges[i] + (kuws[i][:, :GDN_DV] + r[CHUNK:])
    for (z, hh), s in state.items():
        s_ref[z, hh] = s


def _gdn_scan(q, k, v, gates, n_lat, n_ctx):
    b, t, _ = q.shape
    nh = GDN_SCAN_HEADS
    ng = GDN_HEADS // nh
    g = gates[..., :4 * GDN_HEADS].reshape(b, t, 2, 2, ng, nh)
    gcol = g.transpose(0, 4, 1, 2, 3, 5).reshape(b, ng, t, 4 * nh)
    grow = gcol.transpose(0, 1, 3, 2)
    blk = _scan_blk(n_lat, n_ctx)
    nlb, ncb = n_lat // blk, n_ctx // blk
    fwd, bwd = _scan_maps(nlb, ncb)
    d = None
    tok = lambda n, m: pl.BlockSpec((d, blk, n), lambda i, h, j: (i, m(j), h))
    gcs = lambda m: pl.BlockSpec((d, d, blk, 4 * nh), lambda i, h, j: (i, h, m(j), 0))
    grs = lambda m: pl.BlockSpec((d, d, 4 * nh, blk), lambda i, h, j: (i, h, 0, m(j)))
    out = jax.ShapeDtypeStruct((b, t, GDN_HEADS * GDN_DV), F32)
    dk2, dv2 = nh * GDN_DK, nh * GDN_DV
    return pl.pallas_call(
        _gdn_kernel,
        grid=(b, ng, nlb + ncb),
        in_specs=[tok(dk2, fwd), tok(dk2, fwd), tok(dv2, fwd), gcs(fwd), grs(fwd),
                  tok(dk2, bwd), tok(dk2, bwd), tok(dv2, bwd), gcs(bwd), grs(bwd)],
        out_specs=[tok(dv2, fwd), tok(dv2, bwd)],
        out_shape=[out, out],
        scratch_shapes=[pltpu.VMEM((2, nh, GDN_DK, GDN_DV), F32)],
        compiler_params=_cparams(("parallel", "parallel", "arbitrary")),
        name="gdn_scan",
    )(q, k, v, gcol, grow, q, k, v, gcol, grow)


def kernel(x, c, ctx, c_ctx, ada_w, ada_b, norm1_w, norm2_w, mlp_w1, mlp_w2, even_w_in, mla_q_norm,
           mla_w_uq, mla_kv_norm, mla_w_ukv, gla_gate_up, gla_gate_bias, gla_o_norm, even_w_out,
           gdn_w_in, gdn_conv_w, gdn_a_log, gdn_dt_bias, gdn_o_norm, gdn_w_out, final_norm):
    b, n_lat, d = x.shape
    n_ctx = ctx.shape[1]
    depth = ada_w.shape[0]
    tm = 256 if (n_lat % 256 == 0 and n_ctx % 256 == 0) else 128
    assert n_lat % tm == 0 and n_ctx % tm == 0 and n_lat % n_ctx == 0 and n_lat % GRID_W == 0

    rows = -(-(b + 1) // 8) * 8
    cc = jnp.concatenate([c, c_ctx[None, :], jnp.zeros((rows - b - 1, d), F32)], axis=0)
    mods = _ada_all(cc, ada_w, ada_b).reshape(depth, rows, N_MOD, d)
    mod_tok = jnp.stack([mods[:, :b], jnp.broadcast_to(mods[:, b:b + 1], (depth, b, N_MOD, d))], axis=2)

    stream = (x, ctx)
    tabs = _rope_tables(n_lat, n_ctx)
    w1_bf, w2_bf = _to_bf16(mlp_w1), _to_bf16(mlp_w2)
    row = lambda v: v[None, :]
    for layer in range(depth):
        need_ctx = layer < depth - 1
        n_tok = n_lat + n_ctx if need_ctx else n_lat
        mod = mod_tok[layer]
        n1 = row(norm1_w[layer])
        i = layer // 2
        if layer % 2 == 0:
            ew = _even_weights(even_w_in[i], mla_w_uq[i], mla_w_ukv[i], gla_gate_up[i], gla_gate_bias[i])
            q, k, v, gq, gk, gv, gg, lg = _even_project(
                stream, mod, n1, ew, row(mla_q_norm[i]), row(mla_kv_norm[i]), tabs, tm, n_lat)
            atts = [a for a in _attention(q, k, v, n_lat, n_ctx, need_ctx) if a is not None]
            o_f, o_b = _gla_scan(gq, gk, gv, lg, n_lat, n_ctx)
            parts, on, wo = [o_f, o_b, gg], gla_o_norm[i], even_w_out[i]
        else:
            q, k, v, g, gates = _odd_project(stream[0], mod, n1, gdn_w_in[i], gdn_conv_w[i], gdn_a_log[i],
                                             gdn_dt_bias[i], tm, n_lat)
            o_f, o_b = _gdn_scan(q, k, v, gates, n_lat, n_ctx)
            atts, parts, on, wo = [], [o_f, o_b, g], gdn_o_norm[i], gdn_w_out[i]
        stream = (_merge_mlp(atts, parts, row(on), wo.astype(BF16), stream, mod, row(norm2_w[layer]), w1_bf, w2_bf,
                             layer, tm, n_lat, n_tok, final_norm=None if need_ctx else row(final_norm)),)
    return stream[0]
```

```python
import functools

import jax
import jax.numpy as jnp
from jax import lax
from jax.experimental import pallas as pl
from jax.experimental.pallas import tpu as pltpu

F32 = jnp.float32
BF16 = jnp.bfloat16
EPS = 1e-6
LOG2E = 1.4426950408889634

GRID_W = 64
N_MOD = 6
MLA_HEADS = 8
MLA_NOPE = 64
MLA_ROPE = 32
MLA_V = 64
MLA_QK = MLA_NOPE + MLA_ROPE
MLA_Q_RANK = 384
MLA_KV_RANK = 256
ROPE_THETA = 10000.0
GLA_HEADS = 4
GLA_DK = 64
GLA_DV = 128
GLA_GATE_RANK = 16
GLA_GATE_NORM = 16.0
GDN_HEADS = 8
GDN_DK = 64
GDN_DV = 128
CHUNK = 64
MLP_HIDDEN_CHUNK = 1024
CAST_BLOCK_BYTES = 8 * 1024 * 1024
GLA_SCAN_BLOCK = 256
GDN_SCAN_HEADS = 8
GDN_SCAN_BLOCK = 128
LANES = 128
HEAD_SLOT = 128
V_SLOT = 80
VMEM_LIMIT = 56 * 1024 * 1024


def _cparams(sem):
    return pltpu.CompilerParams(dimension_semantics=sem, vmem_limit_bytes=VMEM_LIMIT)


def _dot(a, b):
    return jnp.dot(a.astype(BF16), b.astype(BF16), preferred_element_type=F32)


def _dot_nt(a, b):
    return lax.dot_general(a.astype(BF16), b.astype(BF16), (((1,), (1,)), ((), ())),
                           preferred_element_type=F32)


def _dot_tn(a, b):
    return lax.dot_general(a.astype(BF16), b.astype(BF16), (((0,), (0,)), ((), ())),
                           preferred_element_type=F32)


def _split(x, n):
    out = []
    for _ in range(n - 1):
        p = x.astype(BF16)
        out.append(p)
        x = x - p.astype(F32)
    out.append(x.astype(BF16))
    return out


def _dot_exact_lhs(m_bf, x, nt=False):
    acc = None
    for p in _split(x, 2):
        if nt:
            t = lax.dot_general(p, m_bf, (((1,), (1,)), ((), ())), preferred_element_type=F32)
        else:
            t = jnp.dot(m_bf, p, preferred_element_type=F32)
        acc = t if acc is None else acc + t
    return acc


def _rms(x, w):
    return x * lax.rsqrt(jnp.mean(x * x, axis=-1, keepdims=True) + EPS) * w


def _sigmoid(x):
    return 1.0 / (1.0 + jnp.exp(-x))


def _silu(x):
    return x * _sigmoid(x)


def _softplus(x):
    return jnp.maximum(x, 0.0) + jnp.log1p(jnp.exp(-jnp.abs(x)))


def _tile_lanes(x, n):
    return jnp.concatenate([x] * n, axis=-1)


def _stream_specs(stream, tm, nlb):
    d = stream[0].shape[-1]
    if len(stream) == 1:
        return [pl.BlockSpec((None, tm, d), lambda i, j: (i, j, 0))]
    return [pl.BlockSpec((None, tm, d), lambda i, j: (i, jnp.minimum(j, nlb - 1), 0)),
            pl.BlockSpec((None, tm, d), lambda i, j: (i, jnp.maximum(j - nlb, 0), 0))]


def _stream_block(refs, nlb):
    if len(refs) == 1:
        return refs[0][...]
    return jnp.where(pl.program_id(1) >= nlb, refs[1][...], refs[0][...])


def _stage(fn, *cols):
    return [fn(*a) for a in zip(*cols)]


def _ada_kernel(cc_ref, w_ref, b_ref, o_ref):
    o_ref[0] = _dot(_silu(cc_ref[...]), w_ref[0]) + b_ref[0]


def _ada_all(cc, ada_w, ada_b):
    depth, d, n = ada_w.shape
    rows = cc.shape[0]
    tn = 1024
    return pl.pallas_call(
        _ada_kernel,
        grid=(depth, n // tn),
        in_specs=[pl.BlockSpec((rows, d), lambda l, j: (0, 0)),
                  pl.BlockSpec((1, d, tn), lambda l, j: (l, 0, j)),
                  pl.BlockSpec((1, 1, tn), lambda l, j: (l, 0, j))],
        out_specs=pl.BlockSpec((1, rows, tn), lambda l, j: (l, 0, j)),
        out_shape=jax.ShapeDtypeStruct((depth, rows, n), F32),
        compiler_params=_cparams(("parallel", "parallel")),
        name="ada_mod",
    )(cc, ada_w, ada_b.reshape(depth, 1, n))


def _col_groups(*widths):
    edges = [0]
    for w in widths:
        edges.append(edges[-1] + w)
    return list(zip(edges[:-1], edges[1:]))


_E_CQ, _E_CKV, _E_ROT, _E_GQ, _E_GK, _E_GV, _E_GG = _col_groups(
    MLA_Q_RANK, MLA_KV_RANK, LANES, GLA_HEADS * GLA_DK, GLA_HEADS * GLA_DK, GLA_HEADS * GLA_DV, GLA_HEADS * GLA_DV)


def _even_proj_kernel(*refs, n_stream, nlb):
    (mod_ref, n1_ref, win_ref, qn_ref, wq_ref, kvn_ref, wk_ref, wv_ref, wg_ref, gb_ref, c_ref, s_ref,
     ctt_ref, stt_ref, q_out, k_out, v_out, gq_out, gk_out, gv_out, gg_out, lg_out) = refs[n_stream:]
    mod = mod_ref[...]
    a = _rms(_stream_block(refs[:n_stream], nlb), n1_ref[...]) * (1.0 + mod[1:2]) + mod[0:1]
    y = _dot(a, win_ref[...])
    sl = lambda r: y[:, r[0]:r[1]]
    ct, st = c_ref[...], s_ref[...]
    nh = MLA_HEADS
    qq = _dot_nt(wq_ref[...], _rms(sl(_E_CQ), qn_ref[...]))
    c32, s32 = ctt_ref[...], stt_ref[...]
    rot0 = nh * MLA_QK
    zero_pad = jnp.zeros((HEAD_SLOT - MLA_QK, qq.shape[1]), F32)
    pieces = []
    for h in range(nh):
        nope = qq[h * MLA_QK:h * MLA_QK + MLA_NOPE]
        pe = qq[h * MLA_QK + MLA_NOPE:(h + 1) * MLA_QK]
        pe_rot = qq[rot0 + h * MLA_ROPE:rot0 + (h + 1) * MLA_ROPE]
        pieces += [nope, pe * c32 + pe_rot * s32, zero_pad]
    q_out[...] = jnp.concatenate(pieces, axis=0).astype(q_out.dtype)
    ckvn = _rms(sl(_E_CKV), kvn_ref[...])
    yr = sl(_E_ROT)
    kpe = pltpu.roll(yr, MLA_NOPE, axis=1) * ct + pltpu.roll(yr, MLA_NOPE - MLA_ROPE, axis=1) * st
    k_out[...] = (_dot(ckvn, wk_ref[...]) + _tile_lanes(kpe, nh)).astype(k_out.dtype)
    vt = _dot_nt(wv_ref[...], ckvn)
    pad_rows = lax.broadcasted_iota(jnp.int32, (V_SLOT - MLA_V, vt.shape[1]), 0)
    ones_row = jnp.where(pad_rows == 0, 1.0, 0.0)
    v_out[...] = jnp.concatenate(
        [x for h in range(nh) for x in (vt[h * MLA_V:(h + 1) * MLA_V], ones_row)], axis=0).astype(v_out.dtype)
    gq_out[...] = sl(_E_GQ)
    gk_out[...] = sl(_E_GK)
    gv_out[...] = sl(_E_GV)
    gg_out[...] = sl(_E_GG)
    logit = _dot(yr, wg_ref[...]) + gb_ref[...]
    lg_out[...] = (jnp.minimum(logit, 0.0) - jnp.log1p(jnp.exp(-jnp.abs(logit)))) * (1.0 / GLA_GATE_NORM)


def _rot_half_cols(w):
    f = MLA_ROPE // 4
    return jnp.concatenate([-w[..., f:2 * f], w[..., 0:f], -w[..., 3 * f:4 * f], w[..., 2 * f:3 * f]], axis=-1)


def _even_weights(w_in, w_uq, w_ukv, gate_up, gate_bias):
    d = w_in.shape[0]
    hk_, hv_ = GLA_HEADS * GLA_DK, GLA_HEADS * GLA_DV
    cq, ckv, kpe, gq, gk, gv, gg, glow = (
        w_in[:, a:b_] for a, b_ in _col_groups(MLA_Q_RANK, MLA_KV_RANK, MLA_ROPE, hk_, hk_, hv_, hv_, 2 * GLA_GATE_RANK))
    lowr = 2 * GLA_GATE_RANK
    win = jnp.concatenate([cq, ckv, kpe, _rot_half_cols(kpe), glow, jnp.zeros((d, LANES - 2 * MLA_ROPE - lowr), F32),
                           gq, gk, gv, gg], axis=1).astype(BF16)
    r = w_uq.shape[0]
    wq = (w_uq * (MLA_QK ** -0.5 * LOG2E)).reshape(r, MLA_HEADS, MLA_QK)
    wq_rot = _rot_half_cols(wq[..., MLA_NOPE:])
    wq_all = jnp.concatenate([wq.reshape(r, -1), wq_rot.reshape(r, -1)], axis=1).T.astype(BF16)
    rk = w_ukv.shape[0]
    wkv = w_ukv.reshape(rk, MLA_HEADS, MLA_NOPE + MLA_V)
    wk = jnp.concatenate([wkv[..., :MLA_NOPE], jnp.zeros((rk, MLA_HEADS, HEAD_SLOT - MLA_NOPE), F32)],
                         axis=-1).reshape(rk, -1).astype(BF16)
    wv = wkv[..., MLA_NOPE:].reshape(rk, -1).T.astype(BF16)
    hk = GLA_HEADS * GLA_DK
    g0 = 2 * MLA_ROPE
    wg = jnp.zeros((LANES, 2 * hk), F32)
    wg = wg.at[g0:g0 + GLA_GATE_RANK, 0:hk].set(gate_up[0])
    wg = wg.at[g0 + GLA_GATE_RANK:g0 + 2 * GLA_GATE_RANK, hk:].set(gate_up[1])
    gb = jnp.concatenate([gate_bias[0], gate_bias[1]])[None, :]
    return win, wq_all, wk, wv, wg.astype(BF16), gb


def _rope_tables(n_lat, n_ctx):
    rows = n_lat // GRID_W
    row = jnp.repeat(jnp.arange(rows, dtype=F32), GRID_W)
    col = jnp.tile(jnp.arange(GRID_W, dtype=F32), rows)
    axis_dim = MLA_ROPE // 2
    inv_freq = ROPE_THETA ** (-jnp.arange(0, axis_dim, 2, dtype=F32) / axis_dim)
    ar, ac = row[:, None] * inv_freq, col[:, None] * inv_freq
    c32 = jnp.concatenate([jnp.cos(ar), jnp.cos(ar), jnp.cos(ac), jnp.cos(ac)], axis=1)
    s32 = jnp.concatenate([jnp.sin(ar), jnp.sin(ar), jnp.sin(ac), jnp.sin(ac)], axis=1)
    c32 = jnp.concatenate([c32, jnp.ones((n_ctx, MLA_ROPE), F32)], axis=0)
    s32 = jnp.concatenate([s32, jnp.zeros((n_ctx, MLA_ROPE), F32)], axis=0)
    slot = lambda x: jnp.pad(x, ((0, 0), (MLA_NOPE, HEAD_SLOT - MLA_QK)))
    return slot(c32), slot(s32), c32.T, s32.T


def _full(shape):
    nd = len(shape)
    return pl.BlockSpec(shape, lambda *_: (0,) * nd)


def _even_project(stream, mod, n1, ew, qn, kvn, tabs, tm, n_lat):
    b, d = stream[0].shape[0], stream[0].shape[-1]
    t = sum(s.shape[1] for s in stream)
    win, wq, wk, wv, wg, gb = ew
    ct, st, ctt, stt = tabs
    nlb = n_lat // tm
    tok = lambda n: pl.BlockSpec((None, tm, n), lambda i, j: (i, j, 0))
    tab = pl.BlockSpec((tm, HEAD_SLOT), lambda i, j: (j, 0))
    tabt = pl.BlockSpec((MLA_ROPE, tm), lambda i, j: (0, j))
    hk, hv = GLA_HEADS * GLA_DK, GLA_HEADS * GLA_DV
    feat = lambda n: pl.BlockSpec((None, n, tm), lambda i, j: (i, 0, j))
    tshape = lambda n, dt: jax.ShapeDtypeStruct((b, t, n), dt)
    fshape = lambda n, dt: jax.ShapeDtypeStruct((b, n, t), dt)
    nq, nv = MLA_HEADS * HEAD_SLOT, MLA_HEADS * V_SLOT
    return pl.pallas_call(
        functools.partial(_even_proj_kernel, n_stream=len(stream), nlb=nlb),
        grid=(b, t // tm),
        in_specs=_stream_specs(stream, tm, nlb) + [
                  pl.BlockSpec((None, None, N_MOD, d), lambda i, j: (i, j // nlb, 0, 0)),
                  _full(n1.shape), _full(win.shape), _full(qn.shape), _full(wq.shape), _full(kvn.shape),
                  _full(wk.shape), _full(wv.shape), _full(wg.shape), _full(gb.shape), tab, tab, tabt, tabt],
        out_specs=[feat(nq), tok(nq), feat(nv), tok(hk), tok(hk), tok(hv), tok(hv), tok(2 * hk)],
        out_shape=[fshape(nq, BF16), tshape(nq, BF16), fshape(nv, BF16), tshape(hk, F32), tshape(hk, F32),
                   tshape(hv, F32), tshape(hv, F32), tshape(2 * hk, F32)],
        compiler_params=_cparams(("parallel", "parallel")),
        name="even_proj",
    )(*stream, mod, n1, win, qn, wq, kvn, wk, wv, wg, gb, ct, st, ctt, stt)


def _attn_kernel(qt_ref, k_ref, vt_ref, o_ref):
    tk_all = k_ref.shape[0]
    tk = 256 if tk_all % 256 == 0 else 128
    heads = range(qt_ref.shape[0] // HEAD_SLOT)
    nkb = tk_all // tk
    qts = [qt_ref[hh * HEAD_SLOT:(hh + 1) * HEAD_SLOT, :] for hh in heads]

    def scores(kb):
        return [jnp.dot(k_ref[kb * tk:(kb + 1) * tk, hh * HEAD_SLOT:(hh + 1) * HEAD_SLOT], qts[hh],
                        preferred_element_type=F32) for hh in heads]

    s_next = scores(0)
    m = acc = None
    for kb in range(nkb):
        s = s_next
        if kb + 1 < nkb:
            s_next = scores(kb + 1)
        bm = [jnp.max(x, axis=0, keepdims=True) for x in s]
        if m is None:
            m_new = bm
        else:
            m_new = [jnp.maximum(a, b_) for a, b_ in zip(m, bm)]
        p = [jnp.exp2(x - mn) for x, mn in zip(s, m_new)]
        pv = [jnp.dot(vt_ref[hh * V_SLOT:(hh + 1) * V_SLOT, kb * tk:(kb + 1) * tk], p[hh].astype(BF16),
                      preferred_element_type=F32) for hh in heads]
        if m is None:
            acc = pv
        else:
            acc = [jnp.exp2(a - b_) * x + y for a, b_, x, y in zip(m, m_new, acc, pv)]
        m = m_new
    o_ref[...] = jnp.concatenate([(a[:MLA_V] / a[MLA_V:MLA_V + 1]).T for a in acc], axis=-1).astype(o_ref.dtype)


def _attention(qt, k, vt, n_lat, n_ctx, need_ctx):
    b, t, _ = k.shape
    hp = MLA_HEADS // 2
    tq = 1024 if n_lat % 1024 == 0 else 128
    nv = MLA_HEADS * MLA_V
    att = pl.pallas_call(
        _attn_kernel,
        grid=(b, hp, n_lat // tq),
        in_specs=[pl.BlockSpec((None, 2 * HEAD_SLOT, tq), lambda i, h, j: (i, h, j)),
                  pl.BlockSpec((None, t, 2 * HEAD_SLOT), lambda i, h, j: (i, 0, h)),
                  pl.BlockSpec((None, 2 * V_SLOT, t), lambda i, h, j: (i, h, 0))],
        out_specs=pl.BlockSpec((None, tq, 2 * MLA_V), lambda i, h, j: (i, j, h)),
        out_shape=jax.ShapeDtypeStruct((b, n_lat, nv), BF16),
        compiler_params=_cparams(("parallel", "parallel", "arbitrary")),
        name="mla_attn_lat",
    )(qt, k, vt)
    if not need_ctx:
        return att, None
    cb = t // n_ctx - 1
    att_ctx = pl.pallas_call(
        _attn_kernel,
        grid=(b,),
        in_specs=[pl.BlockSpec((None, MLA_HEADS * HEAD_SLOT, n_ctx), lambda i: (i, 0, cb)),
                  pl.BlockSpec((None, n_ctx, MLA_HEADS * HEAD_SLOT), lambda i: (i, cb, 0)),
                  pl.BlockSpec((None, MLA_HEADS * V_SLOT, n_ctx), lambda i: (i, 0, cb))],
        out_specs=pl.BlockSpec((None, n_ctx, nv), lambda i: (i, 0, 0)),
        out_shape=jax.ShapeDtypeStruct((b, n_ctx, nv), BF16),
        compiler_params=_cparams(("parallel",)),
        name="mla_attn_ctx",
    )(qt, k, vt)
    return att, att_ctx


def _scan_maps(nlb, ncb):
    fwd = lambda j: jnp.where(j < ncb, nlb + j, j - ncb)
    bwd = lambda j: nlb + ncb - 1 - j
    return fwd, bwd


def _scan_blk(n_lat, n_ctx, limit):
    for blk in (256, 128, 64):
        if blk <= limit and n_lat % blk == 0 and n_ctx % blk == 0:
            return blk
    raise ValueError("sequence lengths must be multiples of the 64-token chunk")


def _gla_kernel(qf, kf, vf, lf, qb, kb, vb, lb, of, ob, sf, sb):
    @pl.when(pl.program_id(1) == 0)
    def _():
        sf[...] = jnp.zeros_like(sf)
        sb[...] = jnp.zeros_like(sb)

    hk, hv = GLA_HEADS * GLA_DK, GLA_HEADS * GLA_DV
    i32 = jnp.int32
    io = lambda shape, d: lax.broadcasted_iota(i32, shape, d)
    blk = qf.shape[0]
    nc = blk // CHUNK
    aj = io((CHUNK, GLA_HEADS * CHUNK), 1) % CHUNK
    ai = io((CHUNK, GLA_HEADS * CHUNK), 0)
    a_incl = (ai >= aj, ai <= aj)
    klane = io((CHUNK, hk), 1) // GLA_DK
    bi, bj = io((blk, blk), 0), io((blk, blk), 1)
    same_chunk = (bi // CHUNK) == (bj // CHUNK)

    zs, cs, qs, ks, vs, bcs, bts = ([] for _ in range(7))
    for z, (q, k, v, lg) in enumerate(((qf, kf, vf, lf), (qb, kb, vb, lb))):
        tri = jnp.where(jnp.logical_and(same_chunk, (bi <= bj) if z else (bi >= bj)), 1.0, 0.0).astype(BF16)
        bcum = _dot_exact_lhs(tri, lg[:, z * hk:(z + 1) * hk])
        for c in range(nc):
            r = slice(c * CHUNK, (c + 1) * CHUNK)
            last = c * CHUNK if z else (c + 1) * CHUNK - 1
            zs.append(z), cs.append(c)
            qs.append(q[r, :]), ks.append(k[r, :]), vs.append(v[r, :])
            bcs.append(bcum[r, :]), bts.append(bcum[last:last + 1, :])

    heads = range(GLA_HEADS)
    hd = lambda x, h, w: x[:, h * w:(h + 1) * w]
    zero_k = jnp.zeros((CHUNK, hk), BF16)
    zero_v = jnp.zeros((CHUNK, GLA_DV), BF16)
    qds = _stage(lambda q, b_: q * (GLA_DK ** -0.5) * jnp.exp(b_), qs, bcs)
    kinv = _stage(lambda k, b_: (k * jnp.exp(-b_)).astype(BF16), ks, bcs)
    kbs = _stage(lambda x: jnp.concatenate([jnp.where(klane == h, x, zero_k) for h in heads], axis=0), kinv)
    a_s = _stage(lambda z, qd, kb_: jnp.where(a_incl[z], _dot_nt(qd, kb_), 0.0), zs, qds, kbs)
    vbs = _stage(lambda v: jnp.concatenate(
        [jnp.concatenate([hd(v, j, GLA_DV).astype(BF16) if j == h else zero_v for j in heads], axis=1)
         for h in heads], axis=0), vs)
    ois = _stage(_dot, a_s, vbs)
    kes = _stage(lambda k, b_, bt: k * jnp.exp(bt - b_), ks, bcs, bts)
    upds = _stage(lambda v, ke: [_dot_tn(hd(v, h, GLA_DV), hd(ke, h, GLA_DK)) for h in heads], vs, kes)
    ges = _stage(jnp.exp, bts)

    chain = {(z, c): i for i, (z, c) in enumerate(zip(zs, cs))}
    state = [[s_[h] for h in heads] for s_ in (sf, sb)]
    outs = (of, ob)
    for step in range(nc):
        for z in range(2):
            c = nc - 1 - step if z else step
            i = chain[(z, c)]
            inter = [_dot_nt(hd(qds[i], h, GLA_DK), state[z][h]) for h in heads]
            outs[z][c * CHUNK:(c + 1) * CHUNK, :] = ois[i] + jnp.concatenate(inter, axis=1)
            state[z] = [state[z][h] * hd(ges[i], h, GLA_DK) + upds[i][h] for h in heads]
    for s_, st in zip((sf, sb), state):
        for h in heads:
            s_[h] = st[h]


def _gla_scan(gq, gk, gv, lg, n_lat, n_ctx):
    b, t, hk = gq.shape
    hv = gv.shape[-1]
    blk = _scan_blk(n_lat, n_ctx, GLA_SCAN_BLOCK)
    nlb, ncb = n_lat // blk, n_ctx // blk
    fwd, bwd = _scan_maps(nlb, ncb)
    spec = lambda n, m: pl.BlockSpec((None, blk, n), lambda i, j: (i, m(j), 0))
    out = jax.ShapeDtypeStruct((b, t, hv), F32)
    return pl.pallas_call(
        _gla_kernel,
        grid=(b, nlb + ncb),
        in_specs=[spec(hk, fwd), spec(hk, fwd), spec(hv, fwd), spec(2 * hk, fwd),
                  spec(hk, bwd), spec(hk, bwd), spec(hv, bwd), spec(2 * hk, bwd)],
        out_specs=[spec(hv, fwd), spec(hv, bwd)],
        out_shape=[out, out],
        scratch_shapes=[pltpu.VMEM((GLA_HEADS, GLA_DV, GLA_DK), F32)] * 2,
        compiler_params=_cparams(("parallel", "arbitrary")),
        name="gla_scan",
    )(gq, gk, gv, lg, gq, gk, gv, lg)


def _cast_kernel(x_ref, o_ref):
    o_ref[...] = x_ref[...].astype(o_ref.dtype)


def _to_bf16(w):
    depth, r, c = w.shape
    tr = min(r, CAST_BLOCK_BYTES // (4 * c))
    assert r % tr == 0
    spec = pl.BlockSpec((None, tr, c), lambda l, i: (l, i, 0))
    return pl.pallas_call(
        _cast_kernel,
        grid=(depth, r // tr),
        in_specs=[spec],
        out_specs=spec,
        out_shape=jax.ShapeDtypeStruct(w.shape, BF16),
        compiler_params=_cparams(("parallel", "parallel")),
        name="cast_bf16",
    )(w)


def _head_norm_gate(of_ref, ob_ref, g_ref, on_ref, n_heads):
    x = of_ref[...] + ob_ref[...]
    g = g_ref[...]
    outs = []
    for h in range(n_heads):
        s = slice(h * LANES, (h + 1) * LANES)
        outs.append(_rms(x[:, s], on_ref[...]) * _silu(g[:, s]))
    return jnp.concatenate(outs, axis=-1)


def _merge_mlp_kernel(*refs, n_att, n_stream, nlb, final):
    atts, parts = refs[:n_att], refs[n_att:n_att + 3]
    n_parts = n_att + 3 + n_stream
    stream = refs[n_att + 3:n_parts]
    on_ref, wo_ref, mod_ref, n2_ref, w1_ref, w2_ref = refs[n_parts:n_parts + 6]
    o_ref = refs[-1]
    mod = mod_ref[...]
    if n_att:
        att = atts[0][...]
        if n_att == 2:
            att = jnp.where(pl.program_id(1) >= nlb, atts[1][...], att)
        na = att.shape[-1]
        y = _head_norm_gate(*parts, on_ref, GLA_HEADS)
        o = jnp.dot(att, wo_ref[0:na, :], preferred_element_type=F32) + _dot(y, wo_ref[na:, :])
    else:
        o = _dot(_head_norm_gate(*parts, on_ref, GDN_HEADS), wo_ref[...])
    h = _stream_block(stream, nlb) + mod[2:3] * o
    a = (_rms(h, n2_ref[...]) * (1.0 + mod[4:5]) + mod[3:4]).astype(BF16)
    ff = w1_ref.shape[1]
    hc = min(ff, MLP_HIDDEN_CHUNK)
    acc = None
    for c in range(ff // hc):
        u = jnp.maximum(jnp.dot(a, w1_ref[:, c * hc:(c + 1) * hc], preferred_element_type=F32), 0.0)
        t = _dot(u * u, w2_ref[c * hc:(c + 1) * hc, :])
        acc = t if acc is None else acc + t
    out = h + mod[5:6] * acc
    if final:
        out = _rms(out, refs[n_parts + 6][...])
    o_ref[...] = out


def _merge_mlp(atts, parts, on, wo, stream, mod, n2, w1, w2, layer, tm, n_lat, n_tok, final_norm=None):
    b, d = stream[0].shape[0], stream[0].shape[-1]
    nlb = n_lat // tm
    tok = lambda n: pl.BlockSpec((None, tm, n), lambda i, j: (i, j, 0))
    once = lambda a: pl.BlockSpec(a.shape, lambda *_: (0,) * a.ndim, pipeline_mode=pl.Buffered(1))
    slab = lambda a: pl.BlockSpec((None,) + a.shape[1:], lambda *_: (layer, 0, 0), pipeline_mode=pl.Buffered(1))
    att_maps = (lambda i, j: (i, jnp.minimum(j, nlb - 1), 0), lambda i, j: (i, jnp.maximum(j - nlb, 0), 0))
    args = [*atts, *parts, *stream, on, wo, mod, n2, w1, w2]
    specs = [pl.BlockSpec((None, tm, a.shape[-1]), m) for a, m in zip(atts, att_maps)] + [
        tok(p.shape[-1]) for p in parts] + _stream_specs(stream, tm, nlb) + [
        _full(on.shape), once(wo),
        pl.BlockSpec((None, None, N_MOD, d), lambda i, j: (i, j // nlb, 0, 0)),
        _full(n2.shape), slab(w1), slab(w2)]
    if final_norm is not None:
        args.append(final_norm)
        specs.append(_full(final_norm.shape))
    return pl.pallas_call(
        functools.partial(_merge_mlp_kernel, n_att=len(atts), n_stream=len(stream), nlb=nlb,
                          final=final_norm is not None),
        grid=(b, n_tok // tm),
        in_specs=specs,
        out_specs=tok(d),
        out_shape=jax.ShapeDtypeStruct((b, n_tok, d), F32),
        compiler_params=_cparams(("parallel", "parallel")),
        name="merge_mlp",
    )(*args)


_O_QKV = 2 * GDN_HEADS * GDN_DK + GDN_HEADS * GDN_DV
_O_QK = 2 * GDN_HEADS * GDN_DK
_O_G = GDN_HEADS * GDN_DV
_O_N = _O_QKV + _O_G + LANES
HALO = 8


def _odd_proj_kernel(h_ref, hp_ref, hn_ref, mod_ref, n1_ref, win_ref, cw_ref, e_ref, et_ref, rs_ref,
                     al_ref, dt_ref, q_out, k_out, v_out, g_out, gt_out, *, nlb, nb):
    j = pl.program_id(1)
    mod = mod_ref[...]
    n1 = n1_ref[...]
    pre = lambda x: (_rms(x, n1) * (1.0 + mod[1:2]) + mod[0:1]).astype(BF16)
    a = pre(h_ref[...])
    first = jnp.logical_or(j == 0, j == nlb)
    last = jnp.logical_or(j == nlb - 1, j == nb - 1)
    zero_halo = jnp.zeros((HALO, a.shape[1]), BF16)
    a_ext = jnp.concatenate([a, jnp.where(last, zero_halo, pre(hn_ref[...])),
                             jnp.where(first, zero_halo, pre(hp_ref[...]))], axis=0)
    tm = a.shape[0]
    te = tm + 2 * HALO
    hq = GDN_HEADS * GDN_DK
    gw = hq
    mm = lambda lhs, c0, c1: jnp.dot(lhs, win_ref[:, c0:c1], preferred_element_type=F32)
    project = lambda g: mm(a_ext, g * gw, (g + 1) * gw)

    pending = project(0)
    for g in range(_O_QKV // gw):
        xe = pending
        if (g + 1) * gw < _O_QKV:
            pending = project(g + 1)
        else:
            y_tail = mm(a, _O_QKV, _O_N)
        cw = cw_ref[:, g * gw:(g + 1) * gw]
        s = _silu(cw[0:1] * pltpu.roll(xe, 1, axis=0)[:tm] + cw[1:2] * xe[:tm]
                  + cw[2:3] * pltpu.roll(xe, te - 1, axis=0)[:tm])
        if g < 2:
            ss = _dot(s * s, e_ref[g * gw:(g + 1) * gw, :])
            r = lax.rsqrt(ss + EPS) * rs_ref[...]
            et = et_ref[:, g * gw:(g + 1) * gw]
            rf = sum(jnp.dot(p, et, preferred_element_type=F32) for p in _split(r, 2))
            (q_out, k_out)[g][...] = s * rf
        else:
            v_out[:, (g - 2) * gw:(g - 1) * gw] = s
    g_out[...] = y_tail[:, 0:_O_G]
    t = y_tail[:, _O_G:]
    log_a = -jnp.exp(al_ref[...]) * _softplus(t + dt_ref[...])
    lane = lax.broadcasted_iota(jnp.int32, t.shape, 1)
    gt_out[...] = jnp.where(lane < 2 * GDN_HEADS, log_a, _sigmoid(t))


def _odd_project(h, mod, n1, w_in, conv_w, a_log, dt_bias, tm, n_lat):
    b, t, d = h.shape
    nlb, nb = n_lat // tm, t // tm
    win = jnp.concatenate([w_in, jnp.zeros((d, _O_N - w_in.shape[1]), F32)], axis=1).astype(BF16)
    e = (jnp.arange(_O_QK)[:, None] // GDN_DK == jnp.arange(LANES)[None, :]).astype(BF16)
    et = e.T
    rs = jnp.concatenate([jnp.full((GDN_HEADS,), GDN_DK ** -0.5, F32), jnp.ones((LANES - GDN_HEADS,), F32)])[None, :]
    padl = lambda v: jnp.concatenate([v.reshape(-1), jnp.zeros((LANES - v.size,), F32)])[None, :]
    al, dt = padl(a_log), padl(dt_bias)
    tok = lambda n: pl.BlockSpec((None, tm, n), lambda i, j: (i, j, 0))
    hb = tm // HALO
    outs = [GDN_HEADS * GDN_DK, GDN_HEADS * GDN_DK, GDN_HEADS * GDN_DV, _O_G, LANES]
    return pl.pallas_call(
        functools.partial(_odd_proj_kernel, nlb=nlb, nb=nb),
        grid=(b, nb),
        in_specs=[tok(d),
                  pl.BlockSpec((None, HALO, d), lambda i, j: (i, jnp.maximum(j * hb - 1, 0), 0)),
                  pl.BlockSpec((None, HALO, d), lambda i, j: (i, jnp.minimum((j + 1) * hb, t // HALO - 1), 0)),
                  pl.BlockSpec((None, None, N_MOD, d), lambda i, j: (i, j // nlb, 0, 0)),
                  _full(n1.shape), _full(win.shape), _full(conv_w.shape), _full(e.shape), _full(et.shape),
                  _full(rs.shape), _full(al.shape), _full(dt.shape)],
        out_specs=[tok(n) for n in outs],
        out_shape=[jax.ShapeDtypeStruct((b, t, n), F32) for n in outs],
        compiler_params=_cparams(("parallel", "parallel")),
        name="odd_proj",
    )(h, h, h, mod, n1, win, conv_w, e, et, rs, al, dt)


def _gdn_kernel(qf, kf, vf, gcf, grf, qb, kb, vb, gcb, grb, of, ob, s_ref):
    @pl.when(pl.program_id(2) == 0)
    def _():
        s_ref[...] = jnp.zeros_like(s_ref)

    blk = qf.shape[0]
    nc = blk // CHUNK
    nh = qf.shape[1] // GDN_DK
    i32 = jnp.int32
    ri = lax.broadcasted_iota(i32, (CHUNK, CHUNK), 0)
    ci = lax.broadcasted_iota(i32, (CHUNK, CHUNK), 1)
    same = [(ri // w) == (ci // w) for w in (2, 4, 8, 16, 32)]
    lvl = [same[0]] + [jnp.logical_and(same[i], jnp.logical_not(same[i - 1])) for i in range(1, 5)]
    lvl.append(jnp.logical_not(same[4]))
    eye = jnp.where(ri == ci, 1.0, 0.0).astype(F32)
    incl = (ri >= ci, ri <= ci)
    strict = (ri > ci, ri < ci)
    bi = lax.broadcasted_iota(i32, (blk, blk), 0)
    bj = lax.broadcasted_iota(i32, (blk, blk), 1)
    same_chunk = (bi // CHUNK) == (bj // CHUNK)

    zs, cs, hs, qs, ks, vs, gbs, bbs, grows, brows, gls = ([] for _ in range(11))
    sq = (CHUNK, CHUNK)
    for z, (q, k, v, gc, gr) in enumerate(((qf, kf, vf, gcf, grf), (qb, kb, vb, gcb, grb))):
        tri = jnp.where(jnp.logical_and(same_chunk, (bi <= bj) if z else (bi >= bj)), 1.0, 0.0).astype(BF16)
        gcv, grv = gc[...], gr[...]
        cum_col = _dot_exact_lhs(tri, gcv)
        cum_row = _dot_exact_lhs(tri, grv, nt=True)
        for c in range(nc):
            r = slice(c * CHUNK, (c + 1) * CHUNK)
            last = c * CHUNK if z else (c + 1) * CHUNK - 1
            for hh in range(nh):
                idx = nh * z + hh
                zs.append(z), cs.append(c), hs.append(hh)
                qs.append(q[r, hh * GDN_DK:(hh + 1) * GDN_DK])
                ks.append(k[r, hh * GDN_DK:(hh + 1) * GDN_DK])
                vs.append(v[r, hh * GDN_DV:(hh + 1) * GDN_DV])
                gbs.append(jnp.broadcast_to(cum_col[r, idx:idx + 1], sq))
                bbs.append(jnp.broadcast_to(gcv[r, 2 * nh + idx:2 * nh + idx + 1], sq))
                grows.append(cum_row[idx:idx + 1, r])
                brows.append(grv[2 * nh + idx:2 * nh + idx + 1, r])
                gls.append(cum_col[last:last + 1, idx:idx + 1])

    decays = _stage(lambda z, gb, gr_: jnp.where(incl[z], jnp.exp(jnp.where(incl[z], gb - gr_, 0.0)), 0.0),
                    zs, gbs, grows)
    qkk = _stage(lambda q, k: _dot_nt(jnp.concatenate([q, k], axis=0), k), qs, ks)
    ns = _stage(lambda z, x, bb, d_: jnp.where(strict[z], x[CHUNK:] * bb * d_, 0.0), zs, qkk, bbs, decays)
    ts = _stage(lambda n: eye - jnp.where(lvl[0], n, 0.0), ns)
    for off in lvl[1:]:
        xs = _stage(lambda t, n: _dot(t, jnp.where(off, n, 0.0)), ts, ns)
        ts = _stage(lambda t, x: t - _dot(x, t), ts, xs)
    egs = _stage(jnp.exp, gbs)
    uws = _stage(lambda t, br, v, k, e_: _dot(t * br, jnp.concatenate([v, k * e_], axis=-1)),
                 ts, brows, vs, ks, egs)
    qks = _stage(lambda x, d_: x[:CHUNK] * d_, qkk, decays)
    kes = _stage(lambda k, gl, gb: k * jnp.exp(gl - gb), ks, gls, gbs)
    kuws = _stage(_dot_tn, kes, uws)
    quws = _stage(_dot, qks, uws)
    lhs = _stage(lambda q, e_, qu, ku: jnp.concatenate([q * e_ - qu[:, GDN_DV:], -ku[:, GDN_DV:]], axis=0),
                 qs, egs, quws, kuws)
    ges = _stage(jnp.exp, gls)

    chain = {(z, c, hh): i for i, (z, c, hh) in enumerate(zip(zs, cs, hs))}
    state = {(z, hh): s_ref[z, hh] for z in range(2) for hh in range(nh)}
    outs = (of, ob)
    for step in range(nc):
        for z in range(2):
            c = nc - 1 - step if z else step
            for hh in range(nh):
                i = chain[(z, c, hh)]
                s = state[(z, hh)]
                r = _dot(lhs[i], s)
                outs[z][c * CHUNK:(c + 1) * CHUNK, hh * GDN_DV:(hh + 1) * GDN_DV] = (
                    r[:CHUNK] + quws[i][:, :GDN_DV])
                state[(z, hh)] = s * ges[i] + (kuws[i][:, :GDN_DV] + r[CHUNK:])
    for (z, hh), s in state.items():
        s_ref[z, hh] = s


def _gdn_scan(q, k, v, gates, n_lat, n_ctx):
    b, t, _ = q.shape
    nh = GDN_SCAN_HEADS
    ng = GDN_HEADS // nh
    g = gates[..., :4 * GDN_HEADS].reshape(b, t, 2, 2, ng, nh)
    gcol = g.transpose(0, 4, 1, 2, 3, 5).reshape(b, ng, t, 4 * nh)
    grow = gcol.transpose(0, 1, 3, 2)
    blk = _scan_blk(n_lat, n_ctx, GDN_SCAN_BLOCK)
    nlb, ncb = n_lat // blk, n_ctx // blk
    fwd, bwd = _scan_maps(nlb, ncb)
    d = None
    tok = lambda n, m: pl.BlockSpec((d, blk, n), lambda i, h, j: (i, m(j), h))
    gcs = lambda m: pl.BlockSpec((d, d, blk, 4 * nh), lambda i, h, j: (i, h, m(j), 0))
    grs = lambda m: pl.BlockSpec((d, d, 4 * nh, blk), lambda i, h, j: (i, h, 0, m(j)))
    out = jax.ShapeDtypeStruct((b, t, GDN_HEADS * GDN_DV), F32)
    dk2, dv2 = nh * GDN_DK, nh * GDN_DV
    return pl.pallas_call(
        _gdn_kernel,
        grid=(b, ng, nlb + ncb),
        in_specs=[tok(dk2, fwd), tok(dk2, fwd), tok(dv2, fwd), gcs(fwd), grs(fwd),
                  tok(dk2, bwd), tok(dk2, bwd), tok(dv2, bwd), gcs(bwd), grs(bwd)],
        out_specs=[tok(dv2, fwd), tok(dv2, bwd)],
        out_shape=[out, out],
        scratch_shapes=[pltpu.VMEM((2, nh, GDN_DK, GDN_DV), F32)],
        compiler_params=_cparams(("parallel", "parallel", "arbitrary")),
        name="gdn_scan",
    )(q, k, v, gcol, grow, q, k, v, gcol, grow)


def kernel(x, c, ctx, c_ctx, ada_w, ada_b, norm1_w, norm2_w, mlp_w1, mlp_w2, even_w_in, mla_q_norm,
           mla_w_uq, mla_kv_norm, mla_w_ukv, gla_gate_up, gla_gate_bias, gla_o_norm, even_w_out,
           gdn_w_in, gdn_conv_w, gdn_a_log, gdn_dt_bias, gdn_o_norm, gdn_w_out, final_norm):
    b, n_lat, d = x.shape
    n_ctx = ctx.shape[1]
    depth = ada_w.shape[0]
    tm = 256 if (n_lat % 256 == 0 and n_ctx % 256 == 0) else 128
    assert n_lat % tm == 0 and n_ctx % tm == 0 and n_lat % n_ctx == 0 and n_lat % GRID_W == 0

    rows = -(-(b + 1) // 8) * 8
    cc = jnp.concatenate([c, c_ctx[None, :], jnp.zeros((rows - b - 1, d), F32)], axis=0)
    mods = _ada_all(cc, ada_w, ada_b).reshape(depth, rows, N_MOD, d)
    mod_tok = jnp.stack([mods[:, :b], jnp.broadcast_to(mods[:, b:b + 1], (depth, b, N_MOD, d))], axis=2)

    stream = (x, ctx)
    tabs = _rope_tables(n_lat, n_ctx)
    w1_bf, w2_bf = _to_bf16(mlp_w1), _to_bf16(mlp_w2)
    row = lambda v: v[None, :]
    for layer in range(depth):
        need_ctx = layer < depth - 1
        n_tok = n_lat + n_ctx if need_ctx else n_lat
        mod = mod_tok[layer]
        n1 = row(norm1_w[layer])
        i = layer // 2
        if layer % 2 == 0:
            ew = _even_weights(even_w_in[i], mla_w_uq[i], mla_w_ukv[i], gla_gate_up[i], gla_gate_bias[i])
            q, k, v, gq, gk, gv, gg, lg = _even_project(
                stream, mod, n1, ew, row(mla_q_norm[i]), row(mla_kv_norm[i]), tabs, tm, n_lat)
            atts = [a for a in _attention(q, k, v, n_lat, n_ctx, need_ctx) if a is not None]
            o_f, o_b = _gla_scan(gq, gk, gv, lg, n_lat, n_ctx)
            parts, on, wo = [o_f, o_b, gg], gla_o_norm[i], even_w_out[i]
        else:
            q, k, v, g, gates = _odd_project(stream[0], mod, n1, gdn_w_in[i], gdn_conv_w[i], gdn_a_log[i],
                                             gdn_dt_bias[i], tm, n_lat)
            o_f, o_b = _gdn_scan(q, k, v, gates, n_lat, n_ctx)
            atts, parts, on, wo = [], [o_f, o_b, g], gdn_o_norm[i], gdn_w_out[i]
        stream = (_merge_mlp(atts, parts, row(on), wo.astype(BF16), stream, mod, row(norm2_w[layer]), w1_bf, w2_bf,
                             layer, tm, n_lat, n_tok, final_norm=None if need_ctx else row(final_norm)),)
    return stream[0]
```

```python
import functools

import jax
import jax.numpy as jnp
from jax import lax
from jax.experimental import pallas as pl
from jax.experimental.pallas import tpu as pltpu

F32 = jnp.float32
BF16 = jnp.bfloat16
EPS = 1e-6
LOG2E = 1.4426950408889634

GRID_W = 64
N_MOD = 6
MLA_HEADS = 8
MLA_NOPE = 64
MLA_ROPE = 32
MLA_V = 64
MLA_QK = MLA_NOPE + MLA_ROPE
MLA_Q_RANK = 384
MLA_KV_RANK = 256
ROPE_THETA = 10000.0
GLA_HEADS = 4
GLA_DK = 64
GLA_DV = 128
GLA_GATE_RANK = 16
GLA_GATE_NORM = 16.0
GDN_HEADS = 8
GDN_DK = 64
GDN_DV = 128
CHUNK = 64
MLP_HIDDEN_CHUNK = 1024
CAST_BLOCK_BYTES = 8 * 1024 * 1024
GLA_SCAN_BLOCK = 256
GLA_SCAN_BATCH = 4
GDN_SCAN_HEADS = 8
GDN_SCAN_BLOCK = 128
LANES = 128
HEAD_SLOT = 128
V_SLOT = 80
VMEM_LIMIT = 56 * 1024 * 1024


def _cparams(sem):
    return pltpu.CompilerParams(dimension_semantics=sem, vmem_limit_bytes=VMEM_LIMIT)


def _dot(a, b):
    return jnp.dot(a.astype(BF16), b.astype(BF16), preferred_element_type=F32)


def _dot_nt(a, b):
    return lax.dot_general(a.astype(BF16), b.astype(BF16), (((1,), (1,)), ((), ())),
                           preferred_element_type=F32)


def _dot_tn(a, b):
    return lax.dot_general(a.astype(BF16), b.astype(BF16), (((0,), (0,)), ((), ())),
                           preferred_element_type=F32)


def _split(x, n):
    out = []
    for _ in range(n - 1):
        p = x.astype(BF16)
        out.append(p)
        x = x - p.astype(F32)
    out.append(x.astype(BF16))
    return out


def _dot_exact_lhs(m_bf, x, nt=False):
    acc = None
    for p in _split(x, 2):
        if nt:
            t = lax.dot_general(p, m_bf, (((1,), (1,)), ((), ())), preferred_element_type=F32)
        else:
            t = jnp.dot(m_bf, p, preferred_element_type=F32)
        acc = t if acc is None else acc + t
    return acc


def _rms(x, w):
    return x * lax.rsqrt(jnp.mean(x * x, axis=-1, keepdims=True) + EPS) * w


def _sigmoid(x):
    return 1.0 / (1.0 + jnp.exp(-x))


def _silu(x):
    return x * _sigmoid(x)


def _softplus(x):
    return jnp.maximum(x, 0.0) + jnp.log1p(jnp.exp(-jnp.abs(x)))


def _tile_lanes(x, n):
    return jnp.concatenate([x] * n, axis=-1)


def _stream_specs(stream, tm, nlb):
    d = stream[0].shape[-1]
    if len(stream) == 1:
        return [pl.BlockSpec((None, tm, d), lambda i, j: (i, j, 0))]
    return [pl.BlockSpec((None, tm, d), lambda i, j: (i, jnp.minimum(j, nlb - 1), 0)),
            pl.BlockSpec((None, tm, d), lambda i, j: (i, jnp.maximum(j - nlb, 0), 0))]


def _stream_block(refs, nlb):
    if len(refs) == 1:
        return refs[0][...]
    return jnp.where(pl.program_id(1) >= nlb, refs[1][...], refs[0][...])


def _stage(fn, *cols):
    return [fn(*a) for a in zip(*cols)]


def _ada_kernel(cc_ref, w_ref, b_ref, o_ref):
    o_ref[0] = _dot(_silu(cc_ref[...]), w_ref[0]) + b_ref[0]


def _ada_all(cc, ada_w, ada_b):
    depth, d, n = ada_w.shape
    rows = cc.shape[0]
    tn = 1024
    return pl.pallas_call(
        _ada_kernel,
        grid=(depth, n // tn),
        in_specs=[pl.BlockSpec((rows, d), lambda l, j: (0, 0)),
                  pl.BlockSpec((1, d, tn), lambda l, j: (l, 0, j)),
                  pl.BlockSpec((1, 1, tn), lambda l, j: (l, 0, j))],
        out_specs=pl.BlockSpec((1, rows, tn), lambda l, j: (l, 0, j)),
        out_shape=jax.ShapeDtypeStruct((depth, rows, n), F32),
        compiler_params=_cparams(("parallel", "parallel")),
        name="ada_mod",
    )(cc, ada_w, ada_b.reshape(depth, 1, n))


def _col_groups(*widths):
    edges = [0]
    for w in widths:
        edges.append(edges[-1] + w)
    return list(zip(edges[:-1], edges[1:]))


_E_CQ, _E_CKV, _E_ROT, _E_GQ, _E_GK, _E_GV, _E_GG = _col_groups(
    MLA_Q_RANK, MLA_KV_RANK, LANES, GLA_HEADS * GLA_DK, GLA_HEADS * GLA_DK, GLA_HEADS * GLA_DV, GLA_HEADS * GLA_DV)


def _even_proj_kernel(*refs, n_stream, nlb):
    (mod_ref, n1_ref, win_ref, qn_ref, wq_ref, kvn_ref, wk_ref, wv_ref, wg_ref, gb_ref, c_ref, s_ref,
     ctt_ref, stt_ref, q_out, k_out, v_out, gq_out, gk_out, gv_out, gg_out, lg_out) = refs[n_stream:]
    mod = mod_ref[...]
    a = _rms(_stream_block(refs[:n_stream], nlb), n1_ref[...]) * (1.0 + mod[1:2]) + mod[0:1]
    y = _dot(a, win_ref[...])
    sl = lambda r: y[:, r[0]:r[1]]
    ct, st = c_ref[...], s_ref[...]
    nh = MLA_HEADS
    qq = _dot_nt(wq_ref[...], _rms(sl(_E_CQ), qn_ref[...]))
    c32, s32 = ctt_ref[...], stt_ref[...]
    rot0 = nh * MLA_QK
    zero_pad = jnp.zeros((HEAD_SLOT - MLA_QK, qq.shape[1]), F32)
    pieces = []
    for h in range(nh):
        nope = qq[h * MLA_QK:h * MLA_QK + MLA_NOPE]
        pe = qq[h * MLA_QK + MLA_NOPE:(h + 1) * MLA_QK]
        pe_rot = qq[rot0 + h * MLA_ROPE:rot0 + (h + 1) * MLA_ROPE]
        pieces += [nope, pe * c32 + pe_rot * s32, zero_pad]
    q_out[...] = jnp.concatenate(pieces, axis=0).astype(q_out.dtype)
    ckvn = _rms(sl(_E_CKV), kvn_ref[...])
    yr = sl(_E_ROT)
    kpe = pltpu.roll(yr, MLA_NOPE, axis=1) * ct + pltpu.roll(yr, MLA_NOPE - MLA_ROPE, axis=1) * st
    k_out[...] = (_dot(ckvn, wk_ref[...]) + _tile_lanes(kpe, nh)).astype(k_out.dtype)
    vt = _dot_nt(wv_ref[...], ckvn)
    pad_rows = lax.broadcasted_iota(jnp.int32, (V_SLOT - MLA_V, vt.shape[1]), 0)
    ones_row = jnp.where(pad_rows == 0, 1.0, 0.0)
    v_out[...] = jnp.concatenate(
        [x for h in range(nh) for x in (vt[h * MLA_V:(h + 1) * MLA_V], ones_row)], axis=0).astype(v_out.dtype)
    gq_out[...] = sl(_E_GQ)
    gk_out[...] = sl(_E_GK)
    gv_out[...] = sl(_E_GV)
    gg_out[...] = sl(_E_GG)
    logit = _dot(yr, wg_ref[...]) + gb_ref[...]
    lg_out[...] = (jnp.minimum(logit, 0.0) - jnp.log1p(jnp.exp(-jnp.abs(logit)))) * (1.0 / GLA_GATE_NORM)


def _rot_half_cols(w):
    f = MLA_ROPE // 4
    return jnp.concatenate([-w[..., f:2 * f], w[..., 0:f], -w[..., 3 * f:4 * f], w[..., 2 * f:3 * f]], axis=-1)


def _even_weights(w_in, w_uq, w_ukv, gate_up, gate_bias):
    d = w_in.shape[0]
    hk_, hv_ = GLA_HEADS * GLA_DK, GLA_HEADS * GLA_DV
    cq, ckv, kpe, gq, gk, gv, gg, glow = (
        w_in[:, a:b_] for a, b_ in _col_groups(MLA_Q_RANK, MLA_KV_RANK, MLA_ROPE, hk_, hk_, hv_, hv_, 2 * GLA_GATE_RANK))
    lowr = 2 * GLA_GATE_RANK
    win = jnp.concatenate([cq, ckv, kpe, _rot_half_cols(kpe), glow, jnp.zeros((d, LANES - 2 * MLA_ROPE - lowr), F32),
                           gq, gk, gv, gg], axis=1).astype(BF16)
    r = w_uq.shape[0]
    wq = (w_uq * (MLA_QK ** -0.5 * LOG2E)).reshape(r, MLA_HEADS, MLA_QK)
    wq_rot = _rot_half_cols(wq[..., MLA_NOPE:])
    wq_all = jnp.concatenate([wq.reshape(r, -1), wq_rot.reshape(r, -1)], axis=1).T.astype(BF16)
    rk = w_ukv.shape[0]
    wkv = w_ukv.reshape(rk, MLA_HEADS, MLA_NOPE + MLA_V)
    wk = jnp.concatenate([wkv[..., :MLA_NOPE], jnp.zeros((rk, MLA_HEADS, HEAD_SLOT - MLA_NOPE), F32)],
                         axis=-1).reshape(rk, -1).astype(BF16)
    wv = wkv[..., MLA_NOPE:].reshape(rk, -1).T.astype(BF16)
    hk = GLA_HEADS * GLA_DK
    g0 = 2 * MLA_ROPE
    wg = jnp.zeros((LANES, 2 * hk), F32)
    wg = wg.at[g0:g0 + GLA_GATE_RANK, 0:hk].set(gate_up[0])
    wg = wg.at[g0 + GLA_GATE_RANK:g0 + 2 * GLA_GATE_RANK, hk:].set(gate_up[1])
    gb = jnp.concatenate([gate_bias[0], gate_bias[1]])[None, :]
    return win, wq_all, wk, wv, wg.astype(BF16), gb


def _rope_tables(n_lat, n_ctx):
    rows = n_lat // GRID_W
    row = jnp.repeat(jnp.arange(rows, dtype=F32), GRID_W)
    col = jnp.tile(jnp.arange(GRID_W, dtype=F32), rows)
    axis_dim = MLA_ROPE // 2
    inv_freq = ROPE_THETA ** (-jnp.arange(0, axis_dim, 2, dtype=F32) / axis_dim)
    ar, ac = row[:, None] * inv_freq, col[:, None] * inv_freq
    c32 = jnp.concatenate([jnp.cos(ar), jnp.cos(ar), jnp.cos(ac), jnp.cos(ac)], axis=1)
    s32 = jnp.concatenate([jnp.sin(ar), jnp.sin(ar), jnp.sin(ac), jnp.sin(ac)], axis=1)
    c32 = jnp.concatenate([c32, jnp.ones((n_ctx, MLA_ROPE), F32)], axis=0)
    s32 = jnp.concatenate([s32, jnp.zeros((n_ctx, MLA_ROPE), F32)], axis=0)
    slot = lambda x: jnp.pad(x, ((0, 0), (MLA_NOPE, HEAD_SLOT - MLA_QK)))
    return slot(c32), slot(s32), c32.T, s32.T


def _full(shape):
    nd = len(shape)
    return pl.BlockSpec(shape, lambda *_: (0,) * nd)


def _even_project(stream, mod, n1, ew, qn, kvn, tabs, tm, n_lat):
    b, d = stream[0].shape[0], stream[0].shape[-1]
    t = sum(s.shape[1] for s in stream)
    win, wq, wk, wv, wg, gb = ew
    ct, st, ctt, stt = tabs
    nlb = n_lat // tm
    tok = lambda n: pl.BlockSpec((None, tm, n), lambda i, j: (i, j, 0))
    tab = pl.BlockSpec((tm, HEAD_SLOT), lambda i, j: (j, 0))
    tabt = pl.BlockSpec((MLA_ROPE, tm), lambda i, j: (0, j))
    hk, hv = GLA_HEADS * GLA_DK, GLA_HEADS * GLA_DV
    feat = lambda n: pl.BlockSpec((None, n, tm), lambda i, j: (i, 0, j))
    tshape = lambda n, dt: jax.ShapeDtypeStruct((b, t, n), dt)
    fshape = lambda n, dt: jax.ShapeDtypeStruct((b, n, t), dt)
    nq, nv = MLA_HEADS * HEAD_SLOT, MLA_HEADS * V_SLOT
    return pl.pallas_call(
        functools.partial(_even_proj_kernel, n_stream=len(stream), nlb=nlb),
        grid=(b, t // tm),
        in_specs=_stream_specs(stream, tm, nlb) + [
                  pl.BlockSpec((None, None, N_MOD, d), lambda i, j: (i, j // nlb, 0, 0)),
                  _full(n1.shape), _full(win.shape), _full(qn.shape), _full(wq.shape), _full(kvn.shape),
                  _full(wk.shape), _full(wv.shape), _full(wg.shape), _full(gb.shape), tab, tab, tabt, tabt],
        out_specs=[feat(nq), tok(nq), feat(nv), tok(hk), tok(hk), tok(hv), tok(hv), tok(2 * hk)],
        out_shape=[fshape(nq, BF16), tshape(nq, BF16), fshape(nv, BF16), tshape(hk, F32), tshape(hk, F32),
                   tshape(hv, F32), tshape(hv, F32), tshape(2 * hk, F32)],
        compiler_params=_cparams(("parallel", "parallel")),
        name="even_proj",
    )(*stream, mod, n1, win, qn, wq, kvn, wk, wv, wg, gb, ct, st, ctt, stt)


def _attn_kernel(qt_ref, k_ref, vt_ref, o_ref):
    tk_all = k_ref.shape[0]
    tk = 256 if tk_all % 256 == 0 else 128
    heads = range(qt_ref.shape[0] // HEAD_SLOT)
    nkb = tk_all // tk
    qts = [qt_ref[hh * HEAD_SLOT:(hh + 1) * HEAD_SLOT, :] for hh in heads]

    def scores(kb):
        return [jnp.dot(k_ref[kb * tk:(kb + 1) * tk, hh * HEAD_SLOT:(hh + 1) * HEAD_SLOT], qts[hh],
                        preferred_element_type=F32) for hh in heads]

    s_next = scores(0)
    m = acc = None
    for kb in range(nkb):
        s = s_next
        if kb + 1 < nkb:
            s_next = scores(kb + 1)
        bm = [jnp.max(x, axis=0, keepdims=True) for x in s]
        if m is None:
            m_new = bm
        else:
            m_new = [jnp.maximum(a, b_) for a, b_ in zip(m, bm)]
        p = [jnp.exp2(x - mn) for x, mn in zip(s, m_new)]
        pv = [jnp.dot(vt_ref[hh * V_SLOT:(hh + 1) * V_SLOT, kb * tk:(kb + 1) * tk], p[hh].astype(BF16),
                      preferred_element_type=F32) for hh in heads]
        if m is None:
            acc = pv
        else:
            acc = [jnp.exp2(a - b_) * x + y for a, b_, x, y in zip(m, m_new, acc, pv)]
        m = m_new
    o_ref[...] = jnp.concatenate([(a[:MLA_V] / a[MLA_V:MLA_V + 1]).T for a in acc], axis=-1).astype(o_ref.dtype)


def _attention(qt, k, vt, n_lat, n_ctx, need_ctx):
    b, t, _ = k.shape
    hp = MLA_HEADS // 2
    tq = 1024 if n_lat % 1024 == 0 else 128
    nv = MLA_HEADS * MLA_V
    att = pl.pallas_call(
        _attn_kernel,
        grid=(b, hp, n_lat // tq),
        in_specs=[pl.BlockSpec((None, 2 * HEAD_SLOT, tq), lambda i, h, j: (i, h, j)),
                  pl.BlockSpec((None, t, 2 * HEAD_SLOT), lambda i, h, j: (i, 0, h)),
                  pl.BlockSpec((None, 2 * V_SLOT, t), lambda i, h, j: (i, h, 0))],
        out_specs=pl.BlockSpec((None, tq, 2 * MLA_V), lambda i, h, j: (i, j, h)),
        out_shape=jax.ShapeDtypeStruct((b, n_lat, nv), BF16),
        compiler_params=_cparams(("parallel", "parallel", "arbitrary")),
        name="mla_attn_lat",
    )(qt, k, vt)
    if not need_ctx:
        return att, None
    cb = t // n_ctx - 1
    att_ctx = pl.pallas_call(
        _attn_kernel,
        grid=(b,),
        in_specs=[pl.BlockSpec((None, MLA_HEADS * HEAD_SLOT, n_ctx), lambda i: (i, 0, cb)),
                  pl.BlockSpec((None, n_ctx, MLA_HEADS * HEAD_SLOT), lambda i: (i, cb, 0)),
                  pl.BlockSpec((None, MLA_HEADS * V_SLOT, n_ctx), lambda i: (i, 0, cb))],
        out_specs=pl.BlockSpec((None, n_ctx, nv), lambda i: (i, 0, 0)),
        out_shape=jax.ShapeDtypeStruct((b, n_ctx, nv), BF16),
        compiler_params=_cparams(("parallel",)),
        name="mla_attn_ctx",
    )(qt, k, vt)
    return att, att_ctx


def _scan_maps(nlb, ncb):
    fwd = lambda j: jnp.where(j < ncb, nlb + j, j - ncb)
    bwd = lambda j: nlb + ncb - 1 - j
    return fwd, bwd


def _scan_blk(n_lat, n_ctx, limit):
    for blk in (256, 128, 64):
        if blk <= limit and n_lat % blk == 0 and n_ctx % blk == 0:
            return blk
    raise ValueError("sequence lengths must be multiples of the 64-token chunk")


def _gla_kernel(qf, kf, vf, lf, qb, kb, vb, lb, of, ob, sf, sb):
    @pl.when(pl.program_id(1) == 0)
    def _():
        sf[...] = jnp.zeros_like(sf)
        sb[...] = jnp.zeros_like(sb)

    hk, hv = GLA_HEADS * GLA_DK, GLA_HEADS * GLA_DV
    i32 = jnp.int32
    io = lambda shape, d: lax.broadcasted_iota(i32, shape, d)
    nbat, blk = qf.shape[0], qf.shape[1]
    nc = blk // CHUNK
    aj = io((CHUNK, GLA_HEADS * CHUNK), 1) % CHUNK
    ai = io((CHUNK, GLA_HEADS * CHUNK), 0)
    a_incl = (ai >= aj, ai <= aj)
    klane = io((CHUNK, hk), 1) // GLA_DK
    bi, bj = io((blk, blk), 0), io((blk, blk), 1)
    same_chunk = (bi // CHUNK) == (bj // CHUNK)

    bs, zs, cs, qs, ks, vs, bcs, bts = ([] for _ in range(8))
    for z, (q, k, v, lg) in enumerate(((qf, kf, vf, lf), (qb, kb, vb, lb))):
        tri = jnp.where(jnp.logical_and(same_chunk, (bi <= bj) if z else (bi >= bj)), 1.0, 0.0).astype(BF16)
        for bb in range(nbat):
            bcum = _dot_exact_lhs(tri, lg[bb, :, z * hk:(z + 1) * hk])
            for c in range(nc):
                r = slice(c * CHUNK, (c + 1) * CHUNK)
                last = c * CHUNK if z else (c + 1) * CHUNK - 1
                bs.append(bb), zs.append(z), cs.append(c)
                qs.append(q[bb, r, :]), ks.append(k[bb, r, :]), vs.append(v[bb, r, :])
                bcs.append(bcum[r, :]), bts.append(bcum[last:last + 1, :])

    heads = range(GLA_HEADS)
    hd = lambda x, h, w: x[:, h * w:(h + 1) * w]
    zero_k = jnp.zeros((CHUNK, hk), BF16)
    zero_v = jnp.zeros((CHUNK, GLA_DV), BF16)
    qds = _stage(lambda q, b_: q * (GLA_DK ** -0.5) * jnp.exp(b_), qs, bcs)
    kinv = _stage(lambda k, b_: (k * jnp.exp(-b_)).astype(BF16), ks, bcs)
    kbs = _stage(lambda x: jnp.concatenate([jnp.where(klane == h, x, zero_k) for h in heads], axis=0), kinv)
    a_s = _stage(lambda z, qd, kb_: jnp.where(a_incl[z], _dot_nt(qd, kb_), 0.0), zs, qds, kbs)
    vbs = _stage(lambda v: jnp.concatenate(
        [jnp.concatenate([hd(v, j, GLA_DV).astype(BF16) if j == h else zero_v for j in heads], axis=1)
         for h in heads], axis=0), vs)
    ois = _stage(_dot, a_s, vbs)
    kes = _stage(lambda k, b_, bt: k * jnp.exp(bt - b_), ks, bcs, bts)
    upds = _stage(lambda v, ke: [_dot_tn(hd(v, h, GLA_DV), hd(ke, h, GLA_DK)) for h in heads], vs, kes)
    ges = _stage(jnp.exp, bts)

    chain = {key: i for i, key in enumerate(zip(bs, zs, cs))}
    state = {(bb, z): [s_[bb, h] for h in heads] for z, s_ in enumerate((sf, sb)) for bb in range(nbat)}
    outs = (of, ob)
    for step in range(nc):
        for z in range(2):
            c = nc - 1 - step if z else step
            for bb in range(nbat):
                i = chain[(bb, z, c)]
                st = state[(bb, z)]
                inter = [_dot_nt(hd(qds[i], h, GLA_DK), st[h]) for h in heads]
                outs[z][bb, c * CHUNK:(c + 1) * CHUNK, :] = ois[i] + jnp.concatenate(inter, axis=1)
                state[(bb, z)] = [st[h] * hd(ges[i], h, GLA_DK) + upds[i][h] for h in heads]
    for (bb, z), st in state.items():
        for h in heads:
            (sf, sb)[z][bb, h] = st[h]


def _gla_scan(gq, gk, gv, lg, n_lat, n_ctx):
    b, t, hk = gq.shape
    hv = gv.shape[-1]
    blk = _scan_blk(n_lat, n_ctx, GLA_SCAN_BLOCK)
    nlb, ncb = n_lat // blk, n_ctx // blk
    fwd, bwd = _scan_maps(nlb, ncb)
    nbat = GLA_SCAN_BATCH if b % GLA_SCAN_BATCH == 0 else 1
    spec = lambda n, m: pl.BlockSpec((nbat, blk, n), lambda i, j: (i, m(j), 0))
    out = jax.ShapeDtypeStruct((b, t, hv), F32)
    return pl.pallas_call(
        _gla_kernel,
        grid=(b // nbat, nlb + ncb),
        in_specs=[spec(hk, fwd), spec(hk, fwd), spec(hv, fwd), spec(2 * hk, fwd),
                  spec(hk, bwd), spec(hk, bwd), spec(hv, bwd), spec(2 * hk, bwd)],
        out_specs=[spec(hv, fwd), spec(hv, bwd)],
        out_shape=[out, out],
        scratch_shapes=[pltpu.VMEM((nbat, GLA_HEADS, GLA_DV, GLA_DK), F32)] * 2,
        compiler_params=_cparams(("parallel", "arbitrary")),
        name="gla_scan",
    )(gq, gk, gv, lg, gq, gk, gv, lg)


def _cast_kernel(x_ref, o_ref):
    o_ref[...] = x_ref[...].astype(o_ref.dtype)


def _to_bf16(w):
    depth, r, c = w.shape
    tr = min(r, CAST_BLOCK_BYTES // (4 * c))
    assert r % tr == 0
    spec = pl.BlockSpec((None, tr, c), lambda l, i: (l, i, 0))
    return pl.pallas_call(
        _cast_kernel,
        grid=(depth, r // tr),
        in_specs=[spec],
        out_specs=spec,
        out_shape=jax.ShapeDtypeStruct(w.shape, BF16),
        compiler_params=_cparams(("parallel", "parallel")),
        name="cast_bf16",
    )(w)


def _head_norm_gate(of_ref, ob_ref, g_ref, on_ref, n_heads):
    x = of_ref[...] + ob_ref[...]
    g = g_ref[...]
    outs = []
    for h in range(n_heads):
        s = slice(h * LANES, (h + 1) * LANES)
        outs.append(_rms(x[:, s], on_ref[...]) * _silu(g[:, s]))
    return jnp.concatenate(outs, axis=-1)


def _merge_mlp_kernel(*refs, n_att, n_stream, nlb, final):
    atts, parts = refs[:n_att], refs[n_att:n_att + 3]
    n_parts = n_att + 3 + n_stream
    stream = refs[n_att + 3:n_parts]
    on_ref, wo_ref, mod_ref, n2_ref, w1_ref, w2_ref = refs[n_parts:n_parts + 6]
    o_ref = refs[-1]
    mod = mod_ref[...]
    if n_att:
        att = atts[0][...]
        if n_att == 2:
            att = jnp.where(pl.program_id(1) >= nlb, atts[1][...], att)
        na = att.shape[-1]
        y = _head_norm_gate(*parts, on_ref, GLA_HEADS)
        o = jnp.dot(att, wo_ref[0:na, :], preferred_element_type=F32) + _dot(y, wo_ref[na:, :])
    else:
        o = _dot(_head_norm_gate(*parts, on_ref, GDN_HEADS), wo_ref[...])
    h = _stream_block(stream, nlb) + mod[2:3] * o
    a = (_rms(h, n2_ref[...]) * (1.0 + mod[4:5]) + mod[3:4]).astype(BF16)
    ff = w1_ref.shape[1]
    hc = min(ff, MLP_HIDDEN_CHUNK)
    acc = None
    for c in range(ff // hc):
        u = jnp.maximum(jnp.dot(a, w1_ref[:, c * hc:(c + 1) * hc], preferred_element_type=F32), 0.0)
        t = _dot(u * u, w2_ref[c * hc:(c + 1) * hc, :])
        acc = t if acc is None else acc + t
    out = h + mod[5:6] * acc
    if final:
        out = _rms(out, refs[n_parts + 6][...])
    o_ref[...] = out


def _merge_mlp(atts, parts, on, wo, stream, mod, n2, w1, w2, layer, tm, n_lat, n_tok, final_norm=None):
    b, d = stream[0].shape[0], stream[0].shape[-1]
    nlb = n_lat // tm
    tok = lambda n: pl.BlockSpec((None, tm, n), lambda i, j: (i, j, 0))
    once = lambda a: pl.BlockSpec(a.shape, lambda *_: (0,) * a.ndim, pipeline_mode=pl.Buffered(1))
    slab = lambda a: pl.BlockSpec((None,) + a.shape[1:], lambda *_: (layer, 0, 0), pipeline_mode=pl.Buffered(1))
    att_maps = (lambda i, j: (i, jnp.minimum(j, nlb - 1), 0), lambda i, j: (i, jnp.maximum(j - nlb, 0), 0))
    args = [*atts, *parts, *stream, on, wo, mod, n2, w1, w2]
    specs = [pl.BlockSpec((None, tm, a.shape[-1]), m) for a, m in zip(atts, att_maps)] + [
        tok(p.shape[-1]) for p in parts] + _stream_specs(stream, tm, nlb) + [
        _full(on.shape), once(wo),
        pl.BlockSpec((None, None, N_MOD, d), lambda i, j: (i, j // nlb, 0, 0)),
        _full(n2.shape), slab(w1), slab(w2)]
    if final_norm is not None:
        args.append(final_norm)
        specs.append(_full(final_norm.shape))
    return pl.pallas_call(
        functools.partial(_merge_mlp_kernel, n_att=len(atts), n_stream=len(stream), nlb=nlb,
                          final=final_norm is not None),
        grid=(b, n_tok // tm),
        in_specs=specs,
        out_specs=tok(d),
        out_shape=jax.ShapeDtypeStruct((b, n_tok, d), F32),
        compiler_params=_cparams(("parallel", "parallel")),
        name="merge_mlp",
    )(*args)


_O_QKV = 2 * GDN_HEADS * GDN_DK + GDN_HEADS * GDN_DV
_O_QK = 2 * GDN_HEADS * GDN_DK
_O_G = GDN_HEADS * GDN_DV
_O_N = _O_QKV + _O_G + LANES
HALO = 8


def _odd_proj_kernel(h_ref, hp_ref, hn_ref, mod_ref, n1_ref, win_ref, cw_ref, e_ref, et_ref, rs_ref,
                     al_ref, dt_ref, q_out, k_out, v_out, g_out, gt_out, *, nlb, nb):
    j = pl.program_id(1)
    mod = mod_ref[...]
    n1 = n1_ref[...]
    pre = lambda x: (_rms(x, n1) * (1.0 + mod[1:2]) + mod[0:1]).astype(BF16)
    a = pre(h_ref[...])
    first = jnp.logical_or(j == 0, j == nlb)
    last = jnp.logical_or(j == nlb - 1, j == nb - 1)
    zero_halo = jnp.zeros((HALO, a.shape[1]), BF16)
    a_ext = jnp.concatenate([a, jnp.where(last, zero_halo, pre(hn_ref[...])),
                             jnp.where(first, zero_halo, pre(hp_ref[...]))], axis=0)
    tm = a.shape[0]
    te = tm + 2 * HALO
    hq = GDN_HEADS * GDN_DK
    gw = hq
    mm = lambda lhs, c0, c1: jnp.dot(lhs, win_ref[:, c0:c1], preferred_element_type=F32)
    project = lambda g: mm(a_ext, g * gw, (g + 1) * gw)

    pending = project(0)
    for g in range(_O_QKV // gw):
        xe = pending
        if (g + 1) * gw < _O_QKV:
            pending = project(g + 1)
        else:
            y_tail = mm(a, _O_QKV, _O_N)
        cw = cw_ref[:, g * gw:(g + 1) * gw]
        s = _silu(cw[0:1] * pltpu.roll(xe, 1, axis=0)[:tm] + cw[1:2] * xe[:tm]
                  + cw[2:3] * pltpu.roll(xe, te - 1, axis=0)[:tm])
        if g < 2:
            ss = _dot(s * s, e_ref[g * gw:(g + 1) * gw, :])
            r = lax.rsqrt(ss + EPS) * rs_ref[...]
            et = et_ref[:, g * gw:(g + 1) * gw]
            rf = sum(jnp.dot(p, et, preferred_element_type=F32) for p in _split(r, 2))
            (q_out, k_out)[g][...] = s * rf
        else:
            v_out[:, (g - 2) * gw:(g - 1) * gw] = s
    g_out[...] = y_tail[:, 0:_O_G]
    t = y_tail[:, _O_G:]
    log_a = -jnp.exp(al_ref[...]) * _softplus(t + dt_ref[...])
    lane = lax.broadcasted_iota(jnp.int32, t.shape, 1)
    gt_out[...] = jnp.where(lane < 2 * GDN_HEADS, log_a, _sigmoid(t))


def _odd_project(h, mod, n1, w_in, conv_w, a_log, dt_bias, tm, n_lat):
    b, t, d = h.shape
    nlb, nb = n_lat // tm, t // tm
    win = jnp.concatenate([w_in, jnp.zeros((d, _O_N - w_in.shape[1]), F32)], axis=1).astype(BF16)
    e = (jnp.arange(_O_QK)[:, None] // GDN_DK == jnp.arange(LANES)[None, :]).astype(BF16)
    et = e.T
    rs = jnp.concatenate([jnp.full((GDN_HEADS,), GDN_DK ** -0.5, F32), jnp.ones((LANES - GDN_HEADS,), F32)])[None, :]
    padl = lambda v: jnp.concatenate([v.reshape(-1), jnp.zeros((LANES - v.size,), F32)])[None, :]
    al, dt = padl(a_log), padl(dt_bias)
    tok = lambda n: pl.BlockSpec((None, tm, n), lambda i, j: (i, j, 0))
    hb = tm // HALO
    outs = [GDN_HEADS * GDN_DK, GDN_HEADS * GDN_DK, GDN_HEADS * GDN_DV, _O_G, LANES]
    return pl.pallas_call(
        functools.partial(_odd_proj_kernel, nlb=nlb, nb=nb),
        grid=(b, nb),
        in_specs=[tok(d),
                  pl.BlockSpec((None, HALO, d), lambda i, j: (i, jnp.maximum(j * hb - 1, 0), 0)),
                  pl.BlockSpec((None, HALO, d), lambda i, j: (i, jnp.minimum((j + 1) * hb, t // HALO - 1), 0)),
                  pl.BlockSpec((None, None, N_MOD, d), lambda i, j: (i, j // nlb, 0, 0)),
                  _full(n1.shape), _full(win.shape), _full(conv_w.shape), _full(e.shape), _full(et.shape),
                  _full(rs.shape), _full(al.shape), _full(dt.shape)],
        out_specs=[tok(n) for n in outs],
        out_shape=[jax.ShapeDtypeStruct((b, t, n), F32) for n in outs],
        compiler_params=_cparams(("parallel", "parallel")),
        name="odd_proj",
    )(h, h, h, mod, n1, win, conv_w, e, et, rs, al, dt)


def _gdn_kernel(qf, kf, vf, gcf, grf, qb, kb, vb, gcb, grb, of, ob, s_ref):
    @pl.when(pl.program_id(2) == 0)
    def _():
        s_ref[...] = jnp.zeros_like(s_ref)

    blk = qf.shape[0]
    nc = blk // CHUNK
    nh = qf.shape[1] // GDN_DK
    i32 = jnp.int32
    ri = lax.broadcasted_iota(i32, (CHUNK, CHUNK), 0)
    ci = lax.broadcasted_iota(i32, (CHUNK, CHUNK), 1)
    same = [(ri // w) == (ci // w) for w in (2, 4, 8, 16, 32)]
    lvl = [same[0]] + [jnp.logical_and(same[i], jnp.logical_not(same[i - 1])) for i in range(1, 5)]
    lvl.append(jnp.logical_not(same[4]))
    eye = jnp.where(ri == ci, 1.0, 0.0).astype(F32)
    incl = (ri >= ci, ri <= ci)
    strict = (ri > ci, ri < ci)
    bi = lax.broadcasted_iota(i32, (blk, blk), 0)
    bj = lax.broadcasted_iota(i32, (blk, blk), 1)
    same_chunk = (bi // CHUNK) == (bj // CHUNK)

    zs, cs, hs, qs, ks, vs, gbs, bbs, grows, brows, gls = ([] for _ in range(11))
    sq = (CHUNK, CHUNK)
    for z, (q, k, v, gc, gr) in enumerate(((qf, kf, vf, gcf, grf), (qb, kb, vb, gcb, grb))):
        tri = jnp.where(jnp.logical_and(same_chunk, (bi <= bj) if z else (bi >= bj)), 1.0, 0.0).astype(BF16)
        gcv, grv = gc[...], gr[...]
        cum_col = _dot_exact_lhs(tri, gcv)
        cum_row = _dot_exact_lhs(tri, grv, nt=True)
        for c in range(nc):
            r = slice(c * CHUNK, (c + 1) * CHUNK)
            last = c * CHUNK if z else (c + 1) * CHUNK - 1
            for hh in range(nh):
                idx = nh * z + hh
                zs.append(z), cs.append(c), hs.append(hh)
                qs.append(q[r, hh * GDN_DK:(hh + 1) * GDN_DK])
                ks.append(k[r, hh * GDN_DK:(hh + 1) * GDN_DK])
                vs.append(v[r, hh * GDN_DV:(hh + 1) * GDN_DV])
                gbs.append(jnp.broadcast_to(cum_col[r, idx:idx + 1], sq))
                bbs.append(jnp.broadcast_to(gcv[r, 2 * nh + idx:2 * nh + idx + 1], sq))
                grows.append(cum_row[idx:idx + 1, r])
                brows.append(grv[2 * nh + idx:2 * nh + idx + 1, r])
                gls.append(cum_col[last:last + 1, idx:idx + 1])

    decays = _stage(lambda z, gb, gr_: jnp.where(incl[z], jnp.exp(jnp.where(incl[z], gb - gr_, 0.0)), 0.0),
                    zs, gbs, grows)
    qkk = _stage(lambda q, k: _dot_nt(jnp.concatenate([q, k], axis=0), k), qs, ks)
    ns = _stage(lambda z, x, bb, d_: jnp.where(strict[z], x[CHUNK:] * bb * d_, 0.0), zs, qkk, bbs, decays)
    ts = _stage(lambda n: eye - jnp.where(lvl[0], n, 0.0), ns)
    for off in lvl[1:]:
        xs = _stage(lambda t, n: _dot(t, jnp.where(off, n, 0.0)), ts, ns)
        ts = _stage(lambda t, x: t - _dot(x, t), ts, xs)
    egs = _stage(jnp.exp, gbs)
    uws = _stage(lambda t, br, v, k, e_: _dot(t * br, jnp.concatenate([v, k * e_], axis=-1)),
                 ts, brows, vs, ks, egs)
    qks = _stage(lambda x, d_: x[:CHUNK] * d_, qkk, decays)
    kes = _stage(lambda k, gl, gb: k * jnp.exp(gl - gb), ks, gls, gbs)
    kuws = _stage(_dot_tn, kes, uws)
    quws = _stage(_dot, qks, uws)
    lhs = _stage(lambda q, e_, qu, ku: jnp.concatenate([q * e_ - qu[:, GDN_DV:], -ku[:, GDN_DV:]], axis=0),
                 qs, egs, quws, kuws)
    ges = _stage(jnp.exp, gls)

    chain = {(z, c, hh): i for i, (z, c, hh) in enumerate(zip(zs, cs, hs))}
    state = {(z, hh): s_ref[z, hh] for z in range(2) for hh in range(nh)}
    outs = (of, ob)
    for step in range(nc):
        for z in range(2):
            c = nc - 1 - step if z else step
            for hh in range(nh):
                i = chain[(z, c, hh)]
                s = state[(z, hh)]
                r = _dot(lhs[i], s)
                outs[z][c * CHUNK:(c + 1) * CHUNK, hh * GDN_DV:(hh + 1) * GDN_DV] = (
                    r[:CHUNK] + quws[i][:, :GDN_DV])
                state[(z, hh)] = s * ges[i] + (kuws[i][:, :GDN_DV] + r[CHUNK:])
    for (z, hh), s in state.items():
        s_ref[z, hh] = s


def _gdn_scan(q, k, v, gates, n_lat, n_ctx):
    b, t, _ = q.shape
    nh = GDN_SCAN_HEADS
    ng = GDN_HEADS // nh
    g = gates[..., :4 * GDN_HEADS].reshape(b, t, 2, 2, ng, nh)
    gcol = g.transpose(0, 4, 1, 2, 3, 5).reshape(b, ng, t, 4 * nh)
    grow = gcol.transpose(0, 1, 3, 2)
    blk = _scan_blk(n_lat, n_ctx, GDN_SCAN_BLOCK)
    nlb, ncb = n_lat // blk, n_ctx // blk
    fwd, bwd = _scan_maps(nlb, ncb)
    d = None
    tok = lambda n, m: pl.BlockSpec((d, blk, n), lambda i, h, j: (i, m(j), h))
    gcs = lambda m: pl.BlockSpec((d, d, blk, 4 * nh), lambda i, h, j: (i, h, m(j), 0))
    grs = lambda m: pl.BlockSpec((d, d, 4 * nh, blk), lambda i, h, j: (i, h, 0, m(j)))
    out = jax.ShapeDtypeStruct((b, t, GDN_HEADS * GDN_DV), F32)
    dk2, dv2 = nh * GDN_DK, nh * GDN_DV
    return pl.pallas_call(
        _gdn_kernel,
        grid=(b, ng, nlb + ncb),
        in_specs=[tok(dk2, fwd), tok(dk2, fwd), tok(dv2, fwd), gcs(fwd), grs(fwd),
                  tok(dk2, bwd), tok(dk2, bwd), tok(dv2, bwd), gcs(bwd), grs(bwd)],
        out_specs=[tok(dv2, fwd), tok(dv2, bwd)],
        out_shape=[out, out],
        scratch_shapes=[pltpu.VMEM((2, nh, GDN_DK, GDN_DV), F32)],
        compiler_params=_cparams(("parallel", "parallel", "arbitrary")),
        name="gdn_scan",
    )(q, k, v, gcol, grow, q, k, v, gcol, grow)


def kernel(x, c, ctx, c_ctx, ada_w, ada_b, norm1_w, norm2_w, mlp_w1, mlp_w2, even_w_in, mla_q_norm,
           mla_w_uq, mla_kv_norm, mla_w_ukv, gla_gate_up, gla_gate_bias, gla_o_norm, even_w_out,
           gdn_w_in, gdn_conv_w, gdn_a_log, gdn_dt_bias, gdn_o_norm, gdn_w_out, final_norm):
    b, n_lat, d = x.shape
    n_ctx = ctx.shape[1]
    depth = ada_w.shape[0]
    tm = 256 if (n_lat % 256 == 0 and n_ctx % 256 == 0) else 128
    assert n_lat % tm == 0 and n_ctx % tm == 0 and n_lat % n_ctx == 0 and n_lat % GRID_W == 0

    rows = -(-(b + 1) // 8) * 8
    cc = jnp.concatenate([c, c_ctx[None, :], jnp.zeros((rows - b - 1, d), F32)], axis=0)
    mods = _ada_all(cc, ada_w, ada_b).reshape(depth, rows, N_MOD, d)
    mod_tok = jnp.stack([mods[:, :b], jnp.broadcast_to(mods[:, b:b + 1], (depth, b, N_MOD, d))], axis=2)

    stream = (x, ctx)
    tabs = _rope_tables(n_lat, n_ctx)
    w1_bf, w2_bf = _to_bf16(mlp_w1), _to_bf16(mlp_w2)
    row = lambda v: v[None, :]
    for layer in range(depth):
        need_ctx = layer < depth - 1
        n_tok = n_lat + n_ctx if need_ctx else n_lat
        mod = mod_tok[layer]
        n1 = row(norm1_w[layer])
        i = layer // 2
        if layer % 2 == 0:
            ew = _even_weights(even_w_in[i], mla_w_uq[i], mla_w_ukv[i], gla_gate_up[i], gla_gate_bias[i])
            q, k, v, gq, gk, gv, gg, lg = _even_project(
                stream, mod, n1, ew, row(mla_q_norm[i]), row(mla_kv_norm[i]), tabs, tm, n_lat)
            atts = [a for a in _attention(q, k, v, n_lat, n_ctx, need_ctx) if a is not None]
            o_f, o_b = _gla_scan(gq, gk, gv, lg, n_lat, n_ctx)
            parts, on, wo = [o_f, o_b, gg], gla_o_norm[i], even_w_out[i]
        else:
            q, k, v, g, gates = _odd_project(stream[0], mod, n1, gdn_w_in[i], gdn_conv_w[i], gdn_a_log[i],
                                             gdn_dt_bias[i], tm, n_lat)
            o_f, o_b = _gdn_scan(q, k, v, gates, n_lat, n_ctx)
            atts, parts, on, wo = [], [o_f, o_b, g], gdn_o_norm[i], gdn_w_out[i]
        stream = (_merge_mlp(atts, parts, row(on), wo.astype(BF16), stream, mod, row(norm2_w[layer]), w1_bf, w2_bf,
                             layer, tm, n_lat, n_tok, final_norm=None if need_ctx else row(final_norm)),)
    return stream[0]
```

```python
import functools

import jax
import jax.numpy as jnp
from jax import lax
from jax.experimental import pallas as pl
from jax.experimental.pallas import tpu as pltpu

F32 = jnp.float32
BF16 = jnp.bfloat16
EPS = 1e-6
LOG2E = 1.4426950408889634

GRID_W = 64
N_MOD = 6
MLA_HEADS = 8
MLA_NOPE = 64
MLA_ROPE = 32
MLA_V = 64
MLA_QK = MLA_NOPE + MLA_ROPE
MLA_Q_RANK = 384
MLA_KV_RANK = 256
ROPE_THETA = 10000.0
GLA_HEADS = 4
GLA_DK = 64
GLA_DV = 128
GLA_GATE_RANK = 16
GLA_GATE_NORM = 16.0
GDN_HEADS = 8
GDN_DK = 64
GDN_DV = 128
CHUNK = 64
MLP_HIDDEN_CHUNK = 1024
MERGE_BATCH = 2
CAST_BLOCK_BYTES = 8 * 1024 * 1024
GLA_SCAN_BLOCK = 256
GLA_SCAN_BATCH = 4
GDN_SCAN_HEADS = 8
GDN_SCAN_BLOCK = 128
LANES = 128
HEAD_SLOT = 128
V_SLOT = 80
VMEM_LIMIT = 56 * 1024 * 1024


def _cparams(sem):
    return pltpu.CompilerParams(dimension_semantics=sem, vmem_limit_bytes=VMEM_LIMIT)


def _dot(a, b):
    return jnp.dot(a.astype(BF16), b.astype(BF16), preferred_element_type=F32)


def _dot_nt(a, b):
    return lax.dot_general(a.astype(BF16), b.astype(BF16), (((1,), (1,)), ((), ())),
                           preferred_element_type=F32)


def _dot_tn(a, b):
    return lax.dot_general(a.astype(BF16), b.astype(BF16), (((0,), (0,)), ((), ())),
                           preferred_element_type=F32)


def _split(x, n):
    out = []
    for _ in range(n - 1):
        p = x.astype(BF16)
        out.append(p)
        x = x - p.astype(F32)
    out.append(x.astype(BF16))
    return out


def _dot_exact_lhs(m_bf, x, nt=False):
    acc = None
    for p in _split(x, 2):
        if nt:
            t = lax.dot_general(p, m_bf, (((1,), (1,)), ((), ())), preferred_element_type=F32)
        else:
            t = jnp.dot(m_bf, p, preferred_element_type=F32)
        acc = t if acc is None else acc + t
    return acc


def _rms(x, w):
    return x * lax.rsqrt(jnp.mean(x * x, axis=-1, keepdims=True) + EPS) * w


def _sigmoid(x):
    return 1.0 / (1.0 + jnp.exp(-x))


def _silu(x):
    return x * _sigmoid(x)


def _softplus(x):
    return jnp.maximum(x, 0.0) + jnp.log1p(jnp.exp(-jnp.abs(x)))


def _tile_lanes(x, n):
    return jnp.concatenate([x] * n, axis=-1)


def _stream_specs(stream, tm, nlb):
    d = stream[0].shape[-1]
    if len(stream) == 1:
        return [pl.BlockSpec((None, tm, d), lambda i, j: (i, j, 0))]
    return [pl.BlockSpec((None, tm, d), lambda i, j: (i, jnp.minimum(j, nlb - 1), 0)),
            pl.BlockSpec((None, tm, d), lambda i, j: (i, jnp.maximum(j - nlb, 0), 0))]


def _stream_block(refs, nlb):
    if len(refs) == 1:
        return refs[0][...]
    return jnp.where(pl.program_id(1) >= nlb, refs[1][...], refs[0][...])


def _stage(fn, *cols):
    return [fn(*a) for a in zip(*cols)]


def _ada_kernel(cc_ref, w_ref, b_ref, o_ref):
    o_ref[0] = _dot(_silu(cc_ref[...]), w_ref[0]) + b_ref[0]


def _ada_all(cc, ada_w, ada_b):
    depth, d, n = ada_w.shape
    rows = cc.shape[0]
    tn = 1024
    return pl.pallas_call(
        _ada_kernel,
        grid=(depth, n // tn),
        in_specs=[pl.BlockSpec((rows, d), lambda l, j: (0, 0)),
                  pl.BlockSpec((1, d, tn), lambda l, j: (l, 0, j)),
                  pl.BlockSpec((1, 1, tn), lambda l, j: (l, 0, j))],
        out_specs=pl.BlockSpec((1, rows, tn), lambda l, j: (l, 0, j)),
        out_shape=jax.ShapeDtypeStruct((depth, rows, n), F32),
        compiler_params=_cparams(("parallel", "parallel")),
        name="ada_mod",
    )(cc, ada_w, ada_b.reshape(depth, 1, n))


def _col_groups(*widths):
    edges = [0]
    for w in widths:
        edges.append(edges[-1] + w)
    return list(zip(edges[:-1], edges[1:]))


_E_CQ, _E_CKV, _E_ROT, _E_GQ, _E_GK, _E_GV, _E_GG = _col_groups(
    MLA_Q_RANK, MLA_KV_RANK, LANES, GLA_HEADS * GLA_DK, GLA_HEADS * GLA_DK, GLA_HEADS * GLA_DV, GLA_HEADS * GLA_DV)


def _even_proj_kernel(*refs, n_stream, nlb):
    (mod_ref, n1_ref, win_ref, qn_ref, wq_ref, kvn_ref, wk_ref, wv_ref, wg_ref, gb_ref, c_ref, s_ref,
     ctt_ref, stt_ref, q_out, k_out, v_out, gq_out, gk_out, gv_out, gg_out, lg_out) = refs[n_stream:]
    mod = mod_ref[...]
    a = _rms(_stream_block(refs[:n_stream], nlb), n1_ref[...]) * (1.0 + mod[1:2]) + mod[0:1]
    y = _dot(a, win_ref[...])
    sl = lambda r: y[:, r[0]:r[1]]
    ct, st = c_ref[...], s_ref[...]
    nh = MLA_HEADS
    qq = _dot_nt(wq_ref[...], _rms(sl(_E_CQ), qn_ref[...]))
    c32, s32 = ctt_ref[...], stt_ref[...]
    rot0 = nh * MLA_QK
    zero_pad = jnp.zeros((HEAD_SLOT - MLA_QK, qq.shape[1]), F32)
    pieces = []
    for h in range(nh):
        nope = qq[h * MLA_QK:h * MLA_QK + MLA_NOPE]
        pe = qq[h * MLA_QK + MLA_NOPE:(h + 1) * MLA_QK]
        pe_rot = qq[rot0 + h * MLA_ROPE:rot0 + (h + 1) * MLA_ROPE]
        pieces += [nope, pe * c32 + pe_rot * s32, zero_pad]
    q_out[...] = jnp.concatenate(pieces, axis=0).astype(q_out.dtype)
    ckvn = _rms(sl(_E_CKV), kvn_ref[...])
    yr = sl(_E_ROT)
    kpe = pltpu.roll(yr, MLA_NOPE, axis=1) * ct + pltpu.roll(yr, MLA_NOPE - MLA_ROPE, axis=1) * st
    k_out[...] = (_dot(ckvn, wk_ref[...]) + _tile_lanes(kpe, nh)).astype(k_out.dtype)
    vt = _dot_nt(wv_ref[...], ckvn)
    pad_rows = lax.broadcasted_iota(jnp.int32, (V_SLOT - MLA_V, vt.shape[1]), 0)
    ones_row = jnp.where(pad_rows == 0, 1.0, 0.0)
    v_out[...] = jnp.concatenate(
        [x for h in range(nh) for x in (vt[h * MLA_V:(h + 1) * MLA_V], ones_row)], axis=0).astype(v_out.dtype)
    gq_out[...] = sl(_E_GQ)
    gk_out[...] = sl(_E_GK)
    gv_out[...] = sl(_E_GV)
    gg_out[...] = sl(_E_GG)
    logit = _dot(yr, wg_ref[...]) + gb_ref[...]
    lg_out[...] = (jnp.minimum(logit, 0.0) - jnp.log1p(jnp.exp(-jnp.abs(logit)))) * (1.0 / GLA_GATE_NORM)


def _rot_half_cols(w):
    f = MLA_ROPE // 4
    return jnp.concatenate([-w[..., f:2 * f], w[..., 0:f], -w[..., 3 * f:4 * f], w[..., 2 * f:3 * f]], axis=-1)


def _even_weights(w_in, w_uq, w_ukv, gate_up, gate_bias):
    d = w_in.shape[0]
    hk_, hv_ = GLA_HEADS * GLA_DK, GLA_HEADS * GLA_DV
    cq, ckv, kpe, gq, gk, gv, gg, glow = (
        w_in[:, a:b_] for a, b_ in _col_groups(MLA_Q_RANK, MLA_KV_RANK, MLA_ROPE, hk_, hk_, hv_, hv_, 2 * GLA_GATE_RANK))
    lowr = 2 * GLA_GATE_RANK
    win = jnp.concatenate([cq, ckv, kpe, _rot_half_cols(kpe), glow, jnp.zeros((d, LANES - 2 * MLA_ROPE - lowr), F32),
                           gq, gk, gv, gg], axis=1).astype(BF16)
    r = w_uq.shape[0]
    wq = (w_uq * (MLA_QK ** -0.5 * LOG2E)).reshape(r, MLA_HEADS, MLA_QK)
    wq_rot = _rot_half_cols(wq[..., MLA_NOPE:])
    wq_all = jnp.concatenate([wq.reshape(r, -1), wq_rot.reshape(r, -1)], axis=1).T.astype(BF16)
    rk = w_ukv.shape[0]
    wkv = w_ukv.reshape(rk, MLA_HEADS, MLA_NOPE + MLA_V)
    wk = jnp.concatenate([wkv[..., :MLA_NOPE], jnp.zeros((rk, MLA_HEADS, HEAD_SLOT - MLA_NOPE), F32)],
                         axis=-1).reshape(rk, -1).astype(BF16)
    wv = wkv[..., MLA_NOPE:].reshape(rk, -1).T.astype(BF16)
    hk = GLA_HEADS * GLA_DK
    g0 = 2 * MLA_ROPE
    wg = jnp.zeros((LANES, 2 * hk), F32)
    wg = wg.at[g0:g0 + GLA_GATE_RANK, 0:hk].set(gate_up[0])
    wg = wg.at[g0 + GLA_GATE_RANK:g0 + 2 * GLA_GATE_RANK, hk:].set(gate_up[1])
    gb = jnp.concatenate([gate_bias[0], gate_bias[1]])[None, :]
    return win, wq_all, wk, wv, wg.astype(BF16), gb


def _rope_tables(n_lat, n_ctx):
    rows = n_lat // GRID_W
    row = jnp.repeat(jnp.arange(rows, dtype=F32), GRID_W)
    col = jnp.tile(jnp.arange(GRID_W, dtype=F32), rows)
    axis_dim = MLA_ROPE // 2
    inv_freq = ROPE_THETA ** (-jnp.arange(0, axis_dim, 2, dtype=F32) / axis_dim)
    ar, ac = row[:, None] * inv_freq, col[:, None] * inv_freq
    c32 = jnp.concatenate([jnp.cos(ar), jnp.cos(ar), jnp.cos(ac), jnp.cos(ac)], axis=1)
    s32 = jnp.concatenate([jnp.sin(ar), jnp.sin(ar), jnp.sin(ac), jnp.sin(ac)], axis=1)
    c32 = jnp.concatenate([c32, jnp.ones((n_ctx, MLA_ROPE), F32)], axis=0)
    s32 = jnp.concatenate([s32, jnp.zeros((n_ctx, MLA_ROPE), F32)], axis=0)
    slot = lambda x: jnp.pad(x, ((0, 0), (MLA_NOPE, HEAD_SLOT - MLA_QK)))
    return slot(c32), slot(s32), c32.T, s32.T


def _full(shape):
    nd = len(shape)
    return pl.BlockSpec(shape, lambda *_: (0,) * nd)


def _even_project(stream, mod, n1, ew, qn, kvn, tabs, tm, n_lat):
    b, d = stream[0].shape[0], stream[0].shape[-1]
    t = sum(s.shape[1] for s in stream)
    win, wq, wk, wv, wg, gb = ew
    ct, st, ctt, stt = tabs
    nlb = n_lat // tm
    tok = lambda n: pl.BlockSpec((None, tm, n), lambda i, j: (i, j, 0))
    tab = pl.BlockSpec((tm, HEAD_SLOT), lambda i, j: (j, 0))
    tabt = pl.BlockSpec((MLA_ROPE, tm), lambda i, j: (0, j))
    hk, hv = GLA_HEADS * GLA_DK, GLA_HEADS * GLA_DV
    feat = lambda n: pl.BlockSpec((None, n, tm), lambda i, j: (i, 0, j))
    tshape = lambda n, dt: jax.ShapeDtypeStruct((b, t, n), dt)
    fshape = lambda n, dt: jax.ShapeDtypeStruct((b, n, t), dt)
    nq, nv = MLA_HEADS * HEAD_SLOT, MLA_HEADS * V_SLOT
    return pl.pallas_call(
        functools.partial(_even_proj_kernel, n_stream=len(stream), nlb=nlb),
        grid=(b, t // tm),
        in_specs=_stream_specs(stream, tm, nlb) + [
                  pl.BlockSpec((None, None, N_MOD, d), lambda i, j: (i, j // nlb, 0, 0)),
                  _full(n1.shape), _full(win.shape), _full(qn.shape), _full(wq.shape), _full(kvn.shape),
                  _full(wk.shape), _full(wv.shape), _full(wg.shape), _full(gb.shape), tab, tab, tabt, tabt],
        out_specs=[feat(nq), tok(nq), feat(nv), tok(hk), tok(hk), tok(hv), tok(hv), tok(2 * hk)],
        out_shape=[fshape(nq, BF16), tshape(nq, BF16), fshape(nv, BF16), tshape(hk, F32), tshape(hk, F32),
                   tshape(hv, F32), tshape(hv, F32), tshape(2 * hk, F32)],
        compiler_params=_cparams(("parallel", "parallel")),
        name="even_proj",
    )(*stream, mod, n1, win, qn, wq, kvn, wk, wv, wg, gb, ct, st, ctt, stt)


def _attn_kernel(qt_ref, k_ref, vt_ref, o_ref):
    tk_all = k_ref.shape[0]
    tk = 256 if tk_all % 256 == 0 else 128
    heads = range(qt_ref.shape[0] // HEAD_SLOT)
    nkb = tk_all // tk
    qts = [qt_ref[hh * HEAD_SLOT:(hh + 1) * HEAD_SLOT, :] for hh in heads]

    def scores(kb):
        return [jnp.dot(k_ref[kb * tk:(kb + 1) * tk, hh * HEAD_SLOT:(hh + 1) * HEAD_SLOT], qts[hh],
                        preferred_element_type=F32) for hh in heads]

    s_next = scores(0)
    m = acc = None
    for kb in range(nkb):
        s = s_next
        if kb + 1 < nkb:
            s_next = scores(kb + 1)
        bm = [jnp.max(x, axis=0, keepdims=True) for x in s]
        if m is None:
            m_new = bm
        else:
            m_new = [jnp.maximum(a, b_) for a, b_ in zip(m, bm)]
        p = [jnp.exp2(x - mn) for x, mn in zip(s, m_new)]
        pv = [jnp.dot(vt_ref[hh * V_SLOT:(hh + 1) * V_SLOT, kb * tk:(kb + 1) * tk], p[hh].astype(BF16),
                      preferred_element_type=F32) for hh in heads]
        if m is None:
            acc = pv
        else:
            acc = [jnp.exp2(a - b_) * x + y for a, b_, x, y in zip(m, m_new, acc, pv)]
        m = m_new
    o_ref[...] = jnp.concatenate([(a[:MLA_V] / a[MLA_V:MLA_V + 1]).T for a in acc], axis=-1).astype(o_ref.dtype)


def _attention(qt, k, vt, n_lat, n_ctx, need_ctx):
    b, t, _ = k.shape
    hp = MLA_HEADS // 2
    tq = 1024 if n_lat % 1024 == 0 else 128
    nv = MLA_HEADS * MLA_V
    att = pl.pallas_call(
        _attn_kernel,
        grid=(b, hp, n_lat // tq),
        in_specs=[pl.BlockSpec((None, 2 * HEAD_SLOT, tq), lambda i, h, j: (i, h, j)),
                  pl.BlockSpec((None, t, 2 * HEAD_SLOT), lambda i, h, j: (i, 0, h)),
                  pl.BlockSpec((None, 2 * V_SLOT, t), lambda i, h, j: (i, h, 0))],
        out_specs=pl.BlockSpec((None, tq, 2 * MLA_V), lambda i, h, j: (i, j, h)),
        out_shape=jax.ShapeDtypeStruct((b, n_lat, nv), BF16),
        compiler_params=_cparams(("parallel", "parallel", "arbitrary")),
        name="mla_attn_lat",
    )(qt, k, vt)
    if not need_ctx:
        return att, None
    cb = t // n_ctx - 1
    att_ctx = pl.pallas_call(
        _attn_kernel,
        grid=(b,),
        in_specs=[pl.BlockSpec((None, MLA_HEADS * HEAD_SLOT, n_ctx), lambda i: (i, 0, cb)),
                  pl.BlockSpec((None, n_ctx, MLA_HEADS * HEAD_SLOT), lambda i: (i, cb, 0)),
                  pl.BlockSpec((None, MLA_HEADS * V_SLOT, n_ctx), lambda i: (i, 0, cb))],
        out_specs=pl.BlockSpec((None, n_ctx, nv), lambda i: (i, 0, 0)),
        out_shape=jax.ShapeDtypeStruct((b, n_ctx, nv), BF16),
        compiler_params=_cparams(("parallel",)),
        name="mla_attn_ctx",
    )(qt, k, vt)
    return att, att_ctx


def _scan_maps(nlb, ncb):
    fwd = lambda j: jnp.where(j < ncb, nlb + j, j - ncb)
    bwd = lambda j: nlb + ncb - 1 - j
    return fwd, bwd


def _scan_blk(n_lat, n_ctx, limit):
    for blk in (256, 128, 64):
        if blk <= limit and n_lat % blk == 0 and n_ctx % blk == 0:
            return blk
    raise ValueError("sequence lengths must be multiples of the 64-token chunk")


def _gla_kernel(qf, kf, vf, lf, qb, kb, vb, lb, of, ob, sf, sb):
    @pl.when(pl.program_id(1) == 0)
    def _():
        sf[...] = jnp.zeros_like(sf)
        sb[...] = jnp.zeros_like(sb)

    hk, hv = GLA_HEADS * GLA_DK, GLA_HEADS * GLA_DV
    i32 = jnp.int32
    io = lambda shape, d: lax.broadcasted_iota(i32, shape, d)
    nbat, blk = qf.shape[0], qf.shape[1]
    nc = blk // CHUNK
    aj = io((CHUNK, GLA_HEADS * CHUNK), 1) % CHUNK
    ai = io((CHUNK, GLA_HEADS * CHUNK), 0)
    a_incl = (ai >= aj, ai <= aj)
    klane = io((CHUNK, hk), 1) // GLA_DK
    bi, bj = io((blk, blk), 0), io((blk, blk), 1)
    same_chunk = (bi // CHUNK) == (bj // CHUNK)

    bs, zs, cs, qs, ks, vs, bcs, bts = ([] for _ in range(8))
    for z, (q, k, v, lg) in enumerate(((qf, kf, vf, lf), (qb, kb, vb, lb))):
        tri = jnp.where(jnp.logical_and(same_chunk, (bi <= bj) if z else (bi >= bj)), 1.0, 0.0).astype(BF16)
        for bb in range(nbat):
            bcum = _dot_exact_lhs(tri, lg[bb, :, z * hk:(z + 1) * hk])
            for c in range(nc):
                r = slice(c * CHUNK, (c + 1) * CHUNK)
                last = c * CHUNK if z else (c + 1) * CHUNK - 1
                bs.append(bb), zs.append(z), cs.append(c)
                qs.append(q[bb, r, :]), ks.append(k[bb, r, :]), vs.append(v[bb, r, :])
                bcs.append(bcum[r, :]), bts.append(bcum[last:last + 1, :])

    heads = range(GLA_HEADS)
    hd = lambda x, h, w: x[:, h * w:(h + 1) * w]
    zero_k = jnp.zeros((CHUNK, hk), BF16)
    zero_v = jnp.zeros((CHUNK, GLA_DV), BF16)
    qds = _stage(lambda q, b_: q * (GLA_DK ** -0.5) * jnp.exp(b_), qs, bcs)
    kinv = _stage(lambda k, b_: (k * jnp.exp(-b_)).astype(BF16), ks, bcs)
    kbs = _stage(lambda x: jnp.concatenate([jnp.where(klane == h, x, zero_k) for h in heads], axis=0), kinv)
    a_s = _stage(lambda z, qd, kb_: jnp.where(a_incl[z], _dot_nt(qd, kb_), 0.0), zs, qds, kbs)
    vbs = _stage(lambda v: jnp.concatenate(
        [jnp.concatenate([hd(v, j, GLA_DV).astype(BF16) if j == h else zero_v for j in heads], axis=1)
         for h in heads], axis=0), vs)
    ois = _stage(_dot, a_s, vbs)
    kes = _stage(lambda k, b_, bt: k * jnp.exp(bt - b_), ks, bcs, bts)
    upds = _stage(lambda v, ke: [_dot_tn(hd(v, h, GLA_DV), hd(ke, h, GLA_DK)) for h in heads], vs, kes)
    ges = _stage(jnp.exp, bts)

    chain = {key: i for i, key in enumerate(zip(bs, zs, cs))}
    state = {(bb, z): [s_[bb, h] for h in heads] for z, s_ in enumerate((sf, sb)) for bb in range(nbat)}
    outs = (of, ob)
    for step in range(nc):
        for z in range(2):
            c = nc - 1 - step if z else step
            for bb in range(nbat):
                i = chain[(bb, z, c)]
                st = state[(bb, z)]
                inter = [_dot_nt(hd(qds[i], h, GLA_DK), st[h]) for h in heads]
                outs[z][bb, c * CHUNK:(c + 1) * CHUNK, :] = ois[i] + jnp.concatenate(inter, axis=1)
                state[(bb, z)] = [st[h] * hd(ges[i], h, GLA_DK) + upds[i][h] for h in heads]
    for (bb, z), st in state.items():
        for h in heads:
            (sf, sb)[z][bb, h] = st[h]


def _gla_scan(gq, gk, gv, lg, n_lat, n_ctx):
    b, t, hk = gq.shape
    hv = gv.shape[-1]
    blk = _scan_blk(n_lat, n_ctx, GLA_SCAN_BLOCK)
    nlb, ncb = n_lat // blk, n_ctx // blk
    fwd, bwd = _scan_maps(nlb, ncb)
    nbat = GLA_SCAN_BATCH if b % GLA_SCAN_BATCH == 0 else 1
    spec = lambda n, m: pl.BlockSpec((nbat, blk, n), lambda i, j: (i, m(j), 0))
    out = jax.ShapeDtypeStruct((b, t, hv), F32)
    return pl.pallas_call(
        _gla_kernel,
        grid=(b // nbat, nlb + ncb),
        in_specs=[spec(hk, fwd), spec(hk, fwd), spec(hv, fwd), spec(2 * hk, fwd),
                  spec(hk, bwd), spec(hk, bwd), spec(hv, bwd), spec(2 * hk, bwd)],
        out_specs=[spec(hv, fwd), spec(hv, bwd)],
        out_shape=[out, out],
        scratch_shapes=[pltpu.VMEM((nbat, GLA_HEADS, GLA_DV, GLA_DK), F32)] * 2,
        compiler_params=_cparams(("parallel", "arbitrary")),
        name="gla_scan",
    )(gq, gk, gv, lg, gq, gk, gv, lg)


def _cast_kernel(x_ref, o_ref):
    o_ref[...] = x_ref[...].astype(o_ref.dtype)


def _to_bf16(w):
    depth, r, c = w.shape
    tr = min(r, CAST_BLOCK_BYTES // (4 * c))
    assert r % tr == 0
    spec = pl.BlockSpec((None, tr, c), lambda l, i: (l, i, 0))
    return pl.pallas_call(
        _cast_kernel,
        grid=(depth, r // tr),
        in_specs=[spec],
        out_specs=spec,
        out_shape=jax.ShapeDtypeStruct(w.shape, BF16),
        compiler_params=_cparams(("parallel", "parallel")),
        name="cast_bf16",
    )(w)


def _head_norm_gate(o_fwd, o_bwd, g, on_ref, n_heads):
    x = o_fwd + o_bwd
    outs = []
    for h in range(n_heads):
        s = slice(h * LANES, (h + 1) * LANES)
        outs.append(_rms(x[:, s], on_ref[...]) * _silu(g[:, s]))
    return jnp.concatenate(outs, axis=-1)


def _merge_mlp_kernel(*refs, n_att, n_stream, nlb, final):
    atts, parts = refs[:n_att], refs[n_att:n_att + 3]
    n_parts = n_att + 3 + n_stream
    stream = refs[n_att + 3:n_parts]
    on_ref, wo_ref, mod_ref, n2_ref, w1_ref, w2_ref = refs[n_parts:n_parts + 6]
    o_ref = refs[-1]
    nbat, tm = o_ref.shape[0], o_ref.shape[1]
    batch = range(nbat)
    rows = lambda xs: jnp.concatenate(xs, axis=0)
    part = lambda x, bb: x[bb * tm:(bb + 1) * tm]
    is_ctx = pl.program_id(1) >= nlb
    mods = [mod_ref[bb] for bb in batch]
    ys = rows([_head_norm_gate(*(p[bb] for p in parts), on_ref, GLA_HEADS if n_att else GDN_HEADS) for bb in batch])
    if n_att:
        att = rows([jnp.where(is_ctx, atts[1][bb], atts[0][bb]) if n_att == 2 else atts[0][bb] for bb in batch])
        na = att.shape[-1]
        o = jnp.dot(att, wo_ref[0:na, :], preferred_element_type=F32) + _dot(ys, wo_ref[na:, :])
    else:
        o = _dot(ys, wo_ref[...])
    x0 = [jnp.where(is_ctx, stream[1][bb], stream[0][bb]) if len(stream) == 2 else stream[0][bb] for bb in batch]
    hs = [x0[bb] + mods[bb][2:3] * part(o, bb) for bb in batch]
    a = rows([(_rms(hs[bb], n2_ref[...]) * (1.0 + mods[bb][4:5]) + mods[bb][3:4]).astype(BF16) for bb in batch])
    ff = w1_ref.shape[1]
    hc = min(ff, MLP_HIDDEN_CHUNK)
    acc = None
    for c in range(ff // hc):
        u = jnp.maximum(jnp.dot(a, w1_ref[:, c * hc:(c + 1) * hc], preferred_element_type=F32), 0.0)
        t = _dot(u * u, w2_ref[c * hc:(c + 1) * hc, :])
        acc = t if acc is None else acc + t
    for bb in batch:
        out = hs[bb] + mods[bb][5:6] * part(acc, bb)
        if final:
            out = _rms(out, refs[n_parts + 6][...])
        o_ref[bb] = out


def _merge_mlp(atts, parts, on, wo, stream, mod, n2, w1, w2, layer, tm, n_lat, n_tok, final_norm=None):
    b, d = stream[0].shape[0], stream[0].shape[-1]
    nlb = n_lat // tm
    nbat = MERGE_BATCH if b % MERGE_BATCH == 0 else 1
    tok = lambda n: pl.BlockSpec((nbat, tm, n), lambda i, j: (i, j, 0))
    once = lambda a: pl.BlockSpec(a.shape, lambda *_: (0,) * a.ndim, pipeline_mode=pl.Buffered(1))
    slab = lambda a: pl.BlockSpec((None,) + a.shape[1:], lambda *_: (layer, 0, 0), pipeline_mode=pl.Buffered(1))
    lat_ctx = (lambda i, j: (i, jnp.minimum(j, nlb - 1), 0), lambda i, j: (i, jnp.maximum(j - nlb, 0), 0))
    stream_specs = ([tok(d)] if len(stream) == 1 else [pl.BlockSpec((nbat, tm, d), m) for m in lat_ctx])
    args = [*atts, *parts, *stream, on, wo, mod, n2, w1, w2]
    specs = [pl.BlockSpec((nbat, tm, a.shape[-1]), m) for a, m in zip(atts, lat_ctx)] + [
        tok(p.shape[-1]) for p in parts] + stream_specs + [
        _full(on.shape), once(wo),
        pl.BlockSpec((nbat, None, N_MOD, d), lambda i, j: (i, j // nlb, 0, 0)),
        _full(n2.shape), slab(w1), slab(w2)]
    if final_norm is not None:
        args.append(final_norm)
        specs.append(_full(final_norm.shape))
    return pl.pallas_call(
        functools.partial(_merge_mlp_kernel, n_att=len(atts), n_stream=len(stream), nlb=nlb,
                          final=final_norm is not None),
        grid=(b // nbat, n_tok // tm),
        in_specs=specs,
        out_specs=tok(d),
        out_shape=jax.ShapeDtypeStruct((b, n_tok, d), F32),
        compiler_params=_cparams(("parallel", "parallel")),
        name="merge_mlp",
    )(*args)


_O_QKV = 2 * GDN_HEADS * GDN_DK + GDN_HEADS * GDN_DV
_O_QK = 2 * GDN_HEADS * GDN_DK
_O_G = GDN_HEADS * GDN_DV
_O_N = _O_QKV + _O_G + LANES
HALO = 8


def _odd_proj_kernel(h_ref, hp_ref, hn_ref, mod_ref, n1_ref, win_ref, cw_ref, e_ref, et_ref, rs_ref,
                     al_ref, dt_ref, q_out, k_out, v_out, g_out, gt_out, *, nlb, nb):
    j = pl.program_id(1)
    mod = mod_ref[...]
    n1 = n1_ref[...]
    pre = lambda x: (_rms(x, n1) * (1.0 + mod[1:2]) + mod[0:1]).astype(BF16)
    a = pre(h_ref[...])
    first = jnp.logical_or(j == 0, j == nlb)
    last = jnp.logical_or(j == nlb - 1, j == nb - 1)
    zero_halo = jnp.zeros((HALO, a.shape[1]), BF16)
    a_ext = jnp.concatenate([a, jnp.where(last, zero_halo, pre(hn_ref[...])),
                             jnp.where(first, zero_halo, pre(hp_ref[...]))], axis=0)
    tm = a.shape[0]
    te = tm + 2 * HALO
    hq = GDN_HEADS * GDN_DK
    gw = hq
    mm = lambda lhs, c0, c1: jnp.dot(lhs, win_ref[:, c0:c1], preferred_element_type=F32)
    project = lambda g: mm(a_ext, g * gw, (g + 1) * gw)

    pending = project(0)
    for g in range(_O_QKV // gw):
        xe = pending
        if (g + 1) * gw < _O_QKV:
            pending = project(g + 1)
        else:
            y_tail = mm(a, _O_QKV, _O_N)
        cw = cw_ref[:, g * gw:(g + 1) * gw]
        s = _silu(cw[0:1] * pltpu.roll(xe, 1, axis=0)[:tm] + cw[1:2] * xe[:tm]
                  + cw[2:3] * pltpu.roll(xe, te - 1, axis=0)[:tm])
        if g < 2:
            ss = _dot(s * s, e_ref[g * gw:(g + 1) * gw, :])
            r = lax.rsqrt(ss + EPS) * rs_ref[...]
            et = et_ref[:, g * gw:(g + 1) * gw]
            rf = sum(jnp.dot(p, et, preferred_element_type=F32) for p in _split(r, 2))
            (q_out, k_out)[g][...] = s * rf
        else:
            v_out[:, (g - 2) * gw:(g - 1) * gw] = s
    g_out[...] = y_tail[:, 0:_O_G]
    t = y_tail[:, _O_G:]
    log_a = -jnp.exp(al_ref[...]) * _softplus(t + dt_ref[...])
    lane = lax.broadcasted_iota(jnp.int32, t.shape, 1)
    gt_out[...] = jnp.where(lane < 2 * GDN_HEADS, log_a, _sigmoid(t))


def _odd_project(h, mod, n1, w_in, conv_w, a_log, dt_bias, tm, n_lat):
    b, t, d = h.shape
    nlb, nb = n_lat // tm, t // tm
    win = jnp.concatenate([w_in, jnp.zeros((d, _O_N - w_in.shape[1]), F32)], axis=1).astype(BF16)
    e = (jnp.arange(_O_QK)[:, None] // GDN_DK == jnp.arange(LANES)[None, :]).astype(BF16)
    et = e.T
    rs = jnp.concatenate([jnp.full((GDN_HEADS,), GDN_DK ** -0.5, F32), jnp.ones((LANES - GDN_HEADS,), F32)])[None, :]
    padl = lambda v: jnp.concatenate([v.reshape(-1), jnp.zeros((LANES - v.size,), F32)])[None, :]
    al, dt = padl(a_log), padl(dt_bias)
    tok = lambda n: pl.BlockSpec((None, tm, n), lambda i, j: (i, j, 0))
    hb = tm // HALO
    outs = [GDN_HEADS * GDN_DK, GDN_HEADS * GDN_DK, GDN_HEADS * GDN_DV, _O_G, LANES]
    return pl.pallas_call(
        functools.partial(_odd_proj_kernel, nlb=nlb, nb=nb),
        grid=(b, nb),
        in_specs=[tok(d),
                  pl.BlockSpec((None, HALO, d), lambda i, j: (i, jnp.maximum(j * hb - 1, 0), 0)),
                  pl.BlockSpec((None, HALO, d), lambda i, j: (i, jnp.minimum((j + 1) * hb, t // HALO - 1), 0)),
                  pl.BlockSpec((None, None, N_MOD, d), lambda i, j: (i, j // nlb, 0, 0)),
                  _full(n1.shape), _full(win.shape), _full(conv_w.shape), _full(e.shape), _full(et.shape),
                  _full(rs.shape), _full(al.shape), _full(dt.shape)],
        out_specs=[tok(n) for n in outs],
        out_shape=[jax.ShapeDtypeStruct((b, t, n), F32) for n in outs],
        compiler_params=_cparams(("parallel", "parallel")),
        name="odd_proj",
    )(h, h, h, mod, n1, win, conv_w, e, et, rs, al, dt)


def _gdn_kernel(qf, kf, vf, gcf, grf, qb, kb, vb, gcb, grb, of, ob, s_ref):
    @pl.when(pl.program_id(2) == 0)
    def _():
        s_ref[...] = jnp.zeros_like(s_ref)

    blk = qf.shape[0]
    nc = blk // CHUNK
    nh = qf.shape[1] // GDN_DK
    i32 = jnp.int32
    ri = lax.broadcasted_iota(i32, (CHUNK, CHUNK), 0)
    ci = lax.broadcasted_iota(i32, (CHUNK, CHUNK), 1)
    same = [(ri // w) == (ci // w) for w in (2, 4, 8, 16, 32)]
    lvl = [same[0]] + [jnp.logical_and(same[i], jnp.logical_not(same[i - 1])) for i in range(1, 5)]
    lvl.append(jnp.logical_not(same[4]))
    eye = jnp.where(ri == ci, 1.0, 0.0).astype(F32)
    incl = (ri >= ci, ri <= ci)
    strict = (ri > ci, ri < ci)
    bi = lax.broadcasted_iota(i32, (blk, blk), 0)
    bj = lax.broadcasted_iota(i32, (blk, blk), 1)
    same_chunk = (bi // CHUNK) == (bj // CHUNK)

    zs, cs, hs, qs, ks, vs, gbs, bbs, grows, brows, gls = ([] for _ in range(11))
    sq = (CHUNK, CHUNK)
    for z, (q, k, v, gc, gr) in enumerate(((qf, kf, vf, gcf, grf), (qb, kb, vb, gcb, grb))):
        tri = jnp.where(jnp.logical_and(same_chunk, (bi <= bj) if z else (bi >= bj)), 1.0, 0.0).astype(BF16)
        gcv, grv = gc[...], gr[...]
        cum_col = _dot_exact_lhs(tri, gcv)
        cum_row = _dot_exact_lhs(tri, grv, nt=True)
        for c in range(nc):
            r = slice(c * CHUNK, (c + 1) * CHUNK)
            last = c * CHUNK if z else (c + 1) * CHUNK - 1
            for hh in range(nh):
                idx = nh * z + hh
                zs.append(z), cs.append(c), hs.append(hh)
                qs.append(q[r, hh * GDN_DK:(hh + 1) * GDN_DK])
                ks.append(k[r, hh * GDN_DK:(hh + 1) * GDN_DK])
                vs.append(v[r, hh * GDN_DV:(hh + 1) * GDN_DV])
                gbs.append(jnp.broadcast_to(cum_col[r, idx:idx + 1], sq))
                bbs.append(jnp.broadcast_to(gcv[r, 2 * nh + idx:2 * nh + idx + 1], sq))
                grows.append(cum_row[idx:idx + 1, r])
                brows.append(grv[2 * nh + idx:2 * nh + idx + 1, r])
                gls.append(cum_col[last:last + 1, idx:idx + 1])

    decays = _stage(lambda z, gb, gr_: jnp.where(incl[z], jnp.exp(jnp.where(incl[z], gb - gr_, 0.0)), 0.0),
                    zs, gbs, grows)
    qkk = _stage(lambda q, k: _dot_nt(jnp.concatenate([q, k], axis=0), k), qs, ks)
    ns = _stage(lambda z, x, bb, d_: jnp.where(strict[z], x[CHUNK:] * bb * d_, 0.0), zs, qkk, bbs, decays)
    ts = _stage(lambda n: eye - jnp.where(lvl[0], n, 0.0), ns)
    for off in lvl[1:]:
        xs = _stage(lambda t, n: _dot(t, jnp.where(off, n, 0.0)), ts, ns)
        ts = _stage(lambda t, x: t - _dot(x, t), ts, xs)
    egs = _stage(jnp.exp, gbs)
    uws = _stage(lambda t, br, v, k, e_: _dot(t * br, jnp.concatenate([v, k * e_], axis=-1)),
                 ts, brows, vs, ks, egs)
    qks = _stage(lambda x, d_: x[:CHUNK] * d_, qkk, decays)
    kes = _stage(lambda k, gl, gb: k * jnp.exp(gl - gb), ks, gls, gbs)
    kuws = _stage(_dot_tn, kes, uws)
    quws = _stage(_dot, qks, uws)
    lhs = _stage(lambda q, e_, qu, ku: jnp.concatenate([q * e_ - qu[:, GDN_DV:], -ku[:, GDN_DV:]], axis=0),
                 qs, egs, quws, kuws)
    ges = _stage(jnp.exp, gls)

    chain = {(z, c, hh): i for i, (z, c, hh) in enumerate(zip(zs, cs, hs))}
    state = {(z, hh): s_ref[z, hh] for z in range(2) for hh in range(nh)}
    outs = (of, ob)
    for step in range(nc):
        for z in range(2):
            c = nc - 1 - step if z else step
            for hh in range(nh):
                i = chain[(z, c, hh)]
                s = state[(z, hh)]
                r = _dot(lhs[i], s)
                outs[z][c * CHUNK:(c + 1) * CHUNK, hh * GDN_DV:(hh + 1) * GDN_DV] = (
                    r[:CHUNK] + quws[i][:, :GDN_DV])
                state[(z, hh)] = s * ges[i] + (kuws[i][:, :GDN_DV] + r[CHUNK:])
    for (z, hh), s in state.items():
        s_ref[z, hh] = s


def _gdn_scan(q, k, v, gates, n_lat, n_ctx):
    b, t, _ = q.shape
    nh = GDN_SCAN_HEADS
    ng = GDN_HEADS // nh
    g = gates[..., :4 * GDN_HEADS].reshape(b, t, 2, 2, ng, nh)
    gcol = g.transpose(0, 4, 1, 2, 3, 5).reshape(b, ng, t, 4 * nh)
    grow = gcol.transpose(0, 1, 3, 2)
    blk = _scan_blk(n_lat, n_ctx, GDN_SCAN_BLOCK)
    nlb, ncb = n_lat // blk, n_ctx // blk
    fwd, bwd = _scan_maps(nlb, ncb)
    d = None
    tok = lambda n, m: pl.BlockSpec((d, blk, n), lambda i, h, j: (i, m(j), h))
    gcs = lambda m: pl.BlockSpec((d, d, blk, 4 * nh), lambda i, h, j: (i, h, m(j), 0))
    grs = lambda m: pl.BlockSpec((d, d, 4 * nh, blk), lambda i, h, j: (i, h, 0, m(j)))
    out = jax.ShapeDtypeStruct((b, t, GDN_HEADS * GDN_DV), F32)
    dk2, dv2 = nh * GDN_DK, nh * GDN_DV
    return pl.pallas_call(
        _gdn_kernel,
        grid=(b, ng, nlb + ncb),
        in_specs=[tok(dk2, fwd), tok(dk2, fwd), tok(dv2, fwd), gcs(fwd), grs(fwd),
                  tok(dk2, bwd), tok(dk2, bwd), tok(dv2, bwd), gcs(bwd), grs(bwd)],
        out_specs=[tok(dv2, fwd), tok(dv2, bwd)],
        out_shape=[out, out],
        scratch_shapes=[pltpu.VMEM((2, nh, GDN_DK, GDN_DV), F32)],
        compiler_params=_cparams(("parallel", "parallel", "arbitrary")),
        name="gdn_scan",
    )(q, k, v, gcol, grow, q, k, v, gcol, grow)


def kernel(x, c, ctx, c_ctx, ada_w, ada_b, norm1_w, norm2_w, mlp_w1, mlp_w2, even_w_in, mla_q_norm,
           mla_w_uq, mla_kv_norm, mla_w_ukv, gla_gate_up, gla_gate_bias, gla_o_norm, even_w_out,
           gdn_w_in, gdn_conv_w, gdn_a_log, gdn_dt_bias, gdn_o_norm, gdn_w_out, final_norm):
    b, n_lat, d = x.shape
    n_ctx = ctx.shape[1]
    depth = ada_w.shape[0]
    tm = 256 if (n_lat % 256 == 0 and n_ctx % 256 == 0) else 128
    assert n_lat % tm == 0 and n_ctx % tm == 0 and n_lat % n_ctx == 0 and n_lat % GRID_W == 0

    rows = -(-(b + 1) // 8) * 8
    cc = jnp.concatenate([c, c_ctx[None, :], jnp.zeros((rows - b - 1, d), F32)], axis=0)
    mods = _ada_all(cc, ada_w, ada_b).reshape(depth, rows, N_MOD, d)
    mod_tok = jnp.stack([mods[:, :b], jnp.broadcast_to(mods[:, b:b + 1], (depth, b, N_MOD, d))], axis=2)

    stream = (x, ctx)
    tabs = _rope_tables(n_lat, n_ctx)
    w1_bf, w2_bf = _to_bf16(mlp_w1), _to_bf16(mlp_w2)
    row = lambda v: v[None, :]
    for layer in range(depth):
        need_ctx = layer < depth - 1
        n_tok = n_lat + n_ctx if need_ctx else n_lat
        mod = mod_tok[layer]
        n1 = row(norm1_w[layer])
        i = layer // 2
        if layer % 2 == 0:
            ew = _even_weights(even_w_in[i], mla_w_uq[i], mla_w_ukv[i], gla_gate_up[i], gla_gate_bias[i])
            q, k, v, gq, gk, gv, gg, lg = _even_project(
                stream, mod, n1, ew, row(mla_q_norm[i]), row(mla_kv_norm[i]), tabs, tm, n_lat)
            atts = [a for a in _attention(q, k, v, n_lat, n_ctx, need_ctx) if a is not None]
            o_f, o_b = _gla_scan(gq, gk, gv, lg, n_lat, n_ctx)
            parts, on, wo = [o_f, o_b, gg], gla_o_norm[i], even_w_out[i]
        else:
            q, k, v, g, gates = _odd_project(stream[0], mod, n1, gdn_w_in[i], gdn_conv_w[i], gdn_a_log[i],
                                             gdn_dt_bias[i], tm, n_lat)
            o_f, o_b = _gdn_scan(q, k, v, gates, n_lat, n_ctx)
            atts, parts, on, wo = [], [o_f, o_b, g], gdn_o_norm[i], gdn_w_out[i]
        stream = (_merge_mlp(atts, parts, row(on), wo.astype(BF16), stream, mod, row(norm2_w[layer]), w1_bf, w2_bf,
                             layer, tm, n_lat, n_tok, final_norm=None if need_ctx else row(final_norm)),)
    return stream[0]
```
